```python
import math
import jax, jax.numpy as jnp
from jax import lax
import numpy as np

D_MODEL = 1024
BATCH = 16
SEQ = 2048
DEPTH = 1
DEC_BATCH = 32
DEC_SEQ = 1
PAST_LEN = 16384
PAGE_SIZE = 128

N_HEADS = 8
HEAD_DIM = 64
N_KV = 2
HPG = N_HEADS // N_KV
Q_W = N_HEADS * HEAD_DIM
KV_W = N_KV * HEAD_DIM
CMP_BLOCK = 32
CMP_STRIDE = 16
CMP_RATIO = CMP_BLOCK // CMP_STRIDE
CMP_HID = 64
SEL_BLOCK = 64
N_SEL = 16
SEL_BONUS = 1.0e4
WINDOW = 512
Q_BLOCK = 64
ROPE_THETA = 10000.0
D_RNN = D_MODEL
RG_BLOCKS = 16
RG_BW = D_RNN // RG_BLOCKS
RG_C = 8.0
CONV_W = 4
D_FF = ((8 * D_MODEL // 3 + 255) // 256) * 256
IN_W = Q_W + 6 * KV_W + 3 * N_HEADS + 2 * D_RNN + 2 * D_MODEL
ALPHA = (2.0 * DEPTH) ** 0.25
BETA = (8.0 * DEPTH) ** -0.25
LN_EPS = 1e-5

kernel_name = 'nsa_rglru_macaron_deepnorm_step'


def layer_norm(x, g, b):
    xf = x.astype(jnp.float32)
    mu = jnp.mean(xf, axis=-1, keepdims=True)
    var = jnp.mean(jnp.square(xf - mu), axis=-1, keepdims=True)
    return ((xf - mu) * lax.rsqrt(var + LN_EPS) * g.astype(jnp.float32) + b.astype(jnp.float32)).astype(x.dtype)


def ffn_half_step(x, g, b, w_gate, w_up, w_down):
    f = (jax.nn.silu(x @ w_gate) * (x @ w_up)) @ w_down
    return layer_norm(ALPHA * x + 0.5 * f, g, b)


def rope(x, pos):
    half = HEAD_DIM // 2
    freqs = ROPE_THETA ** (-jnp.arange(half, dtype=jnp.float32) / half)
    ang = pos.astype(jnp.float32)[:, None] * freqs[None, :]
    shape = (1, pos.shape[0]) + (1,) * (x.ndim - 3) + (half,)
    cos = jnp.cos(ang).reshape(shape)
    sin = jnp.sin(ang).reshape(shape)
    xf = x.astype(jnp.float32)
    x1, x2 = xf[..., :half], xf[..., half:]
    return jnp.concatenate([x1 * cos - x2 * sin, x2 * cos + x1 * sin], axis=-1).astype(x.dtype)


def masked_softmax(s, mask):
    s = jnp.where(mask, s.astype(jnp.float32), -jnp.inf)
    m = jnp.max(s, axis=-1, keepdims=True)
    m = jnp.where(jnp.isfinite(m), m, 0.0)
    e = jnp.where(mask, jnp.exp(s - m), 0.0)
    d = jnp.sum(e, axis=-1, keepdims=True)
    return e / jnp.where(d > 0, d, 1.0)


def split_in_proj(h, w_in, pos):
    B, T, _ = h.shape
    z = h @ w_in
    sizes = (Q_W, 6 * KV_W, 3 * N_HEADS, D_RNN, D_RNN, D_MODEL, D_MODEL)
    cuts = [int(c) for c in np.cumsum(sizes)[:-1]]
    q, kv, ng, xr, gr, ga, gb = jnp.split(z, cuts, axis=-1)
    q = q.reshape(B, T, N_KV, HPG, HEAD_DIM)
    kv = kv.reshape(B, T, 6, N_KV, HEAD_DIM)
    q_rot = rope(q, pos)
    cmp_kv = kv[:, :, 0:2]
    sel_kv = jnp.stack([rope(kv[:, :, 2], pos), kv[:, :, 3]], axis=2)
    win_kv = jnp.stack([rope(kv[:, :, 4], pos), kv[:, :, 5]], axis=2)
    gates = jax.nn.sigmoid(ng.astype(jnp.float32)).reshape(B, T, N_KV, HPG, 3)
    return q, q_rot, gates, cmp_kv, sel_kv, win_kv, xr, gr, ga, gb


def compress(rows, pe, w1, w2):
    B, L, G, Dh = rows.shape
    nch = L // CMP_STRIDE
    nc = nch - CMP_RATIO + 1
    ch = rows[:, :nch * CMP_STRIDE].reshape(B, nch, CMP_STRIDE, G, Dh)
    ch = jnp.swapaxes(ch, 2, 3).reshape(B, nch, G, CMP_STRIDE * Dh)
    w1r = w1.reshape(CMP_RATIO, CMP_STRIDE * Dh, CMP_HID)
    u = pe.reshape(-1) @ w1
    for r in range(CMP_RATIO):
        u = u + jnp.einsum('bcgf,fh->bcgh', ch[:, r:r + nc], w1r[r])
    return jax.nn.gelu(u) @ w2


def to_blocks(rows):
    B, L, G, Dh = rows.shape
    nsel = -(-L // SEL_BLOCK)
    rows = jnp.pad(rows, ((0, 0), (0, nsel * SEL_BLOCK - L), (0, 0), (0, 0)))
    return rows.reshape(B, nsel, SEL_BLOCK, G, Dh).transpose(0, 3, 1, 2, 4)


def cover_matrix(nc, nsel):
    start = jnp.arange(nc)[:, None] * CMP_STRIDE
    j = jnp.arange(nsel)[None, :]
    return ((start < (j + 1) * SEL_BLOCK) & (start + CMP_BLOCK > j * SEL_BLOCK)).astype(jnp.float32)


def nsa_attend(q_raw, q_rot, gates, q_pos, ck, cv, kb, vb, kw, vw, kw_pos):
    B, T = q_raw.shape[:2]
    scale = HEAD_DIM ** -0.5
    nc, nsel = ck.shape[1], kb.shape[2]
    c_end = jnp.arange(nc) * CMP_STRIDE + CMP_BLOCK - 1
    s_c = jnp.einsum('btghd,bcgd->bghtc', q_raw, ck) * scale
    p_c = masked_softmax(s_c, c_end[None, :] <= q_pos[:, None])
    o_c = jnp.einsum('bghtc,bcgd->btghd', p_c, cv)
    imp = jnp.einsum('bghtc,cj->bgtj', p_c, cover_matrix(nc, nsel))
    j = jnp.arange(nsel)[None, :]
    cur = (q_pos // SEL_BLOCK)[:, None]
    valid = j * SEL_BLOCK <= q_pos[:, None]
    forced = (j == 0) | (j == cur) | (j == cur - 1)
    score = jnp.where(valid, imp + jnp.where(forced, SEL_BONUS, 0.0), -jnp.inf)
    _, idx = lax.top_k(score, min(N_SEL, nsel))
    n = idx.shape[-1]
    bi = jnp.arange(B)[:, None, None, None]
    gi = jnp.arange(N_KV)[None, :, None, None]
    ks = kb[bi, gi, idx]
    vs = vb[bi, gi, idx]
    kpos = idx[..., None] * SEL_BLOCK + jnp.arange(SEL_BLOCK)
    s_s = jnp.einsum('btghd,bgtnkd->bghtnk', q_rot, ks) * scale
    m_s = (kpos <= q_pos[None, None, :, None, None])[:, :, None]
    p_s = masked_softmax(s_s.reshape(B, N_KV, HPG, T, n * SEL_BLOCK), m_s.reshape(B, N_KV, 1, T, n * SEL_BLOCK))
    o_s = jnp.einsum('bghtnk,bgtnkd->btghd', p_s.reshape(B, N_KV, HPG, T, n, SEL_BLOCK), vs)
    rel = q_pos[:, None] - kw_pos[None, :]
    m_w = (rel >= 0) & (rel <= WINDOW) & (kw_pos[None, :] >= 0)
    s_w = jnp.einsum('btghd,bsgd->bghts', q_rot, kw) * scale
    p_w = masked_softmax(s_w, m_w)
    o_w = jnp.einsum('bghts,bsgd->btghd', p_w, vw)
    return gates[..., 0:1] * o_c + gates[..., 1:2] * o_s + gates[..., 2:3] * o_w


def nsa_prompt(q_raw, q_rot, gates, cmp_kv, sel_kv, win_kv, pe_k, w1_k, w2_k, pe_v, w1_v, w2_v):
    B, T = q_raw.shape[:2]
    ck = compress(cmp_kv[:, :, 0], pe_k, w1_k, w2_k)
    cv = compress(cmp_kv[:, :, 1], pe_v, w1_v, w2_v)
    kb = to_blocks(sel_kv[:, :, 0])
    vb = to_blocks(sel_kv[:, :, 1])
    win_pad = jnp.pad(win_kv, ((0, 0), (WINDOW, 0), (0, 0), (0, 0), (0, 0)))

    def one_block(i):
        s0 = i * Q_BLOCK
        sl = lambda a: lax.dynamic_slice_in_dim(a, s0, Q_BLOCK, axis=1)
        wkv = lax.dynamic_slice_in_dim(win_pad, s0, WINDOW + Q_BLOCK, axis=1)
        q_pos = s0 + jnp.arange(Q_BLOCK)
        w_pos = s0 - WINDOW + jnp.arange(WINDOW + Q_BLOCK)
        return nsa_attend(sl(q_raw), sl(q_rot), sl(gates), q_pos, ck, cv, kb, vb,
                          wkv[:, :, 0], wkv[:, :, 1], w_pos)

    o = lax.map(one_block, jnp.arange(T // Q_BLOCK))
    return jnp.moveaxis(o, 0, 1).reshape(B, T, Q_W)


def nsa_sample(q_raw, q_rot, gates, q_pos, full_cmp, full_sel, win_all, win_pos,
               pe_k, w1_k, w2_k, pe_v, w1_v, w2_v):
    B, T = q_raw.shape[:2]
    ck = compress(full_cmp[:, :, 0], pe_k, w1_k, w2_k)
    cv = compress(full_cmp[:, :, 1], pe_v, w1_v, w2_v)
    kb = to_blocks(full_sel[:, :, 0])
    vb = to_blocks(full_sel[:, :, 1])
    o = nsa_attend(q_raw, q_rot, gates, q_pos, ck, cv, kb, vb, win_all[:, :, 0], win_all[:, :, 1], win_pos)
    return o.reshape(B, T, Q_W)


def rglru_branch(xr, conv_buf, h0, conv_w, conv_b, w_a, b_a, w_x, b_x, lam):
    B, T, _ = xr.shape
    xp = jnp.concatenate([conv_buf.astype(xr.dtype), xr], axis=1)
    xc = conv_b + sum(xp[:, k:k + T] * conv_w[k] for k in range(CONV_W))
    new_buf = xp[:, T:]
    xb = xc.reshape(B, T, RG_BLOCKS, RG_BW)
    r = jax.nn.sigmoid((jnp.einsum('btnd,nde->btne', xb, w_a).reshape(B, T, D_RNN) + b_a).astype(jnp.float32))
    i = jax.nn.sigmoid((jnp.einsum('btnd,nde->btne', xb, w_x).reshape(B, T, D_RNN) + b_x).astype(jnp.float32))
    log_a = -RG_C * r * jax.nn.softplus(-lam.astype(jnp.float32))
    a = jnp.exp(log_a)
    u = jnp.sqrt(-jnp.expm1(2.0 * log_a)) * i * xc.astype(jnp.float32)

    def step(h, au):
        a_t, u_t = au
        h = a_t * h + u_t
        return h, h

    h_last, hs = lax.scan(step, h0.astype(jnp.float32), (jnp.swapaxes(a, 0, 1), jnp.swapaxes(u, 0, 1)))
    return jnp.swapaxes(hs, 0, 1), h_last, new_buf


def mix_merge(x, o_attn, y_r, gr, ga, gb, w_br_attn, w_br_rnn, w_out, g, b):
    dt = x.dtype
    y_rnn = y_r.astype(dt) * jax.nn.gelu(gr)
    m = jax.nn.sigmoid(ga) * (o_attn.astype(dt) @ w_br_attn) + jax.nn.sigmoid(gb) * (y_rnn @ w_br_rnn)
    return layer_norm(ALPHA * x + m @ w_out, g, b)


def setup_inputs(seed: int = 0) -> dict:
    key = jax.random.key(seed)
    ks = iter(jax.random.split(key, 48))
    f32 = jnp.float32

    def nrm(shape, scale):
        return scale * jax.random.normal(next(ks), shape, f32)

    def w(shape, scale):
        return nrm((DEPTH,) + shape, scale)

    n_pages = PAST_LEN // PAGE_SIZE
    n_used = DEC_BATCH * n_pages
    n_pool = n_used + max(1, n_used // 4)
    wb = min(WINDOW, PAST_LEN)
    inp = {}
    inp['x_prompt'] = nrm((BATCH, SEQ, D_MODEL), 1.0)
    inp['x_sample'] = nrm((DEC_BATCH, DEC_SEQ, D_MODEL), 1.0)
    inp['cache_cmp_kv'] = w((n_pool, PAGE_SIZE, 2, N_KV, HEAD_DIM), 1.0)
    inp['cache_sel_kv'] = w((n_pool, PAGE_SIZE, 2, N_KV, HEAD_DIM), 1.0)
    inp['cache_win_kv'] = w((DEC_BATCH, wb, 2, N_KV, HEAD_DIM), 1.0)
    inp['state_conv'] = w((DEC_BATCH, CONV_W - 1, D_RNN), 1.0)
    inp['state_h'] = w((DEC_BATCH, D_RNN), 0.5)
    perm = jax.random.permutation(next(ks), n_pool)
    inp['page_table'] = perm[:n_used].reshape(DEC_BATCH, n_pages).astype(jnp.int32)
    inp['ffn1_w_gate'] = w((D_MODEL, D_FF), D_MODEL ** -0.5)
    inp['ffn1_w_up'] = w((D_MODEL, D_FF), D_MODEL ** -0.5)
    inp['ffn1_w_down'] = w((D_FF, D_MODEL), BETA * D_FF ** -0.5)
    inp['ln1_g'] = 1.0 + w((D_MODEL,), 0.02)
    inp['ln1_b'] = w((D_MODEL,), 0.02)
    inp['w_in'] = w((D_MODEL, IN_W), D_MODEL ** -0.5)
    inp['cmp_pe_k'] = w((CMP_BLOCK, HEAD_DIM), 0.1)
    inp['cmp_w1_k'] = w((CMP_BLOCK * HEAD_DIM, CMP_HID), (CMP_BLOCK * HEAD_DIM) ** -0.5)
    inp['cmp_w2_k'] = w((CMP_HID, HEAD_DIM), CMP_HID ** -0.5)
    inp['cmp_pe_v'] = w((CMP_BLOCK, HEAD_DIM), 0.1)
    inp['cmp_w1_v'] = w((CMP_BLOCK * HEAD_DIM, CMP_HID), (CMP_BLOCK * HEAD_DIM) ** -0.5)
    inp['cmp_w2_v'] = w((CMP_HID, HEAD_DIM), CMP_HID ** -0.5)
    inp['conv_w'] = w((CONV_W, D_RNN), CONV_W ** -0.5)
    inp['conv_b'] = w((D_RNN,), 0.02)
    inp['rg_w_a'] = w((RG_BLOCKS, RG_BW, RG_BW), RG_BW ** -0.5)
    inp['rg_b_a'] = w((D_RNN,), 0.02)
    inp['rg_w_x'] = w((RG_BLOCKS, RG_BW, RG_BW), RG_BW ** -0.5)
    inp['rg_b_x'] = w((D_RNN,), 0.02)
    u = jax.random.uniform(next(ks), (DEPTH, D_RNN), f32, minval=0.9, maxval=0.999)
    inp['rg_lam'] = jnp.log(u) - jnp.log1p(-u)
    inp['w_br_attn'] = w((Q_W, D_MODEL), BETA * Q_W ** -0.5)
    inp['w_br_rnn'] = w((D_RNN, D_MODEL), BETA * D_RNN ** -0.5)
    inp['w_out'] = w((D_MODEL, D_MODEL), BETA * D_MODEL ** -0.5)
    inp['ln2_g'] = 1.0 + w((D_MODEL,), 0.02)
    inp['ln2_b'] = w((D_MODEL,), 0.02)
    inp['ffn2_w_gate'] = w((D_MODEL, D_FF), D_MODEL ** -0.5)
    inp['ffn2_w_up'] = w((D_MODEL, D_FF), D_MODEL ** -0.5)
    inp['ffn2_w_down'] = w((D_FF, D_MODEL), BETA * D_FF ** -0.5)
    inp['ln3_g'] = 1.0 + w((D_MODEL,), 0.02)
    inp['ln3_b'] = w((D_MODEL,), 0.02)
    return inp


def reference(x_prompt, x_sample, cache_cmp_kv, cache_sel_kv, cache_win_kv, state_conv, state_h, page_table,
              ffn1_w_gate, ffn1_w_up, ffn1_w_down, ln1_g, ln1_b,
              w_in, cmp_pe_k, cmp_w1_k, cmp_w2_k, cmp_pe_v, cmp_w1_v, cmp_w2_v,
              conv_w, conv_b, rg_w_a, rg_b_a, rg_w_x, rg_b_x, rg_lam,
              w_br_attn, w_br_rnn, w_out, ln2_g, ln2_b,
              ffn2_w_gate, ffn2_w_up, ffn2_w_down, ln3_g, ln3_b):
    bp, tp, _ = x_prompt.shape
    bs, ts, _ = x_sample.shape
    pos_p = jnp.arange(tp)
    pos_s = PAST_LEN + jnp.arange(ts)
    wb = cache_win_kv.shape[2]
    win_pos_s = PAST_LEN - wb + jnp.arange(wb + ts)
    wbp = min(WINDOW, tp)
    cmp_p_l, cmp_s_l, sel_p_l, sel_s_l, win_p_l, win_s_l = [], [], [], [], [], []
    conv_p_l, conv_s_l, h_p_l, h_s_l = [], [], [], []
    xp, xs = x_prompt, x_sample
    for l in range(DEPTH):
        ffn1 = (ln1_g[l], ln1_b[l], ffn1_w_gate[l], ffn1_w_up[l], ffn1_w_down[l])
        ffn2 = (ln3_g[l], ln3_b[l], ffn2_w_gate[l], ffn2_w_up[l], ffn2_w_down[l])
        cmp_p = (cmp_pe_k[l], cmp_w1_k[l], cmp_w2_k[l], cmp_pe_v[l], cmp_w1_v[l], cmp_w2_v[l])
        rg_p = (conv_w[l], conv_b[l], rg_w_a[l], rg_b_a[l], rg_w_x[l], rg_b_x[l], rg_lam[l])
        mrg_p = (w_br_attn[l], w_br_rnn[l], w_out[l], ln2_g[l], ln2_b[l])

        hp = ffn_half_step(xp, *ffn1)
        q, qr, gt, ckv, skv, wkv, xr, gr, ga, gb = split_in_proj(hp, w_in[l], pos_p)
        o_attn = nsa_prompt(q, qr, gt, ckv, skv, wkv, *cmp_p)
        y_r, h_last, conv_new = rglru_branch(xr, jnp.zeros((bp, CONV_W - 1, D_RNN), xr.dtype),
                                             jnp.zeros((bp, D_RNN), jnp.float32), *rg_p)
        hp = mix_merge(hp, o_attn, y_r, gr, ga, gb, *mrg_p)
        xp = ffn_half_step(hp, *ffn2)
        cmp_p_l.append(ckv)
        sel_p_l.append(skv)
        win_p_l.append(wkv[:, tp - wbp:])
        conv_p_l.append(conv_new)
        h_p_l.append(h_last.astype(xp.dtype))

        hs = ffn_half_step(xs, *ffn1)
        q, qr, gt, ckv, skv, wkv, xr, gr, ga, gb = split_in_proj(hs, w_in[l], pos_s)
        past_cmp = cache_cmp_kv[l][page_table].reshape(bs, -1, 2, N_KV, HEAD_DIM).astype(ckv.dtype)
        past_sel = cache_sel_kv[l][page_table].reshape(bs, -1, 2, N_KV, HEAD_DIM).astype(skv.dtype)
        win_all = jnp.concatenate([cache_win_kv[l].astype(wkv.dtype), wkv], axis=1)
        o_attn = nsa_sample(q, qr, gt, pos_s, jnp.concatenate([past_cmp, ckv], axis=1),
                            jnp.concatenate([past_sel, skv], axis=1), win_all, win_pos_s, *cmp_p)
        y_r, h_last_s, conv_new_s = rglru_branch(xr, state_conv[l], state_h[l], *rg_p)
        hs = mix_merge(hs, o_attn, y_r, gr, ga, gb, *mrg_p)
        xs = ffn_half_step(hs, *ffn2)
        cmp_s_l.append(ckv)
        sel_s_l.append(skv)
        win_s_l.append(win_all[:, ts:])
        conv_s_l.append(conv_new_s)
        h_s_l.append(h_last_s.astype(xs.dtype))
    return (xp, xs, jnp.stack(cmp_p_l), jnp.stack(cmp_s_l), jnp.stack(sel_p_l), jnp.stack(sel_s_l),
            jnp.stack(win_p_l), jnp.stack(win_s_l), jnp.stack(conv_p_l), jnp.stack(conv_s_l),
            jnp.stack(h_p_l), jnp.stack(h_s_l))
```

```python
import functools
import math

import numpy as np
import jax
import jax.numpy as jnp
from jax import lax
from jax.experimental import pallas as pl
from jax.experimental.pallas import tpu as pltpu

F32 = jnp.float32
BF16 = jnp.bfloat16

N_HEADS = 8
HEAD_DIM = 64
N_KV = 2
HPG = N_HEADS // N_KV
Q_W = N_HEADS * HEAD_DIM
KV_W = N_KV * HEAD_DIM
CMP_BLOCK = 32
CMP_STRIDE = 16
CMP_RATIO = CMP_BLOCK // CMP_STRIDE
CMP_HID = 64
SEL_BLOCK = 64
N_SEL = 16
SEL_BONUS = 1.0e4
WINDOW = 512
ROPE_THETA = 10000.0
RG_BLOCKS = 16
RG_C = 8.0
CONV_W = 4
PAGE_SIZE = 128
LN_EPS = 1e-5
SCALE = HEAD_DIM ** -0.5
NEG_INF = float("-inf")

LANES = 128
SUBLANES = 8
MXU_DIM = 256
VMEM_LIMIT_BYTES = 56 * 1024 * 1024

ROW_W = 2 * KV_W
CHUNK_W = CMP_STRIDE * ROW_W
P_W = 2 * N_KV * CMP_RATIO * CMP_HID
GATE_PAD = LANES
PAGES_PER_STEP = 16


def _dot(a, b):
    return jnp.dot(a.astype(BF16), b.astype(BF16), preferred_element_type=F32)


def _dot_nt(a, b):
    return lax.dot_general(a.astype(BF16), b.astype(BF16), (((1,), (1,)), ((), ())),
                           preferred_element_type=F32)


def _dot_f32(a, b):
    return jnp.dot(a, b, preferred_element_type=F32, precision=lax.Precision.HIGHEST)


def _sigmoid(x):
    return 1.0 / (1.0 + jnp.exp(-x))


def _gelu(x):
    return 0.5 * x * (1.0 + jnp.tanh(math.sqrt(2.0 / math.pi) * (x + 0.044715 * (x * x * x))))


def _layer_norm(y, g, b):
    mu = jnp.mean(y, axis=-1, keepdims=True)
    yc = y - mu
    var = jnp.mean(yc * yc, axis=-1, keepdims=True)
    return yc * lax.rsqrt(var + LN_EPS) * g + b


def _softmax_parts(s, mask):
    s = jnp.where(mask, s, NEG_INF)
    m = jnp.max(s, axis=-1, keepdims=True)
    m = jnp.where(m > NEG_INF, m, 0.0)
    e = jnp.exp(s - m)
    d = jnp.sum(e, axis=-1, keepdims=True)
    return e, jnp.where(d > 0, d, 1.0)


def _const_spec(shape):
    nd = len(shape)
    return pl.BlockSpec(shape, lambda *_: (0,) * nd)


def _params(*sem):
    return pltpu.CompilerParams(dimension_semantics=sem, vmem_limit_bytes=VMEM_LIMIT_BYTES)


def _ffn_body(x_ref, wg_ref, wu_ref, wd_ref, g_ref, b_ref, o_ref, *, alpha):
    x = x_ref[...]
    xb = x.astype(BF16)
    gate = jnp.dot(xb, wg_ref[...], preferred_element_type=F32)
    up = jnp.dot(xb, wu_ref[...], preferred_element_type=F32)
    hmid = gate * _sigmoid(gate) * up
    f = _dot(hmid, wd_ref[...])
    o_ref[...] = _layer_norm(alpha * x + 0.5 * f, g_ref[...], b_ref[...])


def _ffn_half_step(x, wg, wu, wd, g, b, *, alpha, tm):
    n, d = x.shape
    ff = wg.shape[1]
    return pl.pallas_call(
        functools.partial(_ffn_body, alpha=alpha),
        grid=(n // tm,),
        in_specs=[pl.BlockSpec((tm, d), lambda i: (i, 0)),
                  _const_spec((d, ff)), _const_spec((d, ff)), _const_spec((ff, d)),
                  _const_spec((1, d)), _const_spec((1, d))],
        out_specs=pl.BlockSpec((tm, d), lambda i: (i, 0)),
        out_shape=jax.ShapeDtypeStruct((n, d), F32),
        compiler_params=_params("parallel"),
        name="ffn_half_step",
    )(x, wg, wu, wd, g, b)


def _rope(x, cos, sin):
    w = x.shape[1]
    reps = w // LANES
    c = jnp.tile(cos, (1, reps)) if reps > 1 else cos
    s = jnp.tile(sin, (1, reps)) if reps > 1 else sin
    lane = lax.broadcasted_iota(jnp.int32, x.shape, 1)
    first_half = (lane % HEAD_DIM) < (HEAD_DIM // 2)
    swapped = jnp.where(first_half, pltpu.roll(x, w - HEAD_DIM // 2, 1), pltpu.roll(x, HEAD_DIM // 2, 1))
    return x * c + swapped * s


def _inproj_body(h_ref, cos_ref, sin_ref, wa_ref, wb_ref,
                 qraw_ref, qrot_ref, cmp_ref, sel_ref, win_ref, gates_ref,
                 xr_ref, gr_ref, ga_ref, gb_ref):
    hb = h_ref[...].astype(BF16)
    cos = cos_ref[...]
    sin = sin_ref[...]
    za = jnp.dot(hb, wa_ref[...], preferred_element_type=F32)
    q = za[:, :Q_W]
    kv = za[:, Q_W:Q_W + 6 * KV_W]
    ng = za[:, Q_W + 6 * KV_W:]
    qraw_ref[...] = q
    qrot_ref[...] = _rope(q, cos, sin)
    cmp_ref[...] = kv[:, :2 * KV_W]
    sel_ref[...] = jnp.concatenate([_rope(kv[:, 2 * KV_W:3 * KV_W], cos, sin), kv[:, 3 * KV_W:4 * KV_W]], axis=1)
    win_ref[...] = jnp.concatenate([_rope(kv[:, 4 * KV_W:5 * KV_W], cos, sin), kv[:, 5 * KV_W:6 * KV_W]], axis=1)
    gates_ref[...] = _sigmoid(ng)
    zb = jnp.dot(hb, wb_ref[...], preferred_element_type=F32)
    d = xr_ref.shape[1]
    xr_ref[...] = zb[:, :d]
    gr_ref[...] = zb[:, d:2 * d]
    ga_ref[...] = zb[:, 2 * d:3 * d]
    gb_ref[...] = zb[:, 3 * d:]


def _in_proj(h, cos, sin, wa, wb, *, tm, table_blocks):
    n, d = h.shape
    wa_w, wb_w = wa.shape[1], wb.shape[1]
    widths = (Q_W, Q_W, ROW_W, ROW_W, ROW_W, GATE_PAD, d, d, d, d)
    row = lambda i: (i, 0)
    tab = lambda i: (i % table_blocks, 0)
    return pl.pallas_call(
        _inproj_body,
        grid=(n // tm,),
        in_specs=[pl.BlockSpec((tm, d), row), pl.BlockSpec((tm, LANES), tab), pl.BlockSpec((tm, LANES), tab),
                  _const_spec((d, wa_w)), _const_spec((d, wb_w))],
        out_specs=[pl.BlockSpec((tm, w), row) for w in widths],
        out_shape=[jax.ShapeDtypeStruct((n, w), F32) for w in widths],
        compiler_params=_params("parallel"),
        name="in_proj",
    )(h, cos, sin, wa, wb)


def _cmp_stage1_body(x_ref, w_ref, o_ref):
    o_ref[...] = _dot(x_ref[...], w_ref[...])


def _cmp_stage1(x, wbig):
    b, nch, _ = x.shape
    return pl.pallas_call(
        _cmp_stage1_body,
        grid=(b,),
        in_specs=[pl.BlockSpec((None, nch, CHUNK_W), lambda i: (i, 0, 0)), _const_spec((CHUNK_W, P_W))],
        out_specs=pl.BlockSpec((None, nch, P_W), lambda i: (i, 0, 0)),
        out_shape=jax.ShapeDtypeStruct((b, nch, P_W), F32),
        compiler_params=_params("parallel"),
        name="cmp_stage1",
    )(x, wbig)


def _cmp_paged_body(pt_ref, *refs):
    del pt_ref
    x_refs = refs[:PAGES_PER_STEP]
    w_ref, o_ref = refs[PAGES_PER_STEP:]
    x = jnp.concatenate([r[...] for r in x_refs], axis=0)
    o_ref[...] = _dot(x, w_ref[...])


def _cmp_stage1_paged(cache, page_table, wbig):
    b, n_pages = page_table.shape
    cpp = cache.shape[1]
    steps = n_pages // PAGES_PER_STEP
    nch = n_pages * cpp

    def page_spec(k):
        return pl.BlockSpec((None, cpp, CHUNK_W), lambda i, s, pt: (pt[i, s * PAGES_PER_STEP + k], 0, 0))

    return pl.pallas_call(
        _cmp_paged_body,
        grid_spec=pltpu.PrefetchScalarGridSpec(
            num_scalar_prefetch=1,
            grid=(b, steps),
            in_specs=[page_spec(k) for k in range(PAGES_PER_STEP)]
            + [pl.BlockSpec((CHUNK_W, P_W), lambda i, s, pt: (0, 0))],
            out_specs=pl.BlockSpec((None, PAGES_PER_STEP * cpp, P_W), lambda i, s, pt: (i, s, 0)),
        ),
        out_shape=jax.ShapeDtypeStruct((b, nch, P_W), F32),
        compiler_params=_params("parallel", "parallel"),
        name="cmp_stage1_paged",
    )(page_table, *([cache] * PAGES_PER_STEP), wbig)


def _cmp_finish_body(p_ref, pek_ref, w1k_ref, w2k_ref, pev_ref, w1v_ref, w2v_ref, o_ref, *, nc):
    p = p_ref[...]
    nch = p.shape[0]
    row = lax.broadcasted_iota(jnp.int32, (nch, CMP_HID), 0)
    outs = []
    for kv, (pe_ref, w1_ref, w2_ref) in enumerate(((pek_ref, w1k_ref, w2k_ref), (pev_ref, w1v_ref, w2v_ref))):
        pe = jnp.broadcast_to(pe_ref[...], (SUBLANES, pe_ref.shape[1]))
        bias = _dot(pe, w1_ref[...])[0:1]
        for g in range(N_KV):
            c0 = (kv * N_KV + g) * CMP_RATIO * CMP_HID
            u = bias + p[:, c0:c0 + CMP_HID]
            for r in range(1, CMP_RATIO):
                u = u + pltpu.roll(p[:, c0 + r * CMP_HID:c0 + (r + 1) * CMP_HID], nch - r, 0)
            out = _dot(_gelu(u), w2_ref[...])
            outs.append(jnp.where(row < nc, out, 0.0))
    o_ref[...] = jnp.concatenate(outs, axis=1)


def _cmp_finish(p, pe_k, w1_k, w2_k, pe_v, w1_v, w2_v, *, nc):
    b, nch, _ = p.shape
    flat = CMP_BLOCK * HEAD_DIM
    return pl.pallas_call(
        functools.partial(_cmp_finish_body, nc=nc),
        grid=(b,),
        in_specs=[pl.BlockSpec((None, nch, P_W), lambda i: (i, 0, 0)),
                  _const_spec((1, flat)), _const_spec((flat, CMP_HID)), _const_spec((CMP_HID, HEAD_DIM)),
                  _const_spec((1, flat)), _const_spec((flat, CMP_HID)), _const_spec((CMP_HID, HEAD_DIM))],
        out_specs=pl.BlockSpec((None, nch, ROW_W), lambda i: (i, 0, 0)),
        out_shape=jax.ShapeDtypeStruct((b, nch, ROW_W), F32),
        compiler_params=_params("parallel"),
        name="cmp_finish",
    )(p, pe_k, w1_k, w2_k, pe_v, w1_v, w2_v)


def _stack_heads(ref, g, rows=None):
    return jnp.concatenate([ref[:, (g * HPG + h) * HEAD_DIM:(g * HPG + h + 1) * HEAD_DIM] for h in range(HPG)], axis=0)


def _nsa_prompt_body(qraw_ref, qrot_ref, gates_ref, ckv_ref, sel_ref, win_ref, cover_ref, expand_ref, o_ref,
                     *, tq, nc):
    t_len = sel_ref.shape[0]
    nchp = ckv_ref.shape[0]
    nsel = cover_ref.shape[1]
    rows = HPG * tq
    s0 = pl.program_id(1) * tq
    qpos_r = s0 + lax.broadcasted_iota(jnp.int32, (rows, 1), 0) % tq
    qpos_t = s0 + lax.broadcasted_iota(jnp.int32, (tq, 1), 0)
    kpos = lax.broadcasted_iota(jnp.int32, (1, t_len), 1)
    cidx = lax.broadcasted_iota(jnp.int32, (1, nchp), 1)
    jblk = lax.broadcasted_iota(jnp.int32, (tq, nsel), 1)
    gates = gates_ref[...]
    causal = kpos <= qpos_r
    rel = qpos_r - kpos
    in_window = (rel >= 0) & (rel <= WINDOW)
    for g in range(N_KV):
        q_raw = _stack_heads(qraw_ref, g)
        q_rot = _stack_heads(qrot_ref, g)
        kcol = slice(g * HEAD_DIM, (g + 1) * HEAD_DIM)
        vcol = slice(KV_W + g * HEAD_DIM, KV_W + (g + 1) * HEAD_DIM)
        s_c = _dot_nt(q_raw, ckv_ref[:, kcol]) * SCALE
        m_c = (cidx * CMP_STRIDE + (CMP_BLOCK - 1) <= qpos_r) & (cidx < nc)
        e_c, d_c = _softmax_parts(s_c, m_c)
        p_c = e_c / d_c
        o_c = _dot(p_c, ckv_ref[:, vcol])
        p_sum = p_c[0:tq]
        for h in range(1, HPG):
            p_sum = p_sum + p_c[h * tq:(h + 1) * tq]
        imp = _dot_f32(p_sum, cover_ref[...])
        cur = qpos_t // SEL_BLOCK
        valid = jblk * SEL_BLOCK <= qpos_t
        forced = (jblk == 0) | (jblk == cur) | (jblk == cur - 1)
        score = jnp.where(valid, imp + jnp.where(forced, SEL_BONUS, 0.0), NEG_INF)
        rank = jnp.zeros((tq, nsel), jnp.int32)
        for i in range(nsel):
            s_i = score[:, i:i + 1]
            beats = (s_i > score) | ((s_i == score) & (i < jblk))
            rank = rank + beats.astype(jnp.int32)
        chosen = (rank < N_SEL).astype(F32)
        key_sel = _dot(chosen, expand_ref[...]) > 0.5
        key_sel = jnp.concatenate([key_sel] * HPG, axis=0)
        s_s = _dot_nt(q_rot, sel_ref[:, kcol]) * SCALE
        e_s, d_s = _softmax_parts(s_s, key_sel & causal)
        o_s = _dot(e_s, sel_ref[:, vcol]) / d_s
        s_w = _dot_nt(q_rot, win_ref[:, kcol]) * SCALE
        e_w, d_w = _softmax_parts(s_w, in_window)
        o_w = _dot(e_w, win_ref[:, vcol]) / d_w
        for h in range(HPG):
            hd = g * HPG + h
            r = slice(h * tq, (h + 1) * tq)
            o_ref[:, hd * HEAD_DIM:(hd + 1) * HEAD_DIM] = (
                gates[:, 3 * hd:3 * hd + 1] * o_c[r] + gates[:, 3 * hd + 1:3 * hd + 2] * o_s[r]
                + gates[:, 3 * hd + 2:3 * hd + 3] * o_w[r])


def _nsa_prompt(q_raw, q_rot, gates, ckv, sel, win, cover, expand, *, tq, nc):
    b, t, _ = q_raw.shape
    nchp = ckv.shape[1]
    nsel = cover.shape[1]
    tile = lambda w: pl.BlockSpec((None, tq, w), lambda i, j: (i, j, 0))
    whole = lambda r, w: pl.BlockSpec((None, r, w), lambda i, j: (i, 0, 0))
    return pl.pallas_call(
        functools.partial(_nsa_prompt_body, tq=tq, nc=nc),
        grid=(b, t // tq),
        in_specs=[tile(Q_W), tile(Q_W), tile(GATE_PAD), whole(nchp, ROW_W), whole(t, ROW_W), whole(t, ROW_W),
                  _const_spec((nchp, nsel)), _const_spec((nsel, t))],
        out_specs=tile(Q_W),
        out_shape=jax.ShapeDtypeStruct((b, t, Q_W), F32),
        compiler_params=_params("parallel", "arbitrary"),
        name="nsa_prompt",
    )(q_raw, q_rot, gates, ckv, sel, win, cover, expand)


def _softplus(x):
    return jnp.maximum(x, 0.0) + jnp.log1p(jnp.exp(-jnp.abs(x)))


def _expm1(x):
    u = jnp.exp(x)
    um1 = u - 1.0
    safe = (um1 != 0.0) & (um1 != -1.0)
    y = um1 * x / jnp.log(jnp.where(safe, u, 2.0))
    return jnp.where(um1 == 0.0, x, jnp.where(um1 == -1.0, -1.0, y))


def _rg_gates(xc, wgate_ref, ba, bx, lam):
    xcb = xc.astype(BF16)
    n_grp = wgate_ref.shape[0]
    gw = wgate_ref.shape[1]
    za, zx = [], []
    for k in range(n_grp):
        z = jnp.dot(xcb[:, k * gw:(k + 1) * gw], wgate_ref[k], preferred_element_type=F32)
        za.append(z[:, :gw])
        zx.append(z[:, gw:])
    r = _sigmoid(jnp.concatenate(za, axis=1) + ba)
    i = _sigmoid(jnp.concatenate(zx, axis=1) + bx)
    log_a = -RG_C * r * _softplus(-lam)
    a = jnp.exp(log_a)
    u = jnp.sqrt(-_expm1(2.0 * log_a)) * i * xc
    return a, u


def _merge(h_res, hs, gr, ga, gb, o_attn, wbra_ref, wbrr_ref, wout_ref, g2, b2, alpha):
    y_rnn = hs * _gelu(gr)
    m = _sigmoid(ga) * _dot(o_attn, wbra_ref[...]) + _sigmoid(gb) * _dot(y_rnn, wbrr_ref[...])
    return _layer_norm(alpha * h_res + _dot(m, wout_ref[...]), g2, b2)


def _shift_rows(x, d, fill):
    row = lax.broadcasted_iota(jnp.int32, x.shape, 0)
    return jnp.where(row < d, fill, pltpu.roll(x, d, 0))


def _mix_prompt_body(xr_ref, gr_ref, ga_ref, gb_ref, oat_ref, h1_ref, convw_ref, convb_ref, wgate_ref,
                     ba_ref, bx_ref, lam_ref, wbra_ref, wbrr_ref, wout_ref, g2_ref, b2_ref,
                     o_ref, hlast_ref, hc_ref, tail_ref, *, tt, alpha):
    @pl.when(pl.program_id(1) == 0)
    def _():
        hc_ref[...] = jnp.zeros_like(hc_ref)
        tail_ref[...] = jnp.zeros_like(tail_ref)

    x = xr_ref[...]
    prev = tail_ref[...]
    convw = convw_ref[...]
    row8 = lax.broadcasted_iota(jnp.int32, prev.shape, 0)
    xc = convb_ref[...] + convw[CONV_W - 1:CONV_W] * x
    for d in range(1, CONV_W):
        xs = pltpu.roll(x, d, 0)
        head = jnp.where(row8 < d, pltpu.roll(prev, d, 0), xs[0:SUBLANES])
        xs = jnp.concatenate([head, xs[SUBLANES:]], axis=0)
        xc = xc + convw[CONV_W - 1 - d:CONV_W - d] * xs
    tail_ref[...] = x[tt - SUBLANES:tt]

    a, u = _rg_gates(xc, wgate_ref, ba_ref[...], bx_ref[...], lam_ref[...])
    d = 1
    while d < tt:
        u = a * _shift_rows(u, d, 0.0) + u
        a = a * _shift_rows(a, d, 1.0)
        d *= 2
    hs = a * hc_ref[...] + u
    hc_ref[...] = hs[tt - 1:tt]
    hlast_ref[...] = hs[tt - 1:tt]
    o_ref[...] = _merge(h1_ref[...], hs, gr_ref[...], ga_ref[...], gb_ref[...], oat_ref[...],
                        wbra_ref, wbrr_ref, wout_ref, g2_ref[...], b2_ref[...], alpha)


def _mix_weight_specs(d, n_grp, gw):
    return [_const_spec((SUBLANES, d)), _const_spec((1, d)), _const_spec((n_grp, gw, 2 * gw)),
            _const_spec((1, d)), _const_spec((1, d)), _const_spec((1, d)),
            _const_spec((Q_W, d)), _const_spec((d, d)), _const_spec((d, d)),
            _const_spec((1, d)), _const_spec((1, d))]


def _mix_prompt(xr, gr, ga, gb, o_attn, h1, mixw, *, tt, alpha):
    b, t, d = xr.shape
    n_grp, gw = mixw[2].shape[:2]
    tile = lambda w: pl.BlockSpec((None, tt, w), lambda i, j: (i, j, 0))
    return pl.pallas_call(
        functools.partial(_mix_prompt_body, tt=tt, alpha=alpha),
        grid=(b, t // tt),
        in_specs=[tile(d), tile(d), tile(d), tile(d), tile(Q_W), tile(d)] + _mix_weight_specs(d, n_grp, gw),
        out_specs=[tile(d), pl.BlockSpec((None, 1, d), lambda i, j: (i, 0, 0))],
        out_shape=[jax.ShapeDtypeStruct((b, t, d), F32), jax.ShapeDtypeStruct((b, 1, d), F32)],
        scratch_shapes=[pltpu.VMEM((1, d), F32), pltpu.VMEM((SUBLANES, d), F32)],
        compiler_params=_params("parallel", "arbitrary"),
        name="mix_prompt",
    )(xr, gr, ga, gb, o_attn, h1, *mixw)


def _mix_sample_body(xr_ref, c0_ref, c1_ref, c2_ref, h0_ref, gr_ref, ga_ref, gb_ref, oat_ref, h1_ref,
                     convw_ref, convb_ref, wgate_ref, ba_ref, bx_ref, lam_ref, wbra_ref, wbrr_ref, wout_ref,
                     g2_ref, b2_ref, o_ref, hnew_ref, *, alpha):
    convw = convw_ref[...]
    xc = (convb_ref[...] + convw[0:1] * c0_ref[...] + convw[1:2] * c1_ref[...] + convw[2:3] * c2_ref[...]
          + convw[3:4] * xr_ref[...])
    a, u = _rg_gates(xc, wgate_ref, ba_ref[...], bx_ref[...], lam_ref[...])
    hs = a * h0_ref[...] + u
    hnew_ref[...] = hs
    o_ref[...] = _merge(h1_ref[...], hs, gr_ref[...], ga_ref[...], gb_ref[...], oat_ref[...],
                        wbra_ref, wbrr_ref, wout_ref, g2_ref[...], b2_ref[...], alpha)


def _mix_sample(xr, conv_rows, h0, gr, ga, gb, o_attn, h1, mixw, *, alpha):
    b, d = xr.shape
    n_grp, gw = mixw[2].shape[:2]
    full = lambda w: _const_spec((b, w))
    return pl.pallas_call(
        functools.partial(_mix_sample_body, alpha=alpha),
        grid=(1,),
        in_specs=[full(d)] * 8 + [full(Q_W), full(d)] + _mix_weight_specs(d, n_grp, gw),
        out_specs=[full(d), full(d)],
        out_shape=[jax.ShapeDtypeStruct((b, d), F32), jax.ShapeDtypeStruct((b, d), F32)],
        compiler_params=_params("arbitrary"),
        name="mix_sample",
    )(xr, *conv_rows, h0, gr, ga, gb, o_attn, h1, *mixw)


def _head_rows(row, g):
    parts = [row[:, (g * HPG + h) * HEAD_DIM:(g * HPG + h + 1) * HEAD_DIM] for h in range(HPG)]
    return jnp.concatenate(parts + [jnp.zeros((SUBLANES - HPG, HEAD_DIM), F32)], axis=0)


def _nsa_sample_select_body(qraw_ref, ckv_ref, cover_ref, oc_ref, idx_ref, *, nc, nsel, q_pos):
    nchp = ckv_ref.shape[0]
    nselp = cover_ref.shape[1]
    cidx = lax.broadcasted_iota(jnp.int32, (1, nchp), 1)
    lane = lax.broadcasted_iota(jnp.int32, (1, nselp), 1)
    lane_out = lax.broadcasted_iota(jnp.int32, (1, LANES), 1)
    qrow = qraw_ref[...]
    m_c = (cidx * CMP_STRIDE + (CMP_BLOCK - 1) <= q_pos) & (cidx < nc)
    cur = q_pos // SEL_BLOCK
    valid = (lane * SEL_BLOCK <= q_pos) & (lane < nsel)
    forced = (lane == 0) | (lane == cur) | (lane == cur - 1)
    oc_parts, idx_rows = [], []
    for g in range(N_KV):
        q = _head_rows(qrow, g)
        s_c = _dot_nt(q, ckv_ref[:, g * HEAD_DIM:(g + 1) * HEAD_DIM]) * SCALE
        e_c, d_c = _softmax_parts(s_c, m_c)
        p_c = e_c / d_c
        o_c = _dot(p_c, ckv_ref[:, KV_W + g * HEAD_DIM:KV_W + (g + 1) * HEAD_DIM])
        oc_parts += [o_c[h:h + 1] for h in range(HPG)]
        p_sum = jnp.sum(p_c[0:HPG], axis=0, keepdims=True)
        imp = _dot_f32(jnp.broadcast_to(p_sum, (SUBLANES, nchp)), cover_ref[...])[0:1]
        score = jnp.where(valid, imp + jnp.where(forced, SEL_BONUS, 0.0), NEG_INF)
        picked = jnp.zeros((1, LANES), jnp.int32)
        for r in range(N_SEL):
            m = jnp.max(score, axis=1, keepdims=True)
            j = jnp.min(jnp.where(score == m, lane, nselp), axis=1, keepdims=True)
            picked = jnp.where(lane_out == r, j, picked)
            score = jnp.where(lane == j, NEG_INF, score)
        idx_rows.append(picked)
    oc_ref[...] = jnp.concatenate(oc_parts, axis=1)
    idx_ref[...] = jnp.concatenate(idx_rows + [jnp.zeros((SUBLANES - N_KV, LANES), jnp.int32)], axis=0)


def _nsa_sample_select(q_raw, ckv, cover, *, nc, nsel, q_pos):
    b = q_raw.shape[0]
    nchp = ckv.shape[1]
    nselp = cover.shape[1]
    return pl.pallas_call(
        functools.partial(_nsa_sample_select_body, nc=nc, nsel=nsel, q_pos=q_pos),
        grid=(b,),
        in_specs=[pl.BlockSpec((None, 1, Q_W), lambda i: (i, 0, 0)),
                  pl.BlockSpec((None, nchp, ROW_W), lambda i: (i, 0, 0)), _const_spec((nchp, nselp))],
        out_specs=[pl.BlockSpec((None, 1, Q_W), lambda i: (i, 0, 0)),
                   pl.BlockSpec((None, SUBLANES, LANES), lambda i: (i, 0, 0))],
        out_shape=[jax.ShapeDtypeStruct((b, 1, Q_W), F32), jax.ShapeDtypeStruct((b, SUBLANES, LANES), jnp.int32)],
        compiler_params=_params("parallel"),
        name="nsa_sample_select",
    )(q_raw, ckv, cover)


def _nsa_sample_attend_body(idx_ref, pt_ref, qrot_ref, gates_ref, oc_ref, selnew_ref, winnew_ref, wincache_ref,
                            *refs, q_pos, n_past_blocks):
    del pt_ref
    blk_refs, o_ref = refs[:-1], refs[-1]
    b = pl.program_id(0)
    wb = wincache_ref.shape[0]
    qrow = qrot_ref[...]
    gates = gates_ref[...]
    oc = oc_ref[...]
    sel_new = selnew_ref[...]
    win_new = winnew_ref[...]
    blk_lane = lax.broadcasted_iota(jnp.int32, (1, SEL_BLOCK), 1)
    wpos = (q_pos - wb) + lax.broadcasted_iota(jnp.int32, (1, wb), 1)
    m_w = (q_pos - wpos <= WINDOW) & (wpos >= 0) & (wpos <= q_pos)
    cur = q_pos // SEL_BLOCK
    out_parts = []
    for g in range(N_KV):
        q = _head_rows(qrow, g)
        kcol = slice(g * HEAD_DIM, (g + 1) * HEAD_DIM)
        vcol = slice(KV_W + g * HEAD_DIM, KV_W + (g + 1) * HEAD_DIM)
        ks, vs, ms = [], [], []
        new_chosen = None
        for n in range(N_SEL):
            j = idx_ref[b, g * N_SEL + n]
            blk = blk_refs[g * N_SEL + n]
            ks.append(blk[:, kcol])
            vs.append(blk[:, vcol])
            ms.append(jnp.where(j < n_past_blocks, j * SEL_BLOCK + blk_lane, q_pos + 1))
            hit = j == cur
            new_chosen = hit if new_chosen is None else (new_chosen | hit)
        s_s = _dot_nt(q, jnp.concatenate(ks, axis=0)) * SCALE
        s_new = jnp.where(new_chosen, jnp.sum(q * sel_new[:, kcol], axis=1, keepdims=True) * SCALE, NEG_INF)
        s_s = jnp.where(jnp.concatenate(ms, axis=1) <= q_pos, s_s, NEG_INF)
        m = jnp.maximum(jnp.max(s_s, axis=1, keepdims=True), s_new)
        m = jnp.where(m > NEG_INF, m, 0.0)
        e_s = jnp.exp(s_s - m)
        e_new = jnp.exp(s_new - m)
        d_s = jnp.sum(e_s, axis=1, keepdims=True) + e_new
        d_s = jnp.where(d_s > 0, d_s, 1.0)
        o_s = (_dot(e_s, jnp.concatenate(vs, axis=0)) + e_new * sel_new[:, vcol]) / d_s
        s_w = jnp.where(m_w, _dot_nt(q, wincache_ref[:, kcol]) * SCALE, NEG_INF)
        s_wn = jnp.sum(q * win_new[:, kcol], axis=1, keepdims=True) * SCALE
        m = jnp.maximum(jnp.max(s_w, axis=1, keepdims=True), s_wn)
        e_w = jnp.exp(s_w - m)
        e_wn = jnp.exp(s_wn - m)
        d_w = jnp.sum(e_w, axis=1, keepdims=True) + e_wn
        o_w = (_dot(e_w, wincache_ref[:, vcol]) + e_wn * win_new[:, vcol]) / d_w
        for h in range(HPG):
            hd = g * HPG + h
            out_parts.append(gates[:, 3 * hd:3 * hd + 1] * oc[:, hd * HEAD_DIM:(hd + 1) * HEAD_DIM]
                             + gates[:, 3 * hd + 1:3 * hd + 2] * o_s[h:h + 1]
                             + gates[:, 3 * hd + 2:3 * hd + 3] * o_w[h:h + 1])
    o_ref[...] = jnp.concatenate(out_parts, axis=1)


def _nsa_sample_attend(idx, page_table, q_rot, gates, o_c, sel_new, win_new, win_cache, sel_blocks,
                       *, q_pos, n_past_blocks):
    b = q_rot.shape[0]
    wb = win_cache.shape[1]
    blocks_per_page = PAGE_SIZE // SEL_BLOCK

    def blk_spec(k):
        def index(i, idx_ref, pt_ref):
            j = jnp.minimum(idx_ref[i, k], n_past_blocks - 1)
            return (pt_ref[i, j // blocks_per_page] * blocks_per_page + j % blocks_per_page, 0, 0)
        return pl.BlockSpec((None, SEL_BLOCK, ROW_W), index)

    one = lambda w: pl.BlockSpec((None, 1, w), lambda i, idx_ref, pt_ref: (i, 0, 0))
    return pl.pallas_call(
        functools.partial(_nsa_sample_attend_body, q_pos=q_pos, n_past_blocks=n_past_blocks),
        grid_spec=pltpu.PrefetchScalarGridSpec(
            num_scalar_prefetch=2,
            grid=(b,),
            in_specs=[one(Q_W), one(GATE_PAD), one(Q_W), one(ROW_W), one(ROW_W),
                      pl.BlockSpec((None, wb, ROW_W), lambda i, idx_ref, pt_ref: (i, 0, 0))]
            + [blk_spec(k) for k in range(N_KV * N_SEL)],
            out_specs=one(Q_W),
        ),
        out_shape=jax.ShapeDtypeStruct((b, 1, Q_W), F32),
        compiler_params=_params("arbitrary"),
        name="nsa_sample_attend",
    )(idx, page_table, q_rot, gates, o_c, sel_new, win_new, win_cache, *([sel_blocks] * (N_KV * N_SEL)))


def _rope_tables(pos):
    half = HEAD_DIM // 2
    freqs = ROPE_THETA ** (-jnp.arange(half, dtype=F32) / half)
    ang = pos.astype(F32)[:, None] * freqs[None, :]
    cos, sin = jnp.cos(ang), jnp.sin(ang)
    reps = LANES // HEAD_DIM
    return jnp.tile(jnp.concatenate([cos, cos], axis=1), (1, reps)), jnp.tile(jnp.concatenate([-sin, sin], axis=1), (1, reps))


def _cover(nc, nsel, rows, cols):
    start = np.arange(rows)[:, None] * CMP_STRIDE
    j = np.arange(cols)[None, :]
    hit = (start < (j + 1) * SEL_BLOCK) & (start + CMP_BLOCK > j * SEL_BLOCK) & (np.arange(rows)[:, None] < nc) & (j < nsel)
    return jnp.asarray(hit.astype(np.float32))


def _cmp_chunk_weight(w1_k, w1_v):
    per = jnp.stack([w.reshape(CMP_RATIO, CMP_STRIDE, HEAD_DIM, CMP_HID) for w in (w1_k, w1_v)])
    per = jnp.repeat(per, N_KV, axis=0)
    eye = jnp.eye(2 * N_KV, dtype=F32)
    big = jnp.einsum("ab,arsdh->sadbrh", eye, per)
    return big.reshape(CHUNK_W, P_W).astype(BF16)


def _rg_gate_weight(w_a, w_x):
    nb, bw, _ = w_a.shape
    per = MXU_DIM // bw
    n_grp = nb // per
    eye = jnp.eye(per, dtype=F32)

    def group(w):
        w = w.reshape(n_grp, per, bw, bw)
        return jnp.einsum("pq,gpde->gpdqe", eye, w).reshape(n_grp, per * bw, per * bw)

    return jnp.concatenate([group(w_a), group(w_x)], axis=2).astype(BF16)


def kernel(x_prompt, x_sample, cache_cmp_kv, cache_sel_kv, cache_win_kv, state_conv, state_h, page_table, ffn1_w_gate, ffn1_w_up, ffn1_w_down, ln1_g, ln1_b, w_in, cmp_pe_k, cmp_w1_k, cmp_w2_k, cmp_pe_v, cmp_w1_v, cmp_w2_v, conv_w, conv_b, rg_w_a, rg_b_a, rg_w_x, rg_b_x, rg_lam, w_br_attn, w_br_rnn, w_out, ln2_g, ln2_b, ffn2_w_gate, ffn2_w_up, ffn2_w_down, ln3_g, ln3_b):
    bp, tp, d = x_prompt.shape
    bs, ts, _ = x_sample.shape
    depth = w_in.shape[0]
    d_rnn = conv_w.shape[2]
    n_pages = page_table.shape[1]
    past_len = n_pages * PAGE_SIZE
    assert ts == 1 and tp % CMP_STRIDE == 0 and tp % SEL_BLOCK == 0
    alpha = (2.0 * depth) ** 0.25
    wb = cache_win_kv.shape[2]
    wbp = min(WINDOW, tp)

    nch_p = tp // CMP_STRIDE
    nc_p = nch_p - CMP_RATIO + 1
    nsel_p = tp // SEL_BLOCK
    nch_s = past_len // CMP_STRIDE
    nc_s = nch_s - CMP_RATIO + 1
    nsel_s = -(-(past_len + ts) // SEL_BLOCK)
    n_past_blocks = past_len // SEL_BLOCK
    assert nsel_s >= N_SEL and n_pages % PAGES_PER_STEP == 0
    nselp_s = -(-nsel_s // LANES) * LANES

    tm = min(256, tp)
    tq = min(128, tp)
    tt = min(256, tp)

    cos_p, sin_p = _rope_tables(jnp.arange(tp))
    cos_s, sin_s = _rope_tables(jnp.full((bs,), past_len))
    cover_p = _cover(nc_p, nsel_p, nch_p, nsel_p)
    cover_s = _cover(nc_s, nsel_s, nch_s, nselp_s)
    expand_p = jnp.asarray((np.arange(tp)[None, :] // SEL_BLOCK == np.arange(nsel_p)[:, None]).astype(np.float32)).astype(BF16)

    xp = x_prompt.reshape(bp * tp, d)
    xs = x_sample.reshape(bs * ts, d)
    outs = [[] for _ in range(10)]
    for l in range(depth):
        row = lambda v: v[l].reshape(1, -1)
        ffn1 = (ffn1_w_gate[l].astype(BF16), ffn1_w_up[l].astype(BF16), ffn1_w_down[l].astype(BF16), row(ln1_g), row(ln1_b))
        ffn2 = (ffn2_w_gate[l].astype(BF16), ffn2_w_up[l].astype(BF16), ffn2_w_down[l].astype(BF16), row(ln3_g), row(ln3_b))
        n_a = Q_W + 6 * KV_W + 3 * N_HEADS
        w_a = jnp.pad(w_in[l][:, :n_a], ((0, 0), (0, GATE_PAD - 3 * N_HEADS))).astype(BF16)
        w_b = w_in[l][:, n_a:].astype(BF16)
        wbig = _cmp_chunk_weight(cmp_w1_k[l], cmp_w1_v[l])
        cmpw = (cmp_pe_k[l].reshape(1, -1), cmp_w1_k[l], cmp_w2_k[l], cmp_pe_v[l].reshape(1, -1), cmp_w1_v[l], cmp_w2_v[l])
        mixw = (jnp.pad(conv_w[l], ((0, SUBLANES - CONV_W), (0, 0))), row(conv_b), _rg_gate_weight(rg_w_a[l], rg_w_x[l]),
                row(rg_b_a), row(rg_b_x), row(rg_lam), w_br_attn[l].astype(BF16), w_br_rnn[l].astype(BF16),
                w_out[l].astype(BF16), row(ln2_g), row(ln2_b))

        h1 = _ffn_half_step(xp, *ffn1, alpha=alpha, tm=tm)
        q_raw, q_rot, cmp, sel, win, gates, xr, gr, ga, gb = _in_proj(h1, cos_p, sin_p, w_a, w_b, tm=tm, table_blocks=tp // tm)
        p1 = _cmp_stage1(cmp.reshape(bp, nch_p, CHUNK_W), wbig)
        ckv = _cmp_finish(p1, *cmpw, nc=nc_p)
        b3 = lambda a: a.reshape(bp, tp, a.shape[-1])
        o_attn = _nsa_prompt(b3(q_raw), b3(q_rot), b3(gates), ckv, b3(sel), b3(win), cover_p, expand_p, tq=tq, nc=nc_p)
        h2, h_last = _mix_prompt(b3(xr), b3(gr), b3(ga), b3(gb), o_attn, b3(h1), mixw, tt=tt, alpha=alpha)
        xp = _ffn_half_step(h2.reshape(bp * tp, d), *ffn2, alpha=alpha, tm=tm)
        outs[0].append(cmp.reshape(bp, tp, 2, N_KV, HEAD_DIM))
        outs[2].append(sel.reshape(bp, tp, 2, N_KV, HEAD_DIM))
        outs[4].append(b3(win)[:, tp - wbp:].reshape(bp, wbp, 2, N_KV, HEAD_DIM))
        outs[6].append(b3(xr)[:, tp - (CONV_W - 1):])
        outs[8].append(h_last.reshape(bp, d_rnn))

        h1s = _ffn_half_step(xs, *ffn1, alpha=alpha, tm=bs)
        q_raw, q_rot, cmp, sel, win, gates, xr, gr, ga, gb = _in_proj(h1s, cos_s, sin_s, w_a, w_b, tm=bs, table_blocks=1)
        p1 = _cmp_stage1_paged(cache_cmp_kv[l].reshape(-1, PAGE_SIZE // CMP_STRIDE, CHUNK_W), page_table, wbig)
        ckv = _cmp_finish(p1, *cmpw, nc=nc_s)
        b1 = lambda a: a.reshape(bs, 1, a.shape[-1])
        q_pos = past_len
        o_c, idx = _nsa_sample_select(b1(q_raw), ckv, cover_s, nc=nc_s, nsel=nsel_s, q_pos=q_pos)
        idx = idx[:, :N_KV, :N_SEL].reshape(bs, N_KV * N_SEL)
        o_attn = _nsa_sample_attend(idx, page_table, b1(q_rot), b1(gates), o_c, b1(sel), b1(win),
                                    cache_win_kv[l].reshape(bs, wb, ROW_W),
                                    cache_sel_kv[l].reshape(-1, SEL_BLOCK, ROW_W), q_pos=q_pos, n_past_blocks=n_past_blocks)
        conv_rows = [state_conv[l][:, k] for k in range(CONV_W - 1)]
        h2s, h_new = _mix_sample(xr, conv_rows, state_h[l], gr, ga, gb, o_attn.reshape(bs, Q_W), h1s, mixw, alpha=alpha)
        xs = _ffn_half_step(h2s, *ffn2, alpha=alpha, tm=bs)
        outs[1].append(cmp.reshape(bs, ts, 2, N_KV, HEAD_DIM))
        outs[3].append(sel.reshape(bs, ts, 2, N_KV, HEAD_DIM))
        win_all = jnp.concatenate([cache_win_kv[l], win.reshape(bs, ts, 2, N_KV, HEAD_DIM)], axis=1)
        outs[5].append(win_all[:, ts:])
        outs[7].append(jnp.concatenate([state_conv[l], xr[:, None, :]], axis=1)[:, ts:])
        outs[9].append(h_new)

    stacked = [jnp.stack(o) for o in outs]
    cmp_p, cmp_s, sel_p, sel_s, win_p, win_s, conv_p, conv_s, h_p, h_s = stacked
    return (xp.reshape(bp, tp, d), xs.reshape(bs, ts, d), cmp_p, cmp_s, sel_p, sel_s, win_p, win_s,
            conv_p, conv_s, h_p, h_s)
```

```python
import functools
import math

import numpy as np
import jax
import jax.numpy as jnp
from jax import lax
from jax.experimental import pallas as pl
from jax.experimental.pallas import tpu as pltpu

F32 = jnp.float32
BF16 = jnp.bfloat16

N_HEADS = 8
HEAD_DIM = 64
N_KV = 2
HPG = N_HEADS // N_KV
Q_W = N_HEADS * HEAD_DIM
KV_W = N_KV * HEAD_DIM
CMP_BLOCK = 32
CMP_STRIDE = 16
CMP_RATIO = CMP_BLOCK // CMP_STRIDE
CMP_HID = 64
SEL_BLOCK = 64
N_SEL = 16
SEL_BONUS = 1.0e4
WINDOW = 512
ROPE_THETA = 10000.0
RG_BLOCKS = 16
RG_C = 8.0
CONV_W = 4
PAGE_SIZE = 128
LN_EPS = 1e-5
SCALE = HEAD_DIM ** -0.5
NEG_INF = float("-inf")

LANES = 128
SUBLANES = 8
MXU_DIM = 256
VMEM_LIMIT_BYTES = 56 * 1024 * 1024

ROW_W = 2 * KV_W
P_W = 2 * N_KV * CMP_RATIO * CMP_HID
GATE_PAD = LANES
PAGES_PER_STEP = 16
BLOCKS_PER_PAGE = PAGE_SIZE // SEL_BLOCK


def _dot(a, b):
    return jnp.dot(a.astype(BF16), b.astype(BF16), preferred_element_type=F32)


def _dot_nt(a, b):
    return lax.dot_general(a.astype(BF16), b.astype(BF16), (((1,), (1,)), ((), ())),
                           preferred_element_type=F32)


def _dot_f32(a, b):
    return jnp.dot(a, b, preferred_element_type=F32, precision=lax.Precision.HIGHEST)


def _sigmoid(x):
    return 1.0 / (1.0 + jnp.exp(-x))


def _gelu(x):
    return 0.5 * x * (1.0 + jnp.tanh(math.sqrt(2.0 / math.pi) * (x + 0.044715 * (x * x * x))))


def _layer_norm(y, g, b):
    mu = jnp.mean(y, axis=-1, keepdims=True)
    yc = y - mu
    var = jnp.mean(yc * yc, axis=-1, keepdims=True)
    return yc * lax.rsqrt(var + LN_EPS) * g + b


def _softmax_parts(s, mask):
    s = jnp.where(mask, s, NEG_INF)
    m = jnp.max(s, axis=-1, keepdims=True)
    m = jnp.where(m > NEG_INF, m, 0.0)
    e = jnp.exp(s - m)
    d = jnp.sum(e, axis=-1, keepdims=True)
    return e, jnp.where(d > 0, d, 1.0)


def _const_spec(shape):
    nd = len(shape)
    return pl.BlockSpec(shape, lambda *_: (0,) * nd)


def _params(*sem):
    return pltpu.CompilerParams(dimension_semantics=sem, vmem_limit_bytes=VMEM_LIMIT_BYTES)


def _ffn_body(x_ref, wg_ref, wu_ref, wd_ref, g_ref, b_ref, o_ref, *, alpha):
    x = x_ref[...]
    xb = x.astype(BF16)
    gate = jnp.dot(xb, wg_ref[...], preferred_element_type=F32)
    up = jnp.dot(xb, wu_ref[...], preferred_element_type=F32)
    hmid = gate * _sigmoid(gate) * up
    f = _dot(hmid, wd_ref[...])
    o_ref[...] = _layer_norm(alpha * x + 0.5 * f, g_ref[...], b_ref[...])


def _ffn_half_step(x, wg, wu, wd, g, b, *, alpha, tm):
    n, d = x.shape
    ff = wg.shape[1]
    return pl.pallas_call(
        functools.partial(_ffn_body, alpha=alpha),
        grid=(n // tm,),
        in_specs=[pl.BlockSpec((tm, d), lambda i: (i, 0)),
                  _const_spec((d, ff)), _const_spec((d, ff)), _const_spec((ff, d)),
                  _const_spec((1, d)), _const_spec((1, d))],
        out_specs=pl.BlockSpec((tm, d), lambda i: (i, 0)),
        out_shape=jax.ShapeDtypeStruct((n, d), F32),
        compiler_params=_params("parallel"),
        name="ffn_half_step",
    )(x, wg, wu, wd, g, b)


def _rope(x, cos, sin):
    w = x.shape[1]
    reps = w // LANES
    c = jnp.tile(cos, (1, reps)) if reps > 1 else cos
    s = jnp.tile(sin, (1, reps)) if reps > 1 else sin
    lane = lax.broadcasted_iota(jnp.int32, x.shape, 1)
    first_half = (lane % HEAD_DIM) < (HEAD_DIM // 2)
    swapped = jnp.where(first_half, pltpu.roll(x, w - HEAD_DIM // 2, 1), pltpu.roll(x, HEAD_DIM // 2, 1))
    return x * c + swapped * s


def _inproj_body(h_ref, cos_ref, sin_ref, wa_ref, wb_ref,
                 qraw_ref, qrot_ref, cmp_ref, sel_ref, win_ref, gates_ref,
                 xr_ref, gr_ref, ga_ref, gb_ref, *t_refs):
    hb = h_ref[...].astype(BF16)
    cos = cos_ref[...]
    sin = sin_ref[...]
    za = jnp.dot(hb, wa_ref[...], preferred_element_type=F32)
    q = za[:, :Q_W]
    kv = za[:, Q_W:Q_W + 6 * KV_W]
    ng = za[:, Q_W + 6 * KV_W:]
    qraw_ref[...] = q
    qrot_ref[...] = _rope(q, cos, sin)
    cmp = kv[:, :2 * KV_W]
    sel = jnp.concatenate([_rope(kv[:, 2 * KV_W:3 * KV_W], cos, sin), kv[:, 3 * KV_W:4 * KV_W]], axis=1)
    win = jnp.concatenate([_rope(kv[:, 4 * KV_W:5 * KV_W], cos, sin), kv[:, 5 * KV_W:6 * KV_W]], axis=1)
    cmp_ref[...] = cmp
    sel_ref[...] = sel
    win_ref[...] = win
    if t_refs:
        cmpt_ref, selt_ref, wint_ref = t_refs
        cmpt_ref[...] = cmp.T
        selt_ref[...] = sel.T
        wint_ref[...] = win.T
    gates_ref[...] = _sigmoid(ng)
    zb = jnp.dot(hb, wb_ref[...], preferred_element_type=F32)
    d = xr_ref.shape[1]
    xr_ref[...] = zb[:, :d]
    gr_ref[...] = zb[:, d:2 * d]
    ga_ref[...] = zb[:, 2 * d:3 * d]
    gb_ref[...] = zb[:, 3 * d:]


def _in_proj(h, cos, sin, wa, wb, *, tm, seq_len, transposed_kv):
    n, d = h.shape
    wa_w, wb_w = wa.shape[1], wb.shape[1]
    widths = (Q_W, Q_W, ROW_W, ROW_W, ROW_W, GATE_PAD, d, d, d, d)
    tiles = seq_len // tm
    row = lambda i: (i, 0)
    tab = lambda i: (i % tiles, 0)
    out_specs = [pl.BlockSpec((tm, w), row) for w in widths]
    out_shape = [jax.ShapeDtypeStruct((n, w), F32) for w in widths]
    if transposed_kv:
        out_specs += [pl.BlockSpec((None, ROW_W, tm), lambda i: (i // tiles, 0, i % tiles))] * 3
        out_shape += [jax.ShapeDtypeStruct((n // seq_len, ROW_W, seq_len), F32)] * 3
    return pl.pallas_call(
        _inproj_body,
        grid=(n // tm,),
        in_specs=[pl.BlockSpec((tm, d), row), pl.BlockSpec((tm, LANES), tab), pl.BlockSpec((tm, LANES), tab),
                  _const_spec((d, wa_w)), _const_spec((d, wb_w))],
        out_specs=out_specs,
        out_shape=out_shape,
        compiler_params=_params("parallel"),
        name="in_proj",
    )(h, cos, sin, wa, wb)


def _cmp_chunks(rows_of, w_ref, kv, nch):
    acc = None
    for s in range(CMP_STRIDE):
        part = _dot(rows_of(s, nch), w_ref[kv, s])
        acc = part if acc is None else acc + part
    return acc


def _cmp_stage1_body(x_ref, w_ref, o_ref, rows_ref):
    nch = x_ref.shape[0] // CMP_STRIDE
    half = P_W // 2
    for kv in range(2):
        rows_ref[kv] = x_ref[:, kv * KV_W:(kv + 1) * KV_W]
        rows_of = lambda s, n, kv=kv: rows_ref[kv, pl.ds(s, n, stride=CMP_STRIDE), :]
        o_ref[:, kv * half:(kv + 1) * half] = _cmp_chunks(rows_of, w_ref, kv, nch)


def _cmp_stage1(x, w):
    b, t, _ = x.shape
    nch = t // CMP_STRIDE
    return pl.pallas_call(
        _cmp_stage1_body,
        grid=(b,),
        in_specs=[pl.BlockSpec((None, t, ROW_W), lambda i: (i, 0, 0)), _const_spec(w.shape)],
        out_specs=pl.BlockSpec((None, nch, P_W), lambda i: (i, 0, 0)),
        out_shape=jax.ShapeDtypeStruct((b, nch, P_W), F32),
        scratch_shapes=[pltpu.VMEM((2, t, KV_W), F32)],
        compiler_params=_params("parallel"),
        name="cmp_stage1",
    )(x, w)


def _cmp_paged_body(pt_ref, *refs):
    del pt_ref
    x_refs = refs[:PAGES_PER_STEP]
    w_ref, o_ref, rows_ref = refs[PAGES_PER_STEP:]
    for k, x_ref in enumerate(x_refs):
        for kv in range(2):
            rows_ref[kv, k * PAGE_SIZE:(k + 1) * PAGE_SIZE, :] = x_ref[kv].T
    nch = PAGES_PER_STEP * PAGE_SIZE // CMP_STRIDE
    half = P_W // 2
    for kv in range(2):
        rows_of = lambda s, n, kv=kv: rows_ref[kv, pl.ds(s, n, stride=CMP_STRIDE), :]
        o_ref[:, kv * half:(kv + 1) * half] = _cmp_chunks(rows_of, w_ref, kv, nch)


def _cmp_stage1_paged(cache_t, page_table, w):
    b, n_pages = page_table.shape
    steps = n_pages // PAGES_PER_STEP
    cpp = PAGE_SIZE // CMP_STRIDE
    nch = n_pages * cpp

    def page_spec(k):
        return pl.BlockSpec((None, 2, KV_W, PAGE_SIZE), lambda i, s, pt: (pt[i, s * PAGES_PER_STEP + k], 0, 0, 0))

    return pl.pallas_call(
        _cmp_paged_body,
        grid_spec=pltpu.PrefetchScalarGridSpec(
            num_scalar_prefetch=1,
            grid=(b, steps),
            in_specs=[page_spec(k) for k in range(PAGES_PER_STEP)]
            + [pl.BlockSpec(w.shape, lambda i, s, pt: (0,) * w.ndim)],
            out_specs=pl.BlockSpec((None, PAGES_PER_STEP * cpp, P_W), lambda i, s, pt: (i, s, 0)),
            scratch_shapes=[pltpu.VMEM((2, PAGES_PER_STEP * PAGE_SIZE, KV_W), F32)],
        ),
        out_shape=jax.ShapeDtypeStruct((b, nch, P_W), F32),
        compiler_params=_params("parallel", "parallel"),
        name="cmp_stage1_paged",
    )(page_table, *([cache_t] * PAGES_PER_STEP), w)


def _cmp_finish_body(p_ref, pek_ref, w1k_ref, w2k_ref, pev_ref, w1v_ref, w2v_ref, o_ref, *, nc):
    p = p_ref[...]
    nch = p.shape[0]
    row = lax.broadcasted_iota(jnp.int32, (nch, CMP_HID), 0)
    outs = []
    for kv, (pe_ref, w1_ref, w2_ref) in enumerate(((pek_ref, w1k_ref, w2k_ref), (pev_ref, w1v_ref, w2v_ref))):
        pe = jnp.broadcast_to(pe_ref[...], (SUBLANES, pe_ref.shape[1]))
        bias = _dot(pe, w1_ref[...])[0:1]
        for g in range(N_KV):
            c0 = (kv * N_KV + g) * CMP_RATIO * CMP_HID
            u = bias + p[:, c0:c0 + CMP_HID]
            for r in range(1, CMP_RATIO):
                u = u + pltpu.roll(p[:, c0 + r * CMP_HID:c0 + (r + 1) * CMP_HID], nch - r, 0)
            out = _dot(_gelu(u), w2_ref[...])
            outs.append(jnp.where(row < nc, out, 0.0))
    o_ref[...] = jnp.concatenate(outs, axis=1)


def _cmp_finish(p, pe_k, w1_k, w2_k, pe_v, w1_v, w2_v, *, nc):
    b, nch, _ = p.shape
    flat = CMP_BLOCK * HEAD_DIM
    return pl.pallas_call(
        functools.partial(_cmp_finish_body, nc=nc),
        grid=(b,),
        in_specs=[pl.BlockSpec((None, nch, P_W), lambda i: (i, 0, 0)),
                  _const_spec((1, flat)), _const_spec((flat, CMP_HID)), _const_spec((CMP_HID, HEAD_DIM)),
                  _const_spec((1, flat)), _const_spec((flat, CMP_HID)), _const_spec((CMP_HID, HEAD_DIM))],
        out_specs=pl.BlockSpec((None, nch, ROW_W), lambda i: (i, 0, 0)),
        out_shape=jax.ShapeDtypeStruct((b, nch, ROW_W), F32),
        compiler_params=_params("parallel"),
        name="cmp_finish",
    )(p, pe_k, w1_k, w2_k, pe_v, w1_v, w2_v)


def _stack_heads(ref, g):
    return jnp.concatenate([ref[:, (g * HPG + h) * HEAD_DIM:(g * HPG + h + 1) * HEAD_DIM] for h in range(HPG)], axis=0)


def _block_scores(imp, q_pos, nsel_valid):
    jblk = lax.broadcasted_iota(jnp.int32, imp.shape, 1)
    cur = q_pos // SEL_BLOCK
    valid = (jblk * SEL_BLOCK <= q_pos) & (jblk < nsel_valid)
    forced = (jblk == 0) | (jblk == cur) | (jblk == cur - 1)
    return jnp.where(valid, imp + jnp.where(forced, SEL_BONUS, 0.0), NEG_INF)


def _nsa_prompt_body(qraw_ref, qrot_ref, gates_ref, ckv_ref, sel_ref, selt_ref, win_ref, wint_ref,
                     cover_ref, expand_ref, o_ref, *, tq, nc):
    nchp = ckv_ref.shape[0]
    nsel = cover_ref.shape[1]
    rows = HPG * tq
    tile = pl.program_id(1)
    s0 = tile * tq
    qpos_r = s0 + lax.broadcasted_iota(jnp.int32, (rows, 1), 0) % tq
    qpos_t = s0 + lax.broadcasted_iota(jnp.int32, (tq, 1), 0)
    cidx = lax.broadcasted_iota(jnp.int32, (1, nchp), 1)
    jblk = lax.broadcasted_iota(jnp.int32, (tq, nsel), 1)
    key_lane = lax.broadcasted_iota(jnp.int32, (1, tq), 1)
    gates = gates_ref[...]

    def attend(q, kt_ref, v_ref, g, first_chunk, mask_of):
        def step(kc, carry):
            m, l, acc = carry
            k0 = pl.multiple_of(kc * tq, tq)
            s = _dot(q, kt_ref[g * HEAD_DIM:(g + 1) * HEAD_DIM, pl.ds(k0, tq)])
            s = jnp.where(mask_of(k0), s, NEG_INF)
            m_new = jnp.maximum(m, jnp.max(s, axis=1, keepdims=True))
            m_safe = jnp.where(m_new > NEG_INF, m_new, 0.0)
            alpha = jnp.exp(m - m_safe)
            p = jnp.exp(s - m_safe)
            l = l * alpha + jnp.sum(p, axis=1, keepdims=True)
            acc = acc * alpha + _dot(p, v_ref[pl.ds(k0, tq), KV_W + g * HEAD_DIM:KV_W + (g + 1) * HEAD_DIM])
            return m_new, l, acc

        init = (jnp.full((rows, 1), NEG_INF, F32), jnp.zeros((rows, 1), F32), jnp.zeros((rows, HEAD_DIM), F32))
        _, l, acc = lax.fori_loop(first_chunk, tile + 1, step, init)
        return acc / jnp.where(l > 0, l, 1.0)

    for g in range(N_KV):
        q_raw = _stack_heads(qraw_ref, g) * SCALE
        q_rot = _stack_heads(qrot_ref, g) * SCALE
        s_c = _dot_nt(q_raw, ckv_ref[:, g * HEAD_DIM:(g + 1) * HEAD_DIM])
        m_c = (cidx * CMP_STRIDE + (CMP_BLOCK - 1) <= qpos_r) & (cidx < nc)
        e_c, d_c = _softmax_parts(s_c, m_c)
        p_c = e_c / d_c
        o_c = _dot(p_c, ckv_ref[:, KV_W + g * HEAD_DIM:KV_W + (g + 1) * HEAD_DIM])
        p_sum = p_c[0:tq]
        for h in range(1, HPG):
            p_sum = p_sum + p_c[h * tq:(h + 1) * tq]
        imp = _dot_f32(p_sum, cover_ref[...])
        score = _block_scores(imp, qpos_t, nsel)
        rank = jnp.zeros((tq, nsel), jnp.int32)
        for i in range(nsel):
            s_i = score[:, i:i + 1]
            beats = (s_i > score) | ((s_i == score) & (i < jblk))
            rank = rank + beats.astype(jnp.int32)
        chosen = (rank < N_SEL).astype(BF16)

        def sel_mask(k0):
            picked = jnp.dot(chosen, expand_ref[:, pl.ds(k0, tq)], preferred_element_type=F32)
            picked = jnp.concatenate([picked] * HPG, axis=0)
            return (picked > 0.5) & (k0 + key_lane <= qpos_r)

        o_s = attend(q_rot, selt_ref, sel_ref, g, 0, sel_mask)

        def win_mask(k0):
            rel = qpos_r - (k0 + key_lane)
            return (rel >= 0) & (rel <= WINDOW)

        o_w = attend(q_rot, wint_ref, win_ref, g, jnp.maximum(s0 - WINDOW, 0) // tq, win_mask)

        for h in range(HPG):
            hd = g * HPG + h
            r = slice(h * tq, (h + 1) * tq)
            o_ref[:, hd * HEAD_DIM:(hd + 1) * HEAD_DIM] = (
                gates[:, 3 * hd:3 * hd + 1] * o_c[r] + gates[:, 3 * hd + 1:3 * hd + 2] * o_s[r]
                + gates[:, 3 * hd + 2:3 * hd + 3] * o_w[r])


def _nsa_prompt(q_raw, q_rot, gates, ckv, sel, sel_t, win, win_t, cover, expand, *, tq, nc):
    b, t, _ = q_raw.shape
    nchp = ckv.shape[1]
    nsel = cover.shape[1]
    tile = lambda w: pl.BlockSpec((None, tq, w), lambda i, j: (i, j, 0))
    whole = lambda r, w: pl.BlockSpec((None, r, w), lambda i, j: (i, 0, 0))
    return pl.pallas_call(
        functools.partial(_nsa_prompt_body, tq=tq, nc=nc),
        grid=(b, t // tq),
        in_specs=[tile(Q_W), tile(Q_W), tile(GATE_PAD), whole(nchp, ROW_W),
                  whole(t, ROW_W), whole(ROW_W, t), whole(t, ROW_W), whole(ROW_W, t),
                  _const_spec((nchp, nsel)), _const_spec((nsel, t))],
        out_specs=tile(Q_W),
        out_shape=jax.ShapeDtypeStruct((b, t, Q_W), F32),
        compiler_params=_params("parallel", "arbitrary"),
        name="nsa_prompt",
    )(q_raw, q_rot, gates, ckv, sel, sel_t, win, win_t, cover, expand)


def _softplus(x):
    return jnp.maximum(x, 0.0) + jnp.log1p(jnp.exp(-jnp.abs(x)))


def _expm1(x):
    u = jnp.exp(x)
    um1 = u - 1.0
    safe = (um1 != 0.0) & (um1 != -1.0)
    y = um1 * x / jnp.log(jnp.where(safe, u, 2.0))
    return jnp.where(um1 == 0.0, x, jnp.where(um1 == -1.0, -1.0, y))


def _rg_gates(xc, wgate_ref, ba, bx, lam):
    xcb = xc.astype(BF16)
    n_grp = wgate_ref.shape[0]
    gw = wgate_ref.shape[1]
    za, zx = [], []
    for k in range(n_grp):
        z = jnp.dot(xcb[:, k * gw:(k + 1) * gw], wgate_ref[k], preferred_element_type=F32)
        za.append(z[:, :gw])
        zx.append(z[:, gw:])
    r = _sigmoid(jnp.concatenate(za, axis=1) + ba)
    i = _sigmoid(jnp.concatenate(zx, axis=1) + bx)
    log_a = -RG_C * r * _softplus(-lam)
    a = jnp.exp(log_a)
    u = jnp.sqrt(-_expm1(2.0 * log_a)) * i * xc
    return a, u


def _merge(h_res, hs, gr, ga, gb, o_attn, wbra_ref, wbrr_ref, wout_ref, g2, b2, alpha):
    y_rnn = hs * _gelu(gr)
    m = _sigmoid(ga) * _dot(o_attn, wbra_ref[...]) + _sigmoid(gb) * _dot(y_rnn, wbrr_ref[...])
    return _layer_norm(alpha * h_res + _dot(m, wout_ref[...]), g2, b2)


def _shift_rows(x, d, fill):
    row = lax.broadcasted_iota(jnp.int32, x.shape, 0)
    return jnp.where(row < d, fill, pltpu.roll(x, d, 0))


def _mix_prompt_body(xr_ref, gr_ref, ga_ref, gb_ref, oat_ref, h1_ref, convw_ref, convb_ref, wgate_ref,
                     ba_ref, bx_ref, lam_ref, wbra_ref, wbrr_ref, wout_ref, g2_ref, b2_ref,
                     o_ref, hlast_ref, hc_ref, tail_ref, *, tt, alpha):
    @pl.when(pl.program_id(1) == 0)
    def _():
        hc_ref[...] = jnp.zeros_like(hc_ref)
        tail_ref[...] = jnp.zeros_like(tail_ref)

    x = xr_ref[...]
    prev = tail_ref[...]
    convw = convw_ref[...]
    row8 = lax.broadcasted_iota(jnp.int32, prev.shape, 0)
    xc = convb_ref[...] + convw[CONV_W - 1:CONV_W] * x
    for d in range(1, CONV_W):
        xs = pltpu.roll(x, d, 0)
        head = jnp.where(row8 < d, pltpu.roll(prev, d, 0), xs[0:SUBLANES])
        xs = jnp.concatenate([head, xs[SUBLANES:]], axis=0)
        xc = xc + convw[CONV_W - 1 - d:CONV_W - d] * xs
    tail_ref[...] = x[tt - SUBLANES:tt]

    a, u = _rg_gates(xc, wgate_ref, ba_ref[...], bx_ref[...], lam_ref[...])
    d = 1
    while d < tt:
        u = a * _shift_rows(u, d, 0.0) + u
        a = a * _shift_rows(a, d, 1.0)
        d *= 2
    hs = a * hc_ref[...] + u
    hc_ref[...] = hs[tt - 1:tt]
    hlast_ref[...] = hs[tt - 1:tt]
    o_ref[...] = _merge(h1_ref[...], hs, gr_ref[...], ga_ref[...], gb_ref[...], oat_ref[...],
                        wbra_ref, wbrr_ref, wout_ref, g2_ref[...], b2_ref[...], alpha)


def _mix_weight_specs(d, n_grp, gw):
    return [_const_spec((SUBLANES, d)), _const_spec((1, d)), _const_spec((n_grp, gw, 2 * gw)),
            _const_spec((1, d)), _const_spec((1, d)), _const_spec((1, d)),
            _const_spec((Q_W, d)), _const_spec((d, d)), _const_spec((d, d)),
            _const_spec((1, d)), _const_spec((1, d))]


def _mix_prompt(xr, gr, ga, gb, o_attn, h1, mixw, *, tt, alpha):
    b, t, d = xr.shape
    n_grp, gw = mixw[2].shape[:2]
    tile = lambda w: pl.BlockSpec((None, tt, w), lambda i, j: (i, j, 0))
    return pl.pallas_call(
        functools.partial(_mix_prompt_body, tt=tt, alpha=alpha),
        grid=(b, t // tt),
        in_specs=[tile(d), tile(d), tile(d), tile(d), tile(Q_W), tile(d)] + _mix_weight_specs(d, n_grp, gw),
        out_specs=[tile(d), pl.BlockSpec((None, 1, d), lambda i, j: (i, 0, 0))],
        out_shape=[jax.ShapeDtypeStruct((b, t, d), F32), jax.ShapeDtypeStruct((b, 1, d), F32)],
        scratch_shapes=[pltpu.VMEM((1, d), F32), pltpu.VMEM((SUBLANES, d), F32)],
        compiler_params=_params("parallel", "arbitrary"),
        name="mix_prompt",
    )(xr, gr, ga, gb, o_attn, h1, *mixw)


def _mix_sample_body(xr_ref, c0_ref, c1_ref, c2_ref, h0_ref, gr_ref, ga_ref, gb_ref, oat_ref, h1_ref,
                     convw_ref, convb_ref, wgate_ref, ba_ref, bx_ref, lam_ref, wbra_ref, wbrr_ref, wout_ref,
                     g2_ref, b2_ref, o_ref, hnew_ref, *, alpha):
    convw = convw_ref[...]
    xc = (convb_ref[...] + convw[0:1] * c0_ref[...] + convw[1:2] * c1_ref[...] + convw[2:3] * c2_ref[...]
          + convw[3:4] * xr_ref[...])
    a, u = _rg_gates(xc, wgate_ref, ba_ref[...], bx_ref[...], lam_ref[...])
    hs = a * h0_ref[...] + u
    hnew_ref[...] = hs
    o_ref[...] = _merge(h1_ref[...], hs, gr_ref[...], ga_ref[...], gb_ref[...], oat_ref[...],
                        wbra_ref, wbrr_ref, wout_ref, g2_ref[...], b2_ref[...], alpha)


def _mix_sample(xr, conv_rows, h0, gr, ga, gb, o_attn, h1, mixw, *, alpha):
    b, d = xr.shape
    n_grp, gw = mixw[2].shape[:2]
    full = lambda w: _const_spec((b, w))
    return pl.pallas_call(
        functools.partial(_mix_sample_body, alpha=alpha),
        grid=(1,),
        in_specs=[full(d)] * 8 + [full(Q_W), full(d)] + _mix_weight_specs(d, n_grp, gw),
        out_specs=[full(d), full(d)],
        out_shape=[jax.ShapeDtypeStruct((b, d), F32), jax.ShapeDtypeStruct((b, d), F32)],
        compiler_params=_params("arbitrary"),
        name="mix_sample",
    )(xr, *conv_rows, h0, gr, ga, gb, o_attn, h1, *mixw)


def _head_rows(row, g):
    parts = [row[:, (g * HPG + h) * HEAD_DIM:(g * HPG + h + 1) * HEAD_DIM] for h in range(HPG)]
    return jnp.concatenate(parts + [jnp.zeros((SUBLANES - HPG, HEAD_DIM), F32)], axis=0)


def _nsa_sample_scores_body(qraw_ref, ckv_ref, cover_ref, oc_ref, score_ref, *, nc, nsel, q_pos):
    nchp = ckv_ref.shape[0]
    nselp = cover_ref.shape[1]
    cidx = lax.broadcasted_iota(jnp.int32, (1, nchp), 1)
    qrow = qraw_ref[...] * SCALE
    m_c = (cidx * CMP_STRIDE + (CMP_BLOCK - 1) <= q_pos) & (cidx < nc)
    oc_parts, score_rows = [], []
    for g in range(N_KV):
        q = _head_rows(qrow, g)
        s_c = _dot_nt(q, ckv_ref[:, g * HEAD_DIM:(g + 1) * HEAD_DIM])
        e_c, d_c = _softmax_parts(s_c, m_c)
        p_c = e_c / d_c
        o_c = _dot(p_c, ckv_ref[:, KV_W + g * HEAD_DIM:KV_W + (g + 1) * HEAD_DIM])
        oc_parts += [o_c[h:h + 1] for h in range(HPG)]
        p_sum = jnp.sum(p_c[0:HPG], axis=0, keepdims=True)
        imp = _dot_f32(jnp.broadcast_to(p_sum, (SUBLANES, nchp)), cover_ref[...])[0:1]
        score_rows.append(_block_scores(imp, q_pos, nsel))
    oc_ref[...] = jnp.concatenate(oc_parts, axis=1)
    score_ref[...] = jnp.concatenate(score_rows + [jnp.full((SUBLANES - N_KV, nselp), NEG_INF, F32)], axis=0)


def _nsa_sample_scores(q_raw, ckv, cover, *, nc, nsel, q_pos):
    b = q_raw.shape[0]
    nchp = ckv.shape[1]
    nselp = cover.shape[1]
    return pl.pallas_call(
        functools.partial(_nsa_sample_scores_body, nc=nc, nsel=nsel, q_pos=q_pos),
        grid=(b,),
        in_specs=[pl.BlockSpec((None, 1, Q_W), lambda i: (i, 0, 0)),
                  pl.BlockSpec((None, nchp, ROW_W), lambda i: (i, 0, 0)), _const_spec((nchp, nselp))],
        out_specs=[pl.BlockSpec((None, 1, Q_W), lambda i: (i, 0, 0)),
                   pl.BlockSpec((None, SUBLANES, nselp), lambda i: (i, 0, 0))],
        out_shape=[jax.ShapeDtypeStruct((b, 1, Q_W), F32), jax.ShapeDtypeStruct((b, SUBLANES, nselp), F32)],
        compiler_params=_params("parallel"),
        name="nsa_sample_scores",
    )(q_raw, ckv, cover)


def _top_blocks_body(score_ref, idx_ref):
    score = score_ref[...]
    n, nselp = score.shape
    lane = lax.broadcasted_iota(jnp.int32, (n, nselp), 1)
    lane_out = lax.broadcasted_iota(jnp.int32, (n, LANES), 1)
    picked = jnp.zeros((n, LANES), jnp.int32)
    for r in range(N_SEL):
        m = jnp.max(score, axis=1, keepdims=True)
        j = jnp.min(jnp.where(score == m, lane, nselp), axis=1, keepdims=True)
        picked = jnp.where(lane_out == r, j, picked)
        score = jnp.where(lane == j, NEG_INF, score)
    idx_ref[...] = picked


def _top_blocks(score):
    n, nselp = score.shape
    return pl.pallas_call(
        _top_blocks_body,
        grid=(1,),
        in_specs=[_const_spec((n, nselp))],
        out_specs=_const_spec((n, LANES)),
        out_shape=jax.ShapeDtypeStruct((n, LANES), jnp.int32),
        compiler_params=_params("arbitrary"),
        name="top_blocks",
    )(score)


def _nsa_sample_attend_body(idx_ref, pt_ref, qrot_ref, gates_ref, oc_ref, selnew_ref, winnew_ref, wincache_ref,
                            *refs, q_pos, n_past_blocks):
    del pt_ref
    page_refs, o_ref = refs[:-1], refs[-1]
    b = pl.program_id(0)
    wb = wincache_ref.shape[2]
    qrow = qrot_ref[...] * SCALE
    gates = gates_ref[...]
    oc = oc_ref[...]
    sel_new = selnew_ref[...]
    win_new = winnew_ref[...]
    page_lane = lax.broadcasted_iota(jnp.int32, (1, PAGE_SIZE), 1)
    wpos = (q_pos - wb) + lax.broadcasted_iota(jnp.int32, (1, wb), 1)
    m_w = (q_pos - wpos <= WINDOW) & (wpos >= 0) & (wpos <= q_pos)
    cur = q_pos // SEL_BLOCK
    out_parts = []
    for g in range(N_KV):
        q = _head_rows(qrow, g)
        feat = slice(g * HEAD_DIM, (g + 1) * HEAD_DIM)
        kcol = slice(g * HEAD_DIM, (g + 1) * HEAD_DIM)
        vcol = slice(KV_W + g * HEAD_DIM, KV_W + (g + 1) * HEAD_DIM)
        kts, vts, kps = [], [], []
        new_chosen = None
        for n in range(N_SEL):
            j = idx_ref[b, g * N_SEL + n]
            page = page_refs[g * N_SEL + n]
            kts.append(page[0, feat, :])
            vts.append(page[1, feat, :])
            in_block = (page_lane // SEL_BLOCK == j % BLOCKS_PER_PAGE) & (j < n_past_blocks)
            kps.append(jnp.where(in_block, (j // BLOCKS_PER_PAGE) * PAGE_SIZE + page_lane, q_pos + 1))
            hit = j == cur
            new_chosen = hit if new_chosen is None else (new_chosen | hit)
        s_s = _dot(q, jnp.concatenate(kts, axis=1))
        s_new = jnp.where(new_chosen, jnp.sum(q * sel_new[:, kcol], axis=1, keepdims=True), NEG_INF)
        s_s = jnp.where(jnp.concatenate(kps, axis=1) <= q_pos, s_s, NEG_INF)
        m = jnp.maximum(jnp.max(s_s, axis=1, keepdims=True), s_new)
        m = jnp.where(m > NEG_INF, m, 0.0)
        e_s = jnp.exp(s_s - m)
        e_new = jnp.exp(s_new - m)
        d_s = jnp.sum(e_s, axis=1, keepdims=True) + e_new
        d_s = jnp.where(d_s > 0, d_s, 1.0)
        o_s = (_dot_nt(e_s, jnp.concatenate(vts, axis=1)) + e_new * sel_new[:, vcol]) / d_s
        s_w = jnp.where(m_w, _dot(q, wincache_ref[0, feat, :]), NEG_INF)
        s_wn = jnp.sum(q * win_new[:, kcol], axis=1, keepdims=True)
        m = jnp.maximum(jnp.max(s_w, axis=1, keepdims=True), s_wn)
        e_w = jnp.exp(s_w - m)
        e_wn = jnp.exp(s_wn - m)
        d_w = jnp.sum(e_w, axis=1, keepdims=True) + e_wn
        o_w = (_dot_nt(e_w, wincache_ref[1, feat, :]) + e_wn * win_new[:, vcol]) / d_w
        for h in range(HPG):
            hd = g * HPG + h
            out_parts.append(gates[:, 3 * hd:3 * hd + 1] * oc[:, hd * HEAD_DIM:(hd + 1) * HEAD_DIM]
                             + gates[:, 3 * hd + 1:3 * hd + 2] * o_s[h:h + 1]
                             + gates[:, 3 * hd + 2:3 * hd + 3] * o_w[h:h + 1])
    o_ref[...] = jnp.concatenate(out_parts, axis=1)


def _nsa_sample_attend(idx, page_table, q_rot, gates, o_c, sel_new, win_new, win_cache_t, sel_cache_t,
                       *, q_pos, n_past_blocks):
    b = q_rot.shape[0]
    wb = win_cache_t.shape[3]

    def page_spec(k):
        def index(i, idx_ref, pt_ref):
            j = jnp.minimum(idx_ref[i, k], n_past_blocks - 1)
            return (pt_ref[i, j // BLOCKS_PER_PAGE], 0, 0, 0)
        return pl.BlockSpec((None, 2, KV_W, PAGE_SIZE), index)

    one = lambda w: pl.BlockSpec((None, 1, w), lambda i, idx_ref, pt_ref: (i, 0, 0))
    return pl.pallas_call(
        functools.partial(_nsa_sample_attend_body, q_pos=q_pos, n_past_blocks=n_past_blocks),
        grid_spec=pltpu.PrefetchScalarGridSpec(
            num_scalar_prefetch=2,
            grid=(b,),
            in_specs=[one(Q_W), one(GATE_PAD), one(Q_W), one(ROW_W), one(ROW_W),
                      pl.BlockSpec((None, 2, KV_W, wb), lambda i, idx_ref, pt_ref: (i, 0, 0, 0))]
            + [page_spec(k) for k in range(N_KV * N_SEL)],
            out_specs=one(Q_W),
        ),
        out_shape=jax.ShapeDtypeStruct((b, 1, Q_W), F32),
        compiler_params=_params("arbitrary"),
        name="nsa_sample_attend",
    )(idx, page_table, q_rot, gates, o_c, sel_new, win_new, win_cache_t, *([sel_cache_t] * (N_KV * N_SEL)))


def _rope_tables(pos):
    half = HEAD_DIM // 2
    freqs = ROPE_THETA ** (-jnp.arange(half, dtype=F32) / half)
    ang = pos.astype(F32)[:, None] * freqs[None, :]
    cos, sin = jnp.cos(ang), jnp.sin(ang)
    reps = LANES // HEAD_DIM
    return jnp.tile(jnp.concatenate([cos, cos], axis=1), (1, reps)), jnp.tile(jnp.concatenate([-sin, sin], axis=1), (1, reps))


def _cover(nc, nsel, rows, cols):
    start = np.arange(rows)[:, None] * CMP_STRIDE
    j = np.arange(cols)[None, :]
    hit = (start < (j + 1) * SEL_BLOCK) & (start + CMP_BLOCK > j * SEL_BLOCK) & (np.arange(rows)[:, None] < nc) & (j < nsel)
    return jnp.asarray(hit.astype(np.float32))


def _cmp_chunk_weight(w1_k, w1_v):
    per = jnp.stack([w.reshape(CMP_RATIO, CMP_STRIDE, HEAD_DIM, CMP_HID) for w in (w1_k, w1_v)])
    eye = jnp.eye(N_KV, dtype=F32)
    big = jnp.einsum("ab,krsdh->ksadbrh", eye, per)
    return big.reshape(2, CMP_STRIDE, KV_W, N_KV * CMP_RATIO * CMP_HID).astype(BF16)


def _rg_gate_weight(w_a, w_x):
    nb, bw, _ = w_a.shape
    per = MXU_DIM // bw
    n_grp = nb // per
    eye = jnp.eye(per, dtype=F32)

    def group(w):
        w = w.reshape(n_grp, per, bw, bw)
        return jnp.einsum("pq,gpde->gpdqe", eye, w).reshape(n_grp, per * bw, per * bw)

    return jnp.concatenate([group(w_a), group(w_x)], axis=2).astype(BF16)


def _feature_major(kv_rows):
    lead = kv_rows.shape[:-4]
    rows = kv_rows.shape[-4]
    nd = len(lead)
    perm = tuple(range(nd)) + (nd + 1, nd + 2, nd + 3, nd)
    return jnp.transpose(kv_rows, perm).reshape(lead + (2, KV_W, rows))


def _row_major_view(kv_t, b, rows):
    return jnp.transpose(kv_t.reshape(b, 2, N_KV, HEAD_DIM, rows), (0, 4, 1, 2, 3))


def kernel(x_prompt, x_sample, cache_cmp_kv, cache_sel_kv, cache_win_kv, state_conv, state_h, page_table, ffn1_w_gate, ffn1_w_up, ffn1_w_down, ln1_g, ln1_b, w_in, cmp_pe_k, cmp_w1_k, cmp_w2_k, cmp_pe_v, cmp_w1_v, cmp_w2_v, conv_w, conv_b, rg_w_a, rg_b_a, rg_w_x, rg_b_x, rg_lam, w_br_attn, w_br_rnn, w_out, ln2_g, ln2_b, ffn2_w_gate, ffn2_w_up, ffn2_w_down, ln3_g, ln3_b):
    bp, tp, d = x_prompt.shape
    bs, ts, _ = x_sample.shape
    depth = w_in.shape[0]
    d_rnn = conv_w.shape[2]
    n_pages = page_table.shape[1]
    past_len = n_pages * PAGE_SIZE
    assert ts == 1 and tp % CMP_STRIDE == 0 and tp % SEL_BLOCK == 0
    alpha = (2.0 * depth) ** 0.25
    wb = cache_win_kv.shape[2]
    wbp = min(WINDOW, tp)

    nch_p = tp // CMP_STRIDE
    nc_p = nch_p - CMP_RATIO + 1
    nsel_p = tp // SEL_BLOCK
    nch_s = past_len // CMP_STRIDE
    nc_s = nch_s - CMP_RATIO + 1
    nsel_s = -(-(past_len + ts) // SEL_BLOCK)
    n_past_blocks = past_len // SEL_BLOCK
    assert nsel_s >= N_SEL and n_pages % PAGES_PER_STEP == 0
    nselp_s = -(-nsel_s // LANES) * LANES

    tm = min(256, tp)
    tq = min(256, tp)
    tt = min(256, tp)

    cos_p, sin_p = _rope_tables(jnp.arange(tp))
    cos_s, sin_s = _rope_tables(jnp.full((bs,), past_len))
    cover_p = _cover(nc_p, nsel_p, nch_p, nsel_p)
    cover_s = _cover(nc_s, nsel_s, nch_s, nselp_s)
    expand_p = jnp.asarray((np.arange(tp)[None, :] // SEL_BLOCK == np.arange(nsel_p)[:, None]).astype(np.float32)).astype(BF16)

    xp = x_prompt.reshape(bp * tp, d)
    xs = x_sample.reshape(bs * ts, d)
    outs = [[] for _ in range(10)]
    for l in range(depth):
        row = lambda v: v[l].reshape(1, -1)
        ffn1 = (ffn1_w_gate[l].astype(BF16), ffn1_w_up[l].astype(BF16), ffn1_w_down[l].astype(BF16), row(ln1_g), row(ln1_b))
        ffn2 = (ffn2_w_gate[l].astype(BF16), ffn2_w_up[l].astype(BF16), ffn2_w_down[l].astype(BF16), row(ln3_g), row(ln3_b))
        n_a = Q_W + 6 * KV_W + 3 * N_HEADS
        w_a = jnp.pad(w_in[l][:, :n_a], ((0, 0), (0, GATE_PAD - 3 * N_HEADS))).astype(BF16)
        w_b = w_in[l][:, n_a:].astype(BF16)
        w_cmp = _cmp_chunk_weight(cmp_w1_k[l], cmp_w1_v[l])
        cmpw = (cmp_pe_k[l].reshape(1, -1), cmp_w1_k[l], cmp_w2_k[l], cmp_pe_v[l].reshape(1, -1), cmp_w1_v[l], cmp_w2_v[l])
        mixw = (jnp.pad(conv_w[l], ((0, SUBLANES - CONV_W), (0, 0))), row(conv_b), _rg_gate_weight(rg_w_a[l], rg_w_x[l]),
                row(rg_b_a), row(rg_b_x), row(rg_lam), w_br_attn[l].astype(BF16), w_br_rnn[l].astype(BF16),
                w_out[l].astype(BF16), row(ln2_g), row(ln2_b))

        h1 = _ffn_half_step(xp, *ffn1, alpha=alpha, tm=tm)
        (q_raw, q_rot, cmp, sel, win, gates, xr, gr, ga, gb, cmp_t, sel_t, win_t) = _in_proj(
            h1, cos_p, sin_p, w_a, w_b, tm=tm, seq_len=tp, transposed_kv=True)
        b3 = lambda a: a.reshape(bp, tp, a.shape[-1])
        p1 = _cmp_stage1(b3(cmp), w_cmp)
        ckv = _cmp_finish(p1, *cmpw, nc=nc_p)
        o_attn = _nsa_prompt(b3(q_raw), b3(q_rot), b3(gates), ckv, b3(sel), sel_t, b3(win), win_t,
                             cover_p, expand_p, tq=tq, nc=nc_p)
        h2, h_last = _mix_prompt(b3(xr), b3(gr), b3(ga), b3(gb), o_attn, b3(h1), mixw, tt=tt, alpha=alpha)
        xp = _ffn_half_step(h2.reshape(bp * tp, d), *ffn2, alpha=alpha, tm=tm)
        outs[0].append(_row_major_view(cmp_t, bp, tp))
        outs[2].append(_row_major_view(sel_t, bp, tp))
        outs[4].append(_row_major_view(win_t[:, :, tp - wbp:], bp, wbp))
        outs[6].append(b3(xr)[:, tp - (CONV_W - 1):])
        outs[8].append(h_last.reshape(bp, d_rnn))

        h1s = _ffn_half_step(xs, *ffn1, alpha=alpha, tm=bs)
        q_raw, q_rot, cmp, sel, win, gates, xr, gr, ga, gb = _in_proj(
            h1s, cos_s, sin_s, w_a, w_b, tm=bs, seq_len=bs, transposed_kv=False)
        p1 = _cmp_stage1_paged(_feature_major(cache_cmp_kv[l]), page_table, w_cmp)
        ckv = _cmp_finish(p1, *cmpw, nc=nc_s)
        b1 = lambda a: a.reshape(bs, 1, a.shape[-1])
        q_pos = past_len
        o_c, score = _nsa_sample_scores(b1(q_raw), ckv, cover_s, nc=nc_s, nsel=nsel_s, q_pos=q_pos)
        idx = _top_blocks(score.reshape(bs * SUBLANES, nselp_s))
        idx = idx.reshape(bs, SUBLANES, LANES)[:, :N_KV, :N_SEL].reshape(bs, N_KV * N_SEL)
        o_attn = _nsa_sample_attend(idx, page_table, b1(q_rot), b1(gates), o_c, b1(sel), b1(win),
                                    _feature_major(cache_win_kv[l]), _feature_major(cache_sel_kv[l]),
                                    q_pos=q_pos, n_past_blocks=n_past_blocks)
        conv_rows = [state_conv[l][:, k] for k in range(CONV_W - 1)]
        h2s, h_new = _mix_sample(xr, conv_rows, state_h[l], gr, ga, gb, o_attn.reshape(bs, Q_W), h1s, mixw, alpha=alpha)
        xs = _ffn_half_step(h2s, *ffn2, alpha=alpha, tm=bs)
        outs[1].append(cmp.reshape(bs, ts, 2, N_KV, HEAD_DIM))
        outs[3].append(sel.reshape(bs, ts, 2, N_KV, HEAD_DIM))
        win_all = jnp.concatenate([cache_win_kv[l], win.reshape(bs, ts, 2, N_KV, HEAD_DIM)], axis=1)
        outs[5].append(win_all[:, ts:])
        outs[7].append(jnp.concatenate([state_conv[l], xr[:, None, :]], axis=1)[:, ts:])
        outs[9].append(h_new)

    stacked = [jnp.stack(o) for o in outs]
    cmp_p, cmp_s, sel_p, sel_s, win_p, win_s, conv_p, conv_s, h_p, h_s = stacked
    return (xp.reshape(bp, tp, d), xs.reshape(bs, ts, d), cmp_p, cmp_s, sel_p, sel_s, win_p, win_s,
            conv_p, conv_s, h_p, h_s)
```

```python
import functools
import math

import numpy as np
import jax
import jax.numpy as jnp
from jax import lax
from jax.experimental import pallas as pl
from jax.experimental.pallas import tpu as pltpu

F32 = jnp.float32
BF16 = jnp.bfloat16

N_HEADS = 8
HEAD_DIM = 64
N_KV = 2
HPG = N_HEADS // N_KV
Q_W = N_HEADS * HEAD_DIM
KV_W = N_KV * HEAD_DIM
CMP_BLOCK = 32
CMP_STRIDE = 16
CMP_RATIO = CMP_BLOCK // CMP_STRIDE
CMP_HID = 64
SEL_BLOCK = 64
N_SEL = 16
SEL_BONUS = 1.0e4
WINDOW = 512
ROPE_THETA = 10000.0
RG_BLOCKS = 16
RG_C = 8.0
CONV_W = 4
PAGE_SIZE = 128
LN_EPS = 1e-5
SCALE = HEAD_DIM ** -0.5
QK_SCALE_LOG2 = SCALE * math.log2(math.e)
NEG_INF = float("-inf")

LANES = 128
SUBLANES = 8
MXU_DIM = 256
VMEM_LIMIT_BYTES = 56 * 1024 * 1024

ROW_W = 2 * KV_W
P_W = 2 * N_KV * CMP_RATIO * CMP_HID
GATE_PAD = LANES
PAGES_PER_STEP = 16
BLOCKS_PER_PAGE = PAGE_SIZE // SEL_BLOCK


def _dot(a, b):
    return jnp.dot(a.astype(BF16), b.astype(BF16), preferred_element_type=F32)


def _dot_nt(a, b):
    return lax.dot_general(a.astype(BF16), b.astype(BF16), (((1,), (1,)), ((), ())),
                           preferred_element_type=F32)


def _dot_f32(a, b):
    return jnp.dot(a, b, preferred_element_type=F32, precision=lax.Precision.HIGHEST)


def _sigmoid(x):
    return 0.5 * jnp.tanh(0.5 * x) + 0.5


def _gelu(x):
    return 0.5 * x * (1.0 + jnp.tanh(math.sqrt(2.0 / math.pi) * (x + 0.044715 * (x * x * x))))


def _layer_norm(y, g, b):
    mu = jnp.mean(y, axis=-1, keepdims=True)
    yc = y - mu
    var = jnp.mean(yc * yc, axis=-1, keepdims=True)
    return yc * lax.rsqrt(var + LN_EPS) * g + b


def _softmax_parts(s, mask):
    s = jnp.where(mask, s, NEG_INF)
    m = jnp.max(s, axis=-1, keepdims=True)
    m = jnp.where(m > NEG_INF, m, 0.0)
    e = jnp.exp(s - m)
    d = jnp.sum(e, axis=-1, keepdims=True)
    return e, jnp.where(d > 0, d, 1.0)


def _const_spec(shape):
    nd = len(shape)
    return pl.BlockSpec(shape, lambda *_: (0,) * nd)


def _params(*sem):
    return pltpu.CompilerParams(dimension_semantics=sem, vmem_limit_bytes=VMEM_LIMIT_BYTES)


def _ffn_body(x_ref, wg_ref, wu_ref, wd_ref, g_ref, b_ref, o_ref, *, alpha):
    x = x_ref[...]
    xb = x.astype(BF16)
    gate = jnp.dot(xb, wg_ref[...], preferred_element_type=F32)
    up = jnp.dot(xb, wu_ref[...], preferred_element_type=F32)
    hmid = gate * _sigmoid(gate) * up
    f = _dot(hmid, wd_ref[...])
    o_ref[...] = _layer_norm(alpha * x + 0.5 * f, g_ref[...], b_ref[...])


def _ffn_half_step(x, wg, wu, wd, g, b, *, alpha, tm):
    n, d = x.shape
    ff = wg.shape[1]
    return pl.pallas_call(
        functools.partial(_ffn_body, alpha=alpha),
        grid=(n // tm,),
        in_specs=[pl.BlockSpec((tm, d), lambda i: (i, 0)),
                  _const_spec((d, ff)), _const_spec((d, ff)), _const_spec((ff, d)),
                  _const_spec((1, d)), _const_spec((1, d))],
        out_specs=pl.BlockSpec((tm, d), lambda i: (i, 0)),
        out_shape=jax.ShapeDtypeStruct((n, d), F32),
        compiler_params=_params("parallel"),
        name="ffn_half_step",
    )(x, wg, wu, wd, g, b)


def _rope(x, cos, sin):
    w = x.shape[1]
    reps = w // LANES
    c = jnp.tile(cos, (1, reps)) if reps > 1 else cos
    s = jnp.tile(sin, (1, reps)) if reps > 1 else sin
    lane = lax.broadcasted_iota(jnp.int32, x.shape, 1)
    first_half = (lane % HEAD_DIM) < (HEAD_DIM // 2)
    swapped = jnp.where(first_half, pltpu.roll(x, w - HEAD_DIM // 2, 1), pltpu.roll(x, HEAD_DIM // 2, 1))
    return x * c + swapped * s


def _inproj_body(h_ref, cos_ref, sin_ref, wa_ref, wb_ref, *out_refs, feature_major):
    hb = h_ref[...].astype(BF16)
    cos = cos_ref[...]
    sin = sin_ref[...]
    za = jnp.dot(hb, wa_ref[...], preferred_element_type=F32)
    q = za[:, :Q_W]
    kv = za[:, Q_W:Q_W + 6 * KV_W]
    gates = _sigmoid(za[:, Q_W + 6 * KV_W:])
    q_rot = _rope(q, cos, sin)
    cmp = kv[:, :2 * KV_W]
    sel_k = _rope(kv[:, 2 * KV_W:3 * KV_W], cos, sin)
    win_k = _rope(kv[:, 4 * KV_W:5 * KV_W], cos, sin)
    sel = jnp.concatenate([sel_k, kv[:, 3 * KV_W:4 * KV_W]], axis=1)
    win = jnp.concatenate([win_k, kv[:, 5 * KV_W:6 * KV_W]], axis=1)
    if feature_major:
        (cmp_ref, selk_ref, wink_ref, xr_ref, gr_ref, ga_ref, gb_ref,
         qrawt_ref, qrott_ref, gatest_ref, cmpt_ref, selt_ref, wint_ref) = out_refs
        cmp_ref[...] = cmp
        selk_ref[...] = sel_k.astype(BF16)
        wink_ref[...] = win_k.astype(BF16)
        qrawt_ref[...] = q.T
        qrott_ref[...] = q_rot.T
        gatest_ref[...] = gates.T
        cmpt_ref[...] = cmp.T
        selt_ref[...] = sel.T
        wint_ref[...] = win.T
    else:
        qraw_ref, qrot_ref, cmp_ref, sel_ref, win_ref, gates_ref, xr_ref, gr_ref, ga_ref, gb_ref = out_refs
        qraw_ref[...] = q
        qrot_ref[...] = q_rot
        cmp_ref[...] = cmp
        sel_ref[...] = sel
        win_ref[...] = win
        gates_ref[...] = gates
    zb = jnp.dot(hb, wb_ref[...], preferred_element_type=F32)
    d = xr_ref.shape[1]
    xr_ref[...] = zb[:, :d]
    gr_ref[...] = zb[:, d:2 * d]
    ga_ref[...] = zb[:, 2 * d:3 * d]
    gb_ref[...] = zb[:, 3 * d:]


def _in_proj(h, cos, sin, wa, wb, *, tm, seq_len, feature_major):
    n, d = h.shape
    wa_w, wb_w = wa.shape[1], wb.shape[1]
    tiles = seq_len // tm
    row = lambda i: (i, 0)
    tab = lambda i: (i % tiles, 0)
    if feature_major:
        rows = [(ROW_W, F32), (KV_W, BF16), (KV_W, BF16), (d, F32), (d, F32), (d, F32), (d, F32)]
        cols = [Q_W, Q_W, GATE_PAD, ROW_W, ROW_W, ROW_W]
    else:
        rows = [(w, F32) for w in (Q_W, Q_W, ROW_W, ROW_W, ROW_W, GATE_PAD, d, d, d, d)]
        cols = []
    out_specs = [pl.BlockSpec((tm, w), row) for w, _ in rows]
    out_shape = [jax.ShapeDtypeStruct((n, w), dt) for w, dt in rows]
    out_specs += [pl.BlockSpec((None, w, tm), lambda i: (i // tiles, 0, i % tiles)) for w in cols]
    out_shape += [jax.ShapeDtypeStruct((n // seq_len, w, seq_len), F32) for w in cols]
    return pl.pallas_call(
        functools.partial(_inproj_body, feature_major=feature_major),
        grid=(n // tm,),
        in_specs=[pl.BlockSpec((tm, d), row), pl.BlockSpec((tm, LANES), tab), pl.BlockSpec((tm, LANES), tab),
                  _const_spec((d, wa_w)), _const_spec((d, wb_w))],
        out_specs=out_specs,
        out_shape=out_shape,
        compiler_params=_params("parallel"),
        name="in_proj",
    )(h, cos, sin, wa, wb)


def _cmp_chunks(rows_of, w_ref, kv, nch):
    acc = None
    for s in range(CMP_STRIDE):
        part = _dot(rows_of(s, nch), w_ref[kv, s])
        acc = part if acc is None else acc + part
    return acc


def _cmp_stage1_body(x_ref, w_ref, o_ref, rows_ref):
    nch = x_ref.shape[0] // CMP_STRIDE
    half = P_W // 2
    for kv in range(2):
        rows_ref[kv] = x_ref[:, kv * KV_W:(kv + 1) * KV_W]
        rows_of = lambda s, n, kv=kv: rows_ref[kv, pl.ds(s, n, stride=CMP_STRIDE), :]
        o_ref[:, kv * half:(kv + 1) * half] = _cmp_chunks(rows_of, w_ref, kv, nch)


def _cmp_stage1(x, w):
    b, t, _ = x.shape
    nch = t // CMP_STRIDE
    return pl.pallas_call(
        _cmp_stage1_body,
        grid=(b,),
        in_specs=[pl.BlockSpec((None, t, ROW_W), lambda i: (i, 0, 0)), _const_spec(w.shape)],
        out_specs=pl.BlockSpec((None, nch, P_W), lambda i: (i, 0, 0)),
        out_shape=jax.ShapeDtypeStruct((b, nch, P_W), F32),
        scratch_shapes=[pltpu.VMEM((2, t, KV_W), F32)],
        compiler_params=_params("parallel"),
        name="cmp_stage1",
    )(x, w)


def _cmp_paged_body(pt_ref, *refs):
    del pt_ref
    x_refs = refs[:PAGES_PER_STEP]
    w_ref, o_ref, rows_ref = refs[PAGES_PER_STEP:]
    for k, x_ref in enumerate(x_refs):
        for kv in range(2):
            rows_ref[kv, k * PAGE_SIZE:(k + 1) * PAGE_SIZE, :] = x_ref[kv].T
    nch = PAGES_PER_STEP * PAGE_SIZE // CMP_STRIDE
    half = P_W // 2
    for kv in range(2):
        rows_of = lambda s, n, kv=kv: rows_ref[kv, pl.ds(s, n, stride=CMP_STRIDE), :]
        o_ref[:, kv * half:(kv + 1) * half] = _cmp_chunks(rows_of, w_ref, kv, nch)


def _cmp_stage1_paged(cache_t, page_table, w):
    b, n_pages = page_table.shape
    steps = n_pages // PAGES_PER_STEP
    cpp = PAGE_SIZE // CMP_STRIDE
    nch = n_pages * cpp

    def page_spec(k):
        return pl.BlockSpec((None, 2, KV_W, PAGE_SIZE), lambda i, s, pt: (pt[i, s * PAGES_PER_STEP + k], 0, 0, 0))

    return pl.pallas_call(
        _cmp_paged_body,
        grid_spec=pltpu.PrefetchScalarGridSpec(
            num_scalar_prefetch=1,
            grid=(b, steps),
            in_specs=[page_spec(k) for k in range(PAGES_PER_STEP)]
            + [pl.BlockSpec(w.shape, lambda i, s, pt: (0,) * w.ndim)],
            out_specs=pl.BlockSpec((None, PAGES_PER_STEP * cpp, P_W), lambda i, s, pt: (i, s, 0)),
            scratch_shapes=[pltpu.VMEM((2, PAGES_PER_STEP * PAGE_SIZE, KV_W), F32)],
        ),
        out_shape=jax.ShapeDtypeStruct((b, nch, P_W), F32),
        compiler_params=_params("parallel", "parallel"),
        name="cmp_stage1_paged",
    )(page_table, *([cache_t] * PAGES_PER_STEP), w)


def _cmp_finish_body(p_ref, pek_ref, w1k_ref, w2k_ref, pev_ref, w1v_ref, w2v_ref, o_ref, ot_ref, *, nc):
    p = p_ref[...]
    nch = p.shape[0]
    row = lax.broadcasted_iota(jnp.int32, (nch, CMP_HID), 0)
    outs = []
    for kv, (pe_ref, w1_ref, w2_ref) in enumerate(((pek_ref, w1k_ref, w2k_ref), (pev_ref, w1v_ref, w2v_ref))):
        pe = jnp.broadcast_to(pe_ref[...], (SUBLANES, pe_ref.shape[1]))
        bias = _dot(pe, w1_ref[...])[0:1]
        for g in range(N_KV):
            c0 = (kv * N_KV + g) * CMP_RATIO * CMP_HID
            u = bias + p[:, c0:c0 + CMP_HID]
            for r in range(1, CMP_RATIO):
                u = u + pltpu.roll(p[:, c0 + r * CMP_HID:c0 + (r + 1) * CMP_HID], nch - r, 0)
            out = _dot(_gelu(u), w2_ref[...])
            outs.append(jnp.where(row < nc, out, 0.0))
    ckv = jnp.concatenate(outs, axis=1)
    o_ref[...] = ckv
    ot_ref[...] = ckv.T


def _cmp_finish(p, pe_k, w1_k, w2_k, pe_v, w1_v, w2_v, *, nc):
    b, nch, _ = p.shape
    flat = CMP_BLOCK * HEAD_DIM
    return pl.pallas_call(
        functools.partial(_cmp_finish_body, nc=nc),
        grid=(b,),
        in_specs=[pl.BlockSpec((None, nch, P_W), lambda i: (i, 0, 0)),
                  _const_spec((1, flat)), _const_spec((flat, CMP_HID)), _const_spec((CMP_HID, HEAD_DIM)),
                  _const_spec((1, flat)), _const_spec((flat, CMP_HID)), _const_spec((CMP_HID, HEAD_DIM))],
        out_specs=[pl.BlockSpec((None, nch, ROW_W), lambda i: (i, 0, 0)),
                   pl.BlockSpec((None, ROW_W, nch), lambda i: (i, 0, 0))],
        out_shape=[jax.ShapeDtypeStruct((b, nch, ROW_W), F32), jax.ShapeDtypeStruct((b, ROW_W, nch), F32)],
        compiler_params=_params("parallel"),
        name="cmp_finish",
    )(p, pe_k, w1_k, w2_k, pe_v, w1_v, w2_v)


def _block_scores(imp, q_pos, nsel_valid, block_axis=1):
    jblk = lax.broadcasted_iota(jnp.int32, imp.shape, block_axis)
    cur = q_pos // SEL_BLOCK
    valid = (jblk * SEL_BLOCK <= q_pos) & (jblk < nsel_valid)
    forced = (jblk == 0) | (jblk == cur) | (jblk == cur - 1)
    return jnp.where(valid, imp + jnp.where(forced, SEL_BONUS, 0.0), NEG_INF)


def _nsa_prompt_body(qrawt_ref, qrott_ref, gatest_ref, ckv_ref, ckvt_ref, selk_ref, selt_ref, wink_ref, wint_ref,
                     covert_ref, expandt_ref, o_ref, *, tq, nc):
    nchp = ckv_ref.shape[0]
    nsel = covert_ref.shape[0]
    tile = pl.program_id(1)
    s0 = tile * tq
    lane_t = lax.broadcasted_iota(jnp.int32, (1, tq), 1)
    key_sub = lax.broadcasted_iota(jnp.int32, (tq, 1), 0)
    qpos = s0 + lane_t
    qpos_heads = jnp.concatenate([qpos] * HPG, axis=1)
    causal_diag = key_sub <= lane_t
    window_edge = lane_t <= key_sub
    cidx = lax.broadcasted_iota(jnp.int32, (nchp, 1), 0)
    jblk = lax.broadcasted_iota(jnp.int32, (nsel, 1), 0)
    m_c = (cidx * CMP_STRIDE + (CMP_BLOCK - 1) <= qpos_heads) & (cidx < nc)
    n_win_chunks = WINDOW // tq
    out_heads = []
    for g in range(N_KV):
        heads = range(g * HPG, (g + 1) * HPG)
        vrow = slice(KV_W + g * HEAD_DIM, KV_W + (g + 1) * HEAD_DIM)

        def group_rows(qt):
            z = jnp.zeros_like(qt)
            return jnp.concatenate([qt, z] if g == 0 else [z, qt], axis=0)

        head_rows = lambda ref, hd: ref[hd * HEAD_DIM:(hd + 1) * HEAD_DIM, :] * QK_SCALE_LOG2
        q_rot = [group_rows(head_rows(qrott_ref, hd).astype(BF16)) for hd in heads]
        q_raw = group_rows(jnp.concatenate([head_rows(qrawt_ref, hd) for hd in heads], axis=1).astype(BF16))
        s_c = jnp.dot(ckv_ref[:, :KV_W].astype(BF16), q_raw, preferred_element_type=F32)
        s_c = jnp.where(m_c, s_c, NEG_INF)
        mx = jnp.max(s_c, axis=0, keepdims=True)
        e_c = jnp.exp2(s_c - jnp.where(mx > NEG_INF, mx, 0.0))
        d_c = jnp.sum(e_c, axis=0, keepdims=True)
        p_c = e_c * (1.0 / jnp.where(d_c > 0, d_c, 1.0))
        o_c = _dot(ckvt_ref[vrow, :], p_c)
        p_sum = p_c[:, 0:tq]
        for h in range(1, HPG):
            p_sum = p_sum + p_c[:, h * tq:(h + 1) * tq]
        imp = _dot_f32(covert_ref[...], p_sum)
        score = _block_scores(imp, qpos, nsel, block_axis=0)
        rank = jnp.zeros((nsel, tq), jnp.int32)
        for i in range(nsel):
            s_i = score[i:i + 1, :]
            beats = (s_i > score) | ((s_i == score) & (i < jblk))
            rank = rank + beats.astype(jnp.int32)
        chosen = (rank < N_SEL).astype(BF16)

        def chunk_step(k_ref, vt_ref, mask_of):
            def step(kc, carry):
                k0 = pl.multiple_of(kc * tq, tq)
                keys = k_ref[pl.ds(k0, tq), :]
                values_t = vt_ref[vrow, pl.ds(k0, tq)].astype(BF16)
                mask = mask_of(k0)
                scores = [jnp.dot(keys, q_rot[h], preferred_element_type=F32) for h in range(HPG)]
                stats, probs = [], []
                for h in range(HPG):
                    m, l, _ = carry[h]
                    s = scores[h] if mask is None else jnp.where(mask, scores[h], NEG_INF)
                    m_new = jnp.maximum(m, jnp.max(s, axis=0, keepdims=True))
                    m_safe = m_new if mask is None else jnp.where(m_new > NEG_INF, m_new, 0.0)
                    alpha = jnp.exp2(m - m_safe)
                    p = jnp.exp2(s - m_safe)
                    stats.append((m_new, l * alpha + jnp.sum(p, axis=0, keepdims=True), alpha))
                    probs.append(p.astype(BF16))
                pv = [jnp.dot(values_t, probs[h], preferred_element_type=F32) for h in range(HPG)]
                return tuple((stats[h][0], stats[h][1], carry[h][2] * stats[h][2] + pv[h]) for h in range(HPG))
            return step

        def finish(carry):
            return [acc * (1.0 / jnp.where(l > 0, l, 1.0)) for _, l, acc in carry]

        init = tuple((jnp.full((1, tq), NEG_INF, F32), jnp.zeros((1, tq), F32), jnp.zeros((HEAD_DIM, tq), F32))
                     for _ in range(HPG))

        def picked(k0):
            return jnp.dot(expandt_ref[pl.ds(k0, tq), :], chosen, preferred_element_type=F32) > 0.5

        carry = lax.fori_loop(0, tile, chunk_step(selk_ref, selt_ref, picked), init)
        o_s = finish(chunk_step(selk_ref, selt_ref, lambda k0: picked(k0) & causal_diag)(tile, carry))

        edge = tile - n_win_chunks
        carry = lax.fori_loop(jnp.maximum(edge, 0), jnp.maximum(edge + 1, 0),
                              chunk_step(wink_ref, wint_ref, lambda k0: window_edge), init)
        carry = lax.fori_loop(jnp.maximum(edge + 1, 0), tile, chunk_step(wink_ref, wint_ref, lambda k0: None), carry)
        o_w = finish(chunk_step(wink_ref, wint_ref, lambda k0: causal_diag)(tile, carry))

        for h, hd in enumerate(heads):
            gate = lambda branch: gatest_ref[3 * hd + branch:3 * hd + branch + 1, :]
            out_heads.append(gate(0) * o_c[:, h * tq:(h + 1) * tq] + gate(1) * o_s[h] + gate(2) * o_w[h])
    o_ref[...] = jnp.concatenate(out_heads, axis=0).T


def _nsa_prompt(q_raw_t, q_rot_t, gates_t, ckv, ckv_t, sel_k, sel_t, win_k, win_t, cover_t, expand_t, *, tq, nc):
    b, _, t = q_raw_t.shape
    nchp = ckv.shape[1]
    nsel = cover_t.shape[0]
    assert WINDOW % tq == 0
    tile_t = lambda w: pl.BlockSpec((None, w, tq), lambda i, j: (i, 0, j))
    whole = lambda r, w: pl.BlockSpec((None, r, w), lambda i, j: (i, 0, 0))
    return pl.pallas_call(
        functools.partial(_nsa_prompt_body, tq=tq, nc=nc),
        grid=(b, t // tq),
        in_specs=[tile_t(Q_W), tile_t(Q_W), tile_t(GATE_PAD), whole(nchp, ROW_W), whole(ROW_W, nchp),
                  whole(t, KV_W), whole(ROW_W, t), whole(t, KV_W), whole(ROW_W, t),
                  _const_spec((nsel, nchp)), _const_spec((t, nsel))],
        out_specs=pl.BlockSpec((None, tq, Q_W), lambda i, j: (i, j, 0)),
        out_shape=jax.ShapeDtypeStruct((b, t, Q_W), F32),
        compiler_params=_params("parallel", "arbitrary"),
        name="nsa_prompt",
    )(q_raw_t, q_rot_t, gates_t, ckv, ckv_t, sel_k, sel_t, win_k, win_t, cover_t, expand_t)


def _softplus(x):
    return jnp.maximum(x, 0.0) + jnp.log1p(jnp.exp(-jnp.abs(x)))


def _rg_gates(xc, wgate_ref, ba, bx, lam):
    xcb = xc.astype(BF16)
    n_grp = wgate_ref.shape[0]
    gw = wgate_ref.shape[1]
    za, zx = [], []
    for k in range(n_grp):
        z = jnp.dot(xcb[:, k * gw:(k + 1) * gw], wgate_ref[k], preferred_element_type=F32)
        za.append(z[:, :gw])
        zx.append(z[:, gw:])
    r = _sigmoid(jnp.concatenate(za, axis=1) + ba)
    i = _sigmoid(jnp.concatenate(zx, axis=1) + bx)
    log_a = -RG_C * r * _softplus(-lam)
    a = jnp.exp(log_a)
    u = jnp.sqrt(-jnp.tanh(log_a) * (1.0 + a * a)) * i * xc
    return a, u


def _merge(h_res, hs, gr, ga, gb, o_attn, wbra_ref, wbrr_ref, wout_ref, g2, b2, alpha):
    y_rnn = hs * _gelu(gr)
    m = _sigmoid(ga) * _dot(o_attn, wbra_ref[...]) + _sigmoid(gb) * _dot(y_rnn, wbrr_ref[...])
    return _layer_norm(alpha * h_res + _dot(m, wout_ref[...]), g2, b2)


def _mix_prompt_body(xr_ref, gr_ref, ga_ref, gb_ref, oat_ref, h1_ref, convw_ref, convb_ref, wgate_ref,
                     ba_ref, bx_ref, lam_ref, wbra_ref, wbrr_ref, wout_ref, g2_ref, b2_ref,
                     o_ref, hlast_ref, hc_ref, tail_ref, *, tt, alpha):
    @pl.when(pl.program_id(1) == 0)
    def _():
        hc_ref[...] = jnp.zeros_like(hc_ref)
        tail_ref[...] = jnp.zeros_like(tail_ref)

    x = xr_ref[...]
    convw = convw_ref[...]
    sub = lax.broadcasted_iota(jnp.int32, (SUBLANES, x.shape[1]), 0)
    shifts = range(1, CONV_W)
    prev_rot = [pltpu.roll(tail_ref[...], d, 0) for d in shifts]
    xc_groups = []
    for k in range(tt // SUBLANES):
        x_grp = x[k * SUBLANES:(k + 1) * SUBLANES]
        rot = [pltpu.roll(x_grp, d, 0) for d in shifts]
        xc_grp = convb_ref[...] + convw[CONV_W - 1:CONV_W] * x_grp
        for d, r_prev, r_cur in zip(shifts, prev_rot, rot):
            xc_grp = xc_grp + convw[CONV_W - 1 - d:CONV_W - d] * jnp.where(sub < d, r_prev, r_cur)
        xc_groups.append(xc_grp)
        prev_rot = rot
    xc = jnp.concatenate(xc_groups, axis=0)
    tail_ref[...] = x[tt - SUBLANES:tt]

    a, u = _rg_gates(xc, wgate_ref, ba_ref[...], bx_ref[...], lam_ref[...])
    sub = lax.broadcasted_iota(jnp.int32, (SUBLANES, a.shape[1]), 0)
    h_prev = jnp.broadcast_to(hc_ref[...], sub.shape)
    groups = []
    for k in range(tt // SUBLANES):
        rows = slice(k * SUBLANES, (k + 1) * SUBLANES)
        a_grp, u_grp = a[rows], u[rows]
        d = 1
        while d < SUBLANES:
            inside = sub >= d
            u_grp = jnp.where(inside, a_grp * pltpu.roll(u_grp, d, 0) + u_grp, u_grp)
            a_grp = jnp.where(inside, a_grp * pltpu.roll(a_grp, d, 0), a_grp)
            d *= 2
        h_grp = a_grp * h_prev + u_grp
        groups.append(h_grp)
        h_prev = jnp.broadcast_to(h_grp[SUBLANES - 1:SUBLANES], h_grp.shape)
    hs = jnp.concatenate(groups, axis=0)
    hc_ref[...] = hs[tt - 1:tt]
    hlast_ref[...] = hs[tt - 1:tt]
    o_ref[...] = _merge(h1_ref[...], hs, gr_ref[...], ga_ref[...], gb_ref[...], oat_ref[...],
                        wbra_ref, wbrr_ref, wout_ref, g2_ref[...], b2_ref[...], alpha)


def _mix_weight_specs(d, n_grp, gw):
    return [_const_spec((SUBLANES, d)), _const_spec((1, d)), _const_spec((n_grp, gw, 2 * gw)),
            _const_spec((1, d)), _const_spec((1, d)), _const_spec((1, d)),
            _const_spec((Q_W, d)), _const_spec((d, d)), _const_spec((d, d)),
            _const_spec((1, d)), _const_spec((1, d))]


def _mix_prompt(xr, gr, ga, gb, o_attn, h1, mixw, *, tt, alpha):
    b, t, d = xr.shape
    n_grp, gw = mixw[2].shape[:2]
    tile = lambda w: pl.BlockSpec((None, tt, w), lambda i, j: (i, j, 0))
    return pl.pallas_call(
        functools.partial(_mix_prompt_body, tt=tt, alpha=alpha),
        grid=(b, t // tt),
        in_specs=[tile(d), tile(d), tile(d), tile(d), tile(Q_W), tile(d)] + _mix_weight_specs(d, n_grp, gw),
        out_specs=[tile(d), pl.BlockSpec((None, 1, d), lambda i, j: (i, 0, 0))],
        out_shape=[jax.ShapeDtypeStruct((b, t, d), F32), jax.ShapeDtypeStruct((b, 1, d), F32)],
        scratch_shapes=[pltpu.VMEM((1, d), F32), pltpu.VMEM((SUBLANES, d), F32)],
        compiler_params=_params("parallel", "arbitrary"),
        name="mix_prompt",
    )(xr, gr, ga, gb, o_attn, h1, *mixw)


def _mix_sample_body(xr_ref, c0_ref, c1_ref, c2_ref, h0_ref, gr_ref, ga_ref, gb_ref, oat_ref, h1_ref,
                     convw_ref, convb_ref, wgate_ref, ba_ref, bx_ref, lam_ref, wbra_ref, wbrr_ref, wout_ref,
                     g2_ref, b2_ref, o_ref, hnew_ref, *, alpha):
    convw = convw_ref[...]
    xc = (convb_ref[...] + convw[0:1] * c0_ref[...] + convw[1:2] * c1_ref[...] + convw[2:3] * c2_ref[...]
          + convw[3:4] * xr_ref[...])
    a, u = _rg_gates(xc, wgate_ref, ba_ref[...], bx_ref[...], lam_ref[...])
    hs = a * h0_ref[...] + u
    hnew_ref[...] = hs
    o_ref[...] = _merge(h1_ref[...], hs, gr_ref[...], ga_ref[...], gb_ref[...], oat_ref[...],
                        wbra_ref, wbrr_ref, wout_ref, g2_ref[...], b2_ref[...], alpha)


def _mix_sample(xr, conv_rows, h0, gr, ga, gb, o_attn, h1, mixw, *, alpha):
    b, d = xr.shape
    n_grp, gw = mixw[2].shape[:2]
    full = lambda w: _const_spec((b, w))
    return pl.pallas_call(
        functools.partial(_mix_sample_body, alpha=alpha),
        grid=(1,),
        in_specs=[full(d)] * 8 + [full(Q_W), full(d)] + _mix_weight_specs(d, n_grp, gw),
        out_specs=[full(d), full(d)],
        out_shape=[jax.ShapeDtypeStruct((b, d), F32), jax.ShapeDtypeStruct((b, d), F32)],
        compiler_params=_params("arbitrary"),
        name="mix_sample",
    )(xr, *conv_rows, h0, gr, ga, gb, o_attn, h1, *mixw)


def _head_rows(row, g):
    parts = [row[:, (g * HPG + h) * HEAD_DIM:(g * HPG + h + 1) * HEAD_DIM] for h in range(HPG)]
    return jnp.concatenate(parts + [jnp.zeros((SUBLANES - HPG, HEAD_DIM), F32)], axis=0)


def _nsa_sample_scores_body(qraw_ref, ckv_ref, cover_ref, oc_ref, score_ref, *, nc, nsel, q_pos):
    nchp = ckv_ref.shape[0]
    nselp = cover_ref.shape[1]
    cidx = lax.broadcasted_iota(jnp.int32, (1, nchp), 1)
    qrow = qraw_ref[...] * SCALE
    m_c = (cidx * CMP_STRIDE + (CMP_BLOCK - 1) <= q_pos) & (cidx < nc)
    oc_parts, score_rows = [], []
    for g in range(N_KV):
        q = _head_rows(qrow, g)
        s_c = _dot_nt(q, ckv_ref[:, g * HEAD_DIM:(g + 1) * HEAD_DIM])
        e_c, d_c = _softmax_parts(s_c, m_c)
        p_c = e_c / d_c
        o_c = _dot(p_c, ckv_ref[:, KV_W + g * HEAD_DIM:KV_W + (g + 1) * HEAD_DIM])
        oc_parts += [o_c[h:h + 1] for h in range(HPG)]
        p_sum = jnp.sum(p_c[0:HPG], axis=0, keepdims=True)
        imp = _dot_f32(jnp.broadcast_to(p_sum, (SUBLANES, nchp)), cover_ref[...])[0:1]
        score_rows.append(_block_scores(imp, q_pos, nsel))
    oc_ref[...] = jnp.concatenate(oc_parts, axis=1)
    score_ref[...] = jnp.concatenate(score_rows + [jnp.full((SUBLANES - N_KV, nselp), NEG_INF, F32)], axis=0)


def _nsa_sample_scores(q_raw, ckv, cover, *, nc, nsel, q_pos):
    b = q_raw.shape[0]
    nchp = ckv.shape[1]
    nselp = cover.shape[1]
    return pl.pallas_call(
        functools.partial(_nsa_sample_scores_body, nc=nc, nsel=nsel, q_pos=q_pos),
        grid=(b,),
        in_specs=[pl.BlockSpec((None, 1, Q_W), lambda i: (i, 0, 0)),
                  pl.BlockSpec((None, nchp, ROW_W), lambda i: (i, 0, 0)), _const_spec((nchp, nselp))],
        out_specs=[pl.BlockSpec((None, 1, Q_W), lambda i: (i, 0, 0)),
                   pl.BlockSpec((None, SUBLANES, nselp), lambda i: (i, 0, 0))],
        out_shape=[jax.ShapeDtypeStruct((b, 1, Q_W), F32), jax.ShapeDtypeStruct((b, SUBLANES, nselp), F32)],
        compiler_params=_params("parallel"),
        name="nsa_sample_scores",
    )(q_raw, ckv, cover)


def _top_blocks_body(score_ref, idx_ref):
    score = score_ref[...]
    n, nselp = score.shape
    lane = lax.broadcasted_iota(jnp.int32, (n, nselp), 1)
    lane_out = lax.broadcasted_iota(jnp.int32, (n, LANES), 1)
    picked = jnp.zeros((n, LANES), jnp.int32)
    for r in range(N_SEL):
        m = jnp.max(score, axis=1, keepdims=True)
        j = jnp.min(jnp.where(score == m, lane, nselp), axis=1, keepdims=True)
        picked = jnp.where(lane_out == r, j, picked)
        score = jnp.where(lane == j, NEG_INF, score)
    idx_ref[...] = picked


def _top_blocks(score):
    n, nselp = score.shape
    return pl.pallas_call(
        _top_blocks_body,
        grid=(1,),
        in_specs=[_const_spec((n, nselp))],
        out_specs=_const_spec((n, LANES)),
        out_shape=jax.ShapeDtypeStruct((n, LANES), jnp.int32),
        compiler_params=_params("arbitrary"),
        name="top_blocks",
    )(score)


def _nsa_sample_attend_body(idx_ref, pt_ref, qrot_ref, gates_ref, oc_ref, selnew_ref, winnew_ref, wincache_ref,
                            *refs, q_pos, n_past_blocks):
    del pt_ref
    page_refs, o_ref = refs[:-1], refs[-1]
    b = pl.program_id(0)
    wb = wincache_ref.shape[2]
    qrow = qrot_ref[...] * SCALE
    gates = gates_ref[...]
    oc = oc_ref[...]
    sel_new = selnew_ref[...]
    win_new = winnew_ref[...]
    page_lane = lax.broadcasted_iota(jnp.int32, (1, PAGE_SIZE), 1)
    wpos = (q_pos - wb) + lax.broadcasted_iota(jnp.int32, (1, wb), 1)
    m_w = (q_pos - wpos <= WINDOW) & (wpos >= 0) & (wpos <= q_pos)
    cur = q_pos // SEL_BLOCK
    out_parts = []
    for g in range(N_KV):
        q = _head_rows(qrow, g)
        feat = slice(g * HEAD_DIM, (g + 1) * HEAD_DIM)
        kcol = slice(g * HEAD_DIM, (g + 1) * HEAD_DIM)
        vcol = slice(KV_W + g * HEAD_DIM, KV_W + (g + 1) * HEAD_DIM)
        kts, vts, kps = [], [], []
        new_chosen = None
        for n in range(N_SEL):
            j = idx_ref[b, g * N_SEL + n]
            page = page_refs[g * N_SEL + n]
            kts.append(page[0, feat, :])
            vts.append(page[1, feat, :])
            in_block = (page_lane // SEL_BLOCK == j % BLOCKS_PER_PAGE) & (j < n_past_blocks)
            kps.append(jnp.where(in_block, (j // BLOCKS_PER_PAGE) * PAGE_SIZE + page_lane, q_pos + 1))
            hit = j == cur
            new_chosen = hit if new_chosen is None else (new_chosen | hit)
        s_s = _dot(q, jnp.concatenate(kts, axis=1))
        s_new = jnp.where(new_chosen, jnp.sum(q * sel_new[:, kcol], axis=1, keepdims=True), NEG_INF)
        s_s = jnp.where(jnp.concatenate(kps, axis=1) <= q_pos, s_s, NEG_INF)
        m = jnp.maximum(jnp.max(s_s, axis=1, keepdims=True), s_new)
        m = jnp.where(m > NEG_INF, m, 0.0)
        e_s = jnp.exp(s_s - m)
        e_new = jnp.exp(s_new - m)
        d_s = jnp.sum(e_s, axis=1, keepdims=True) + e_new
        d_s = jnp.where(d_s > 0, d_s, 1.0)
        o_s = (_dot_nt(e_s, jnp.concatenate(vts, axis=1)) + e_new * sel_new[:, vcol]) / d_s
        s_w = jnp.where(m_w, _dot(q, wincache_ref[0, feat, :]), NEG_INF)
        s_wn = jnp.sum(q * win_new[:, kcol], axis=1, keepdims=True)
        m = jnp.maximum(jnp.max(s_w, axis=1, keepdims=True), s_wn)
        e_w = jnp.exp(s_w - m)
        e_wn = jnp.exp(s_wn - m)
        d_w = jnp.sum(e_w, axis=1, keepdims=True) + e_wn
        o_w = (_dot_nt(e_w, wincache_ref[1, feat, :]) + e_wn * win_new[:, vcol]) / d_w
        for h in range(HPG):
            hd = g * HPG + h
            out_parts.append(gates[:, 3 * hd:3 * hd + 1] * oc[:, hd * HEAD_DIM:(hd + 1) * HEAD_DIM]
                             + gates[:, 3 * hd + 1:3 * hd + 2] * o_s[h:h + 1]
                             + gates[:, 3 * hd + 2:3 * hd + 3] * o_w[h:h + 1])
    o_ref[...] = jnp.concatenate(out_parts, axis=1)


def _nsa_sample_attend(idx, page_table, q_rot, gates, o_c, sel_new, win_new, win_cache_t, sel_cache_t,
                       *, q_pos, n_past_blocks):
    b = q_rot.shape[0]
    wb = win_cache_t.shape[3]

    def page_spec(k):
        def index(i, idx_ref, pt_ref):
            j = jnp.minimum(idx_ref[i, k], n_past_blocks - 1)
            return (pt_ref[i, j // BLOCKS_PER_PAGE], 0, 0, 0)
        return pl.BlockSpec((None, 2, KV_W, PAGE_SIZE), index)

    one = lambda w: pl.BlockSpec((None, 1, w), lambda i, idx_ref, pt_ref: (i, 0, 0))
    return pl.pallas_call(
        functools.partial(_nsa_sample_attend_body, q_pos=q_pos, n_past_blocks=n_past_blocks),
        grid_spec=pltpu.PrefetchScalarGridSpec(
            num_scalar_prefetch=2,
            grid=(b,),
            in_specs=[one(Q_W), one(GATE_PAD), one(Q_W), one(ROW_W), one(ROW_W),
                      pl.BlockSpec((None, 2, KV_W, wb), lambda i, idx_ref, pt_ref: (i, 0, 0, 0))]
            + [page_spec(k) for k in range(N_KV * N_SEL)],
            out_specs=one(Q_W),
        ),
        out_shape=jax.ShapeDtypeStruct((b, 1, Q_W), F32),
        compiler_params=_params("arbitrary"),
        name="nsa_sample_attend",
    )(idx, page_table, q_rot, gates, o_c, sel_new, win_new, win_cache_t, *([sel_cache_t] * (N_KV * N_SEL)))


def _rope_tables(pos):
    half = HEAD_DIM // 2
    freqs = ROPE_THETA ** (-jnp.arange(half, dtype=F32) / half)
    ang = pos.astype(F32)[:, None] * freqs[None, :]
    cos, sin = jnp.cos(ang), jnp.sin(ang)
    reps = LANES // HEAD_DIM
    return jnp.tile(jnp.concatenate([cos, cos], axis=1), (1, reps)), jnp.tile(jnp.concatenate([-sin, sin], axis=1), (1, reps))


def _cover(nc, nsel, rows, cols):
    start = np.arange(rows)[:, None] * CMP_STRIDE
    j = np.arange(cols)[None, :]
    hit = (start < (j + 1) * SEL_BLOCK) & (start + CMP_BLOCK > j * SEL_BLOCK) & (np.arange(rows)[:, None] < nc) & (j < nsel)
    return jnp.asarray(hit.astype(np.float32))


def _cmp_chunk_weight(w1_k, w1_v):
    per = jnp.stack([w.reshape(CMP_RATIO, CMP_STRIDE, HEAD_DIM, CMP_HID) for w in (w1_k, w1_v)])
    eye = jnp.eye(N_KV, dtype=F32)
    big = jnp.einsum("ab,krsdh->ksadbrh", eye, per)
    return big.reshape(2, CMP_STRIDE, KV_W, N_KV * CMP_RATIO * CMP_HID).astype(BF16)


def _rg_gate_weight(w_a, w_x):
    nb, bw, _ = w_a.shape
    per = MXU_DIM // bw
    n_grp = nb // per
    eye = jnp.eye(per, dtype=F32)

    def group(w):
        w = w.reshape(n_grp, per, bw, bw)
        return jnp.einsum("pq,gpde->gpdqe", eye, w).reshape(n_grp, per * bw, per * bw)

    return jnp.concatenate([group(w_a), group(w_x)], axis=2).astype(BF16)


def _feature_major(kv_rows):
    lead = kv_rows.shape[:-4]
    rows = kv_rows.shape[-4]
    nd = len(lead)
    perm = tuple(range(nd)) + (nd + 1, nd + 2, nd + 3, nd)
    return jnp.transpose(kv_rows, perm).reshape(lead + (2, KV_W, rows))


def _row_major_view(kv_t, b, rows):
    return jnp.transpose(kv_t.reshape(b, 2, N_KV, HEAD_DIM, rows), (0, 4, 1, 2, 3))


def kernel(x_prompt, x_sample, cache_cmp_kv, cache_sel_kv, cache_win_kv, state_conv, state_h, page_table, ffn1_w_gate, ffn1_w_up, ffn1_w_down, ln1_g, ln1_b, w_in, cmp_pe_k, cmp_w1_k, cmp_w2_k, cmp_pe_v, cmp_w1_v, cmp_w2_v, conv_w, conv_b, rg_w_a, rg_b_a, rg_w_x, rg_b_x, rg_lam, w_br_attn, w_br_rnn, w_out, ln2_g, ln2_b, ffn2_w_gate, ffn2_w_up, ffn2_w_down, ln3_g, ln3_b):
    bp, tp, d = x_prompt.shape
    bs, ts, _ = x_sample.shape
    depth = w_in.shape[0]
    d_rnn = conv_w.shape[2]
    n_pages = page_table.shape[1]
    past_len = n_pages * PAGE_SIZE
    assert ts == 1 and tp % CMP_STRIDE == 0 and tp % SEL_BLOCK == 0
    alpha = (2.0 * depth) ** 0.25
    wb = cache_win_kv.shape[2]
    wbp = min(WINDOW, tp)

    nch_p = tp // CMP_STRIDE
    nc_p = nch_p - CMP_RATIO + 1
    nsel_p = tp // SEL_BLOCK
    nch_s = past_len // CMP_STRIDE
    nc_s = nch_s - CMP_RATIO + 1
    nsel_s = -(-(past_len + ts) // SEL_BLOCK)
    n_past_blocks = past_len // SEL_BLOCK
    assert nsel_s >= N_SEL and n_pages % PAGES_PER_STEP == 0
    nselp_s = -(-nsel_s // LANES) * LANES

    tm = min(256, tp)
    tq = min(256, tp)
    tt = min(256, tp)

    cos_p, sin_p = _rope_tables(jnp.arange(tp))
    cos_s, sin_s = _rope_tables(jnp.full((bs,), past_len))
    cover_p = _cover(nc_p, nsel_p, nch_p, nsel_p).T
    cover_s = _cover(nc_s, nsel_s, nch_s, nselp_s)
    expand_p = jnp.asarray((np.arange(tp)[:, None] // SEL_BLOCK == np.arange(nsel_p)[None, :]).astype(np.float32)).astype(BF16)

    xp = x_prompt.reshape(bp * tp, d)
    xs = x_sample.reshape(bs * ts, d)
    outs = [[] for _ in range(10)]
    for l in range(depth):
        row = lambda v: v[l].reshape(1, -1)
        ffn1 = (ffn1_w_gate[l].astype(BF16), ffn1_w_up[l].astype(BF16), ffn1_w_down[l].astype(BF16), row(ln1_g), row(ln1_b))
        ffn2 = (ffn2_w_gate[l].astype(BF16), ffn2_w_up[l].astype(BF16), ffn2_w_down[l].astype(BF16), row(ln3_g), row(ln3_b))
        n_a = Q_W + 6 * KV_W + 3 * N_HEADS
        w_a = jnp.pad(w_in[l][:, :n_a], ((0, 0), (0, GATE_PAD - 3 * N_HEADS))).astype(BF16)
        w_b = w_in[l][:, n_a:].astype(BF16)
        w_cmp = _cmp_chunk_weight(cmp_w1_k[l], cmp_w1_v[l])
        cmpw = (cmp_pe_k[l].reshape(1, -1), cmp_w1_k[l], cmp_w2_k[l], cmp_pe_v[l].reshape(1, -1), cmp_w1_v[l], cmp_w2_v[l])
        mixw = (jnp.pad(conv_w[l], ((0, SUBLANES - CONV_W), (0, 0))), row(conv_b), _rg_gate_weight(rg_w_a[l], rg_w_x[l]),
                row(rg_b_a), row(rg_b_x), row(rg_lam), w_br_attn[l].astype(BF16), w_br_rnn[l].astype(BF16),
                w_out[l].astype(BF16), row(ln2_g), row(ln2_b))

        h1 = _ffn_half_step(xp, *ffn1, alpha=alpha, tm=tm)
        (cmp, sel_k, win_k, xr, gr, ga, gb, q_raw_t, q_rot_t, gates_t, cmp_t, sel_t, win_t) = _in_proj(
            h1, cos_p, sin_p, w_a, w_b, tm=tm, seq_len=tp, feature_major=True)
        b3 = lambda a: a.reshape(bp, tp, a.shape[-1])
        p1 = _cmp_stage1(b3(cmp), w_cmp)
        ckv, ckv_t = _cmp_finish(p1, *cmpw, nc=nc_p)
        o_attn = _nsa_prompt(q_raw_t, q_rot_t, gates_t, ckv, ckv_t, b3(sel_k), sel_t, b3(win_k), win_t,
                             cover_p, expand_p, tq=tq, nc=nc_p)
        h2, h_last = _mix_prompt(b3(xr), b3(gr), b3(ga), b3(gb), o_attn, b3(h1), mixw, tt=tt, alpha=alpha)
        xp = _ffn_half_step(h2.reshape(bp * tp, d), *ffn2, alpha=alpha, tm=tm)
        outs[0].append(_row_major_view(cmp_t, bp, tp))
        outs[2].append(_row_major_view(sel_t, bp, tp))
        outs[4].append(_row_major_view(win_t[:, :, tp - wbp:], bp, wbp))
        outs[6].append(b3(xr)[:, tp - (CONV_W - 1):])
        outs[8].append(h_last.reshape(bp, d_rnn))

        h1s = _ffn_half_step(xs, *ffn1, alpha=alpha, tm=bs)
        q_raw, q_rot, cmp, sel, win, gates, xr, gr, ga, gb = _in_proj(
            h1s, cos_s, sin_s, w_a, w_b, tm=bs, seq_len=bs, feature_major=False)
        p1 = _cmp_stage1_paged(_feature_major(cache_cmp_kv[l]), page_table, w_cmp)
        ckv, _ = _cmp_finish(p1, *cmpw, nc=nc_s)
        b1 = lambda a: a.reshape(bs, 1, a.shape[-1])
        q_pos = past_len
        o_c, score = _nsa_sample_scores(b1(q_raw), ckv, cover_s, nc=nc_s, nsel=nsel_s, q_pos=q_pos)
        idx = _top_blocks(score.reshape(bs * SUBLANES, nselp_s))
        idx = idx.reshape(bs, SUBLANES, LANES)[:, :N_KV, :N_SEL].reshape(bs, N_KV * N_SEL)
        o_attn = _nsa_sample_attend(idx, page_table, b1(q_rot), b1(gates), o_c, b1(sel), b1(win),
                                    _feature_major(cache_win_kv[l]), _feature_major(cache_sel_kv[l]),
                                    q_pos=q_pos, n_past_blocks=n_past_blocks)
        conv_rows = [state_conv[l][:, k] for k in range(CONV_W - 1)]
        h2s, h_new = _mix_sample(xr, conv_rows, state_h[l], gr, ga, gb, o_attn.reshape(bs, Q_W), h1s, mixw, alpha=alpha)
        xs = _ffn_half_step(h2s, *ffn2, alpha=alpha, tm=bs)
        outs[1].append(cmp.reshape(bs, ts, 2, N_KV, HEAD_DIM))
        outs[3].append(sel.reshape(bs, ts, 2, N_KV, HEAD_DIM))
        win_all = jnp.concatenate([cache_win_kv[l], win.reshape(bs, ts, 2, N_KV, HEAD_DIM)], axis=1)
        outs[5].append(win_all[:, ts:])
        outs[7].append(jnp.concatenate([state_conv[l], xr[:, None, :]], axis=1)[:, ts:])
        outs[9].append(h_new)

    stacked = [jnp.stack(o) for o in outs]
    cmp_p, cmp_s, sel_p, sel_s, win_p, win_s, conv_p, conv_s, h_p, h_s = stacked
    return (xp.reshape(bp, tp, d), xs.reshape(bs, ts, d), cmp_p, cmp_s, sel_p, sel_s, win_p, win_s,
            conv_p, conv_s, h_p, h_s)
```

```python
import functools
import math

import numpy as np
import jax
import jax.numpy as jnp
from jax import lax
from jax.experimental import pallas as pl
from jax.experimental.pallas import tpu as pltpu

F32 = jnp.float32
BF16 = jnp.bfloat16

N_HEADS = 8
HEAD_DIM = 64
N_KV = 2
HPG = N_HEADS // N_KV
Q_W = N_HEADS * HEAD_DIM
KV_W = N_KV * HEAD_DIM
CMP_BLOCK = 32
CMP_STRIDE = 16
CMP_RATIO = CMP_BLOCK // CMP_STRIDE
CMP_HID = 64
SEL_BLOCK = 64
N_SEL = 16
SEL_BONUS = 1.0e4
WINDOW = 512
ROPE_THETA = 10000.0
RG_BLOCKS = 16
RG_C = 8.0
CONV_W = 4
PAGE_SIZE = 128
LN_EPS = 1e-5
SCALE = HEAD_DIM ** -0.5
QK_SCALE_LOG2 = SCALE * math.log2(math.e)
NEG_INF = float("-inf")

LANES = 128
SUBLANES = 8
MXU_DIM = 256
VMEM_LIMIT_BYTES = 56 * 1024 * 1024

ROW_W = 2 * KV_W
P_W = 2 * N_KV * CMP_RATIO * CMP_HID
GATE_PAD = LANES
PAGES_PER_STEP = 16
BLOCKS_PER_PAGE = PAGE_SIZE // SEL_BLOCK


def _dot(a, b):
    return jnp.dot(a.astype(BF16), b.astype(BF16), preferred_element_type=F32)


def _dot_nt(a, b):
    return lax.dot_general(a.astype(BF16), b.astype(BF16), (((1,), (1,)), ((), ())),
                           preferred_element_type=F32)


def _dot_f32(a, b):
    return jnp.dot(a, b, preferred_element_type=F32, precision=lax.Precision.HIGHEST)


def _sigmoid(x):
    return 0.5 * jnp.tanh(0.5 * x) + 0.5


def _gelu(x):
    return 0.5 * x * (1.0 + jnp.tanh(math.sqrt(2.0 / math.pi) * (x + 0.044715 * (x * x * x))))


def _layer_norm(y, g, b):
    mu = jnp.mean(y, axis=-1, keepdims=True)
    yc = y - mu
    var = jnp.mean(yc * yc, axis=-1, keepdims=True)
    return yc * lax.rsqrt(var + LN_EPS) * g + b


def _softmax_parts(s, mask):
    s = jnp.where(mask, s, NEG_INF)
    m = jnp.max(s, axis=-1, keepdims=True)
    m = jnp.where(m > NEG_INF, m, 0.0)
    e = jnp.exp(s - m)
    d = jnp.sum(e, axis=-1, keepdims=True)
    return e, jnp.where(d > 0, d, 1.0)


def _const_spec(shape):
    nd = len(shape)
    return pl.BlockSpec(shape, lambda *_: (0,) * nd)


def _params(*sem):
    return pltpu.CompilerParams(dimension_semantics=sem, vmem_limit_bytes=VMEM_LIMIT_BYTES)


def _ffn_body(x_ref, wg_ref, wu_ref, wd_ref, g_ref, b_ref, o_ref, *, alpha):
    x = x_ref[...]
    xb = x.astype(BF16)
    gate = jnp.dot(xb, wg_ref[...], preferred_element_type=F32)
    up = jnp.dot(xb, wu_ref[...], preferred_element_type=F32)
    hmid = gate * _sigmoid(gate) * up
    f = _dot(hmid, wd_ref[...])
    o_ref[...] = _layer_norm(alpha * x + 0.5 * f, g_ref[...], b_ref[...])


def _ffn_half_step(x, wg, wu, wd, g, b, *, alpha, tm):
    n, d = x.shape
    ff = wg.shape[1]
    return pl.pallas_call(
        functools.partial(_ffn_body, alpha=alpha),
        grid=(n // tm,),
        in_specs=[pl.BlockSpec((tm, d), lambda i: (i, 0)),
                  _const_spec((d, ff)), _const_spec((d, ff)), _const_spec((ff, d)),
                  _const_spec((1, d)), _const_spec((1, d))],
        out_specs=pl.BlockSpec((tm, d), lambda i: (i, 0)),
        out_shape=jax.ShapeDtypeStruct((n, d), F32),
        compiler_params=_params("parallel"),
        name="ffn_half_step",
    )(x, wg, wu, wd, g, b)


def _rope(x, cos, sin):
    w = x.shape[1]
    reps = w // LANES
    c = jnp.tile(cos, (1, reps)) if reps > 1 else cos
    s = jnp.tile(sin, (1, reps)) if reps > 1 else sin
    lane = lax.broadcasted_iota(jnp.int32, x.shape, 1)
    first_half = (lane % HEAD_DIM) < (HEAD_DIM // 2)
    swapped = jnp.where(first_half, pltpu.roll(x, w - HEAD_DIM // 2, 1), pltpu.roll(x, HEAD_DIM // 2, 1))
    return x * c + swapped * s


def _inproj_body(h_ref, cos_ref, sin_ref, wa_ref, wb_ref, *out_refs, feature_major):
    hb = h_ref[...].astype(BF16)
    cos = cos_ref[...]
    sin = sin_ref[...]
    za = jnp.dot(hb, wa_ref[...], preferred_element_type=F32)
    q = za[:, :Q_W]
    kv = za[:, Q_W:Q_W + 6 * KV_W]
    gates = _sigmoid(za[:, Q_W + 6 * KV_W:])
    q_rot = _rope(q, cos, sin)
    cmp = kv[:, :2 * KV_W]
    sel_k = _rope(kv[:, 2 * KV_W:3 * KV_W], cos, sin)
    win_k = _rope(kv[:, 4 * KV_W:5 * KV_W], cos, sin)
    sel = jnp.concatenate([sel_k, kv[:, 3 * KV_W:4 * KV_W]], axis=1)
    win = jnp.concatenate([win_k, kv[:, 5 * KV_W:6 * KV_W]], axis=1)
    if feature_major:
        (cmp_ref, selk_ref, wink_ref, xr_ref, gr_ref, ga_ref, gb_ref,
         qrawt_ref, qrott_ref, gatest_ref, cmpt_ref, selt_ref, wint_ref) = out_refs
        cmp_ref[...] = cmp
        selk_ref[...] = sel_k.astype(BF16)
        wink_ref[...] = win_k.astype(BF16)
        qrawt_ref[...] = q.T
        qrott_ref[...] = q_rot.T
        gatest_ref[...] = gates.T
        cmpt_ref[...] = cmp.T
        selt_ref[...] = sel.T
        wint_ref[...] = win.T
    else:
        qraw_ref, qrot_ref, cmp_ref, sel_ref, win_ref, gates_ref, xr_ref, gr_ref, ga_ref, gb_ref = out_refs
        qraw_ref[...] = q
        qrot_ref[...] = q_rot
        cmp_ref[...] = cmp
        sel_ref[...] = sel
        win_ref[...] = win
        gates_ref[...] = gates
    zb = jnp.dot(hb, wb_ref[...], preferred_element_type=F32)
    d = xr_ref.shape[1]
    xr_ref[...] = zb[:, :d]
    gr_ref[...] = zb[:, d:2 * d]
    ga_ref[...] = zb[:, 2 * d:3 * d]
    gb_ref[...] = zb[:, 3 * d:]


def _in_proj(h, cos, sin, wa, wb, *, tm, seq_len, feature_major):
    n, d = h.shape
    wa_w, wb_w = wa.shape[1], wb.shape[1]
    tiles = seq_len // tm
    row = lambda i: (i, 0)
    tab = lambda i: (i % tiles, 0)
    if feature_major:
        rows = [(ROW_W, F32), (KV_W, BF16), (KV_W, BF16), (d, F32), (d, F32), (d, F32), (d, F32)]
        cols = [Q_W, Q_W, GATE_PAD, ROW_W, ROW_W, ROW_W]
    else:
        rows = [(w, F32) for w in (Q_W, Q_W, ROW_W, ROW_W, ROW_W, GATE_PAD, d, d, d, d)]
        cols = []
    out_specs = [pl.BlockSpec((tm, w), row) for w, _ in rows]
    out_shape = [jax.ShapeDtypeStruct((n, w), dt) for w, dt in rows]
    out_specs += [pl.BlockSpec((None, w, tm), lambda i: (i // tiles, 0, i % tiles)) for w in cols]
    out_shape += [jax.ShapeDtypeStruct((n // seq_len, w, seq_len), F32) for w in cols]
    return pl.pallas_call(
        functools.partial(_inproj_body, feature_major=feature_major),
        grid=(n // tm,),
        in_specs=[pl.BlockSpec((tm, d), row), pl.BlockSpec((tm, LANES), tab), pl.BlockSpec((tm, LANES), tab),
                  _const_spec((d, wa_w)), _const_spec((d, wb_w))],
        out_specs=out_specs,
        out_shape=out_shape,
        compiler_params=_params("parallel"),
        name="in_proj",
    )(h, cos, sin, wa, wb)


def _cmp_chunks(rows_of, w_ref, kv, nch):
    acc = None
    for s in range(CMP_STRIDE):
        part = _dot(rows_of(s, nch), w_ref[kv, s])
        acc = part if acc is None else acc + part
    return acc


def _cmp_stage1_body(x_ref, w_ref, o_ref, rows_ref):
    nch = x_ref.shape[0] // CMP_STRIDE
    half = P_W // 2
    for kv in range(2):
        rows_ref[kv] = x_ref[:, kv * KV_W:(kv + 1) * KV_W]
        rows_of = lambda s, n, kv=kv: rows_ref[kv, pl.ds(s, n, stride=CMP_STRIDE), :]
        o_ref[:, kv * half:(kv + 1) * half] = _cmp_chunks(rows_of, w_ref, kv, nch)


def _cmp_stage1(x, w):
    b, t, _ = x.shape
    nch = t // CMP_STRIDE
    return pl.pallas_call(
        _cmp_stage1_body,
        grid=(b,),
        in_specs=[pl.BlockSpec((None, t, ROW_W), lambda i: (i, 0, 0)), _const_spec(w.shape)],
        out_specs=pl.BlockSpec((None, nch, P_W), lambda i: (i, 0, 0)),
        out_shape=jax.ShapeDtypeStruct((b, nch, P_W), F32),
        scratch_shapes=[pltpu.VMEM((2, t, KV_W), F32)],
        compiler_params=_params("parallel"),
        name="cmp_stage1",
    )(x, w)


def _cmp_paged_body(pt_ref, *refs):
    del pt_ref
    x_refs = refs[:PAGES_PER_STEP]
    perm_ref, w_ref, o_ref, planes_ref, stage_ref = refs[PAGES_PER_STEP:]
    cpp = PAGE_SIZE // CMP_STRIDE
    perm = perm_ref[...]
    for k, x_ref in enumerate(x_refs):
        for kv in range(2):
            stage_ref[k, kv] = jnp.dot(x_ref[kv].astype(BF16), perm, preferred_element_type=F32)
            rows = stage_ref[k, kv].T
            for s in range(CMP_STRIDE):
                planes_ref[kv, s, k * cpp:(k + 1) * cpp, :] = rows[s * cpp:(s + 1) * cpp]
    half = P_W // 2
    for kv in range(2):
        acc = None
        for j in range(CMP_STRIDE // 2):
            pair = jnp.concatenate([planes_ref[kv, 2 * j], planes_ref[kv, 2 * j + 1]], axis=1)
            part = _dot(pair, w_ref[kv, j])
            acc = part if acc is None else acc + part
        o_ref[:, kv * half:(kv + 1) * half] = acc


def _cmp_stage1_paged(cache_t, page_table, w):
    b, n_pages = page_table.shape
    steps = n_pages // PAGES_PER_STEP
    cpp = PAGE_SIZE // CMP_STRIDE
    nch = n_pages * cpp
    w = w.reshape(2, CMP_STRIDE // 2, 2 * KV_W, w.shape[-1])
    row = np.arange(PAGE_SIZE)
    perm = np.zeros((PAGE_SIZE, PAGE_SIZE), np.float32)
    perm[row, (row % CMP_STRIDE) * cpp + row // CMP_STRIDE] = 1.0
    perm = jnp.asarray(perm).astype(BF16)

    def page_spec(k):
        return pl.BlockSpec((None, 2, KV_W, PAGE_SIZE), lambda i, s, pt: (pt[i, s * PAGES_PER_STEP + k], 0, 0, 0))

    return pl.pallas_call(
        _cmp_paged_body,
        grid_spec=pltpu.PrefetchScalarGridSpec(
            num_scalar_prefetch=1,
            grid=(b, steps),
            in_specs=[page_spec(k) for k in range(PAGES_PER_STEP)]
            + [pl.BlockSpec(perm.shape, lambda i, s, pt: (0, 0)),
               pl.BlockSpec(w.shape, lambda i, s, pt: (0,) * w.ndim)],
            out_specs=pl.BlockSpec((None, PAGES_PER_STEP * cpp, P_W), lambda i, s, pt: (i, s, 0)),
            scratch_shapes=[pltpu.VMEM((2, CMP_STRIDE, PAGES_PER_STEP * cpp, KV_W), F32),
                            pltpu.VMEM((PAGES_PER_STEP, 2, KV_W, PAGE_SIZE), F32)],
        ),
        out_shape=jax.ShapeDtypeStruct((b, nch, P_W), F32),
        compiler_params=_params("parallel", "parallel"),
        name="cmp_stage1_paged",
    )(page_table, *([cache_t] * PAGES_PER_STEP), perm, w)


def _cmp_finish_body(p_ref, pek_ref, w1k_ref, w2k_ref, pev_ref, w1v_ref, w2v_ref, o_ref, ot_ref, *, nc):
    p = p_ref[...]
    nch = p.shape[0]
    row = lax.broadcasted_iota(jnp.int32, (nch, CMP_HID), 0)
    outs = []
    for kv, (pe_ref, w1_ref, w2_ref) in enumerate(((pek_ref, w1k_ref, w2k_ref), (pev_ref, w1v_ref, w2v_ref))):
        pe = jnp.broadcast_to(pe_ref[...], (SUBLANES, pe_ref.shape[1]))
        bias = _dot(pe, w1_ref[...])[0:1]
        for g in range(N_KV):
            c0 = (kv * N_KV + g) * CMP_RATIO * CMP_HID
            u = bias + p[:, c0:c0 + CMP_HID]
            for r in range(1, CMP_RATIO):
                u = u + pltpu.roll(p[:, c0 + r * CMP_HID:c0 + (r + 1) * CMP_HID], nch - r, 0)
            out = _dot(_gelu(u), w2_ref[...])
            outs.append(jnp.where(row < nc, out, 0.0))
    ckv = jnp.concatenate(outs, axis=1)
    o_ref[...] = ckv
    ot_ref[...] = ckv.T


def _cmp_finish(p, pe_k, w1_k, w2_k, pe_v, w1_v, w2_v, *, nc):
    b, nch, _ = p.shape
    flat = CMP_BLOCK * HEAD_DIM
    return pl.pallas_call(
        functools.partial(_cmp_finish_body, nc=nc),
        grid=(b,),
        in_specs=[pl.BlockSpec((None, nch, P_W), lambda i: (i, 0, 0)),
                  _const_spec((1, flat)), _const_spec((flat, CMP_HID)), _const_spec((CMP_HID, HEAD_DIM)),
                  _const_spec((1, flat)), _const_spec((flat, CMP_HID)), _const_spec((CMP_HID, HEAD_DIM))],
        out_specs=[pl.BlockSpec((None, nch, ROW_W), lambda i: (i, 0, 0)),
                   pl.BlockSpec((None, ROW_W, nch), lambda i: (i, 0, 0))],
        out_shape=[jax.ShapeDtypeStruct((b, nch, ROW_W), F32), jax.ShapeDtypeStruct((b, ROW_W, nch), F32)],
        compiler_params=_params("parallel"),
        name="cmp_finish",
    )(p, pe_k, w1_k, w2_k, pe_v, w1_v, w2_v)


def _block_scores(imp, q_pos, nsel_valid, block_axis=1):
    jblk = lax.broadcasted_iota(jnp.int32, imp.shape, block_axis)
    cur = q_pos // SEL_BLOCK
    valid = (jblk * SEL_BLOCK <= q_pos) & (jblk < nsel_valid)
    forced = (jblk == 0) | (jblk == cur) | (jblk == cur - 1)
    return jnp.where(valid, imp + jnp.where(forced, SEL_BONUS, 0.0), NEG_INF)


def _nsa_prompt_body(qrawt_ref, qrott_ref, gatest_ref, ckv_ref, ckvt_ref, selk_ref, selt_ref, wink_ref, wint_ref,
                     covert_ref, expandt_ref, o_ref, *, tq, nc):
    nchp = ckv_ref.shape[0]
    nsel = covert_ref.shape[0]
    tile = pl.program_id(1)
    s0 = tile * tq
    lane_t = lax.broadcasted_iota(jnp.int32, (1, tq), 1)
    key_sub = lax.broadcasted_iota(jnp.int32, (tq, 1), 0)
    qpos = s0 + lane_t
    causal_diag = key_sub <= lane_t
    window_edge = lane_t <= key_sub
    cidx = lax.broadcasted_iota(jnp.int32, (nchp, 1), 0)
    jblk = lax.broadcasted_iota(jnp.int32, (nsel, 1), 0)
    n_win_chunks = WINDOW // tq
    vrows = [slice(KV_W + g * HEAD_DIM, KV_W + (g + 1) * HEAD_DIM) for g in range(N_KV)]
    group_of = lambda hd: hd // HPG

    def group_rows(qt, g):
        z = jnp.zeros_like(qt)
        return jnp.concatenate([qt, z] if g == 0 else [z, qt], axis=0)

    def head_rows(ref, hd):
        return group_rows((ref[hd * HEAD_DIM:(hd + 1) * HEAD_DIM, :] * QK_SCALE_LOG2).astype(BF16), group_of(hd))

    q_rot = [head_rows(qrott_ref, hd) for hd in range(N_HEADS)]
    q_raw = jnp.concatenate([head_rows(qrawt_ref, hd) for hd in range(N_HEADS)], axis=1)
    s_c = jnp.dot(ckv_ref[:, :KV_W].astype(BF16), q_raw, preferred_element_type=F32)
    m_c = (cidx * CMP_STRIDE + (CMP_BLOCK - 1) <= jnp.concatenate([qpos] * N_HEADS, axis=1)) & (cidx < nc)
    s_c = jnp.where(m_c, s_c, NEG_INF)
    mx = jnp.max(s_c, axis=0, keepdims=True)
    e_c = jnp.exp2(s_c - jnp.where(mx > NEG_INF, mx, 0.0))
    d_c = jnp.sum(e_c, axis=0, keepdims=True)
    p_c = e_c * (1.0 / jnp.where(d_c > 0, d_c, 1.0))
    head_cols = lambda hd: slice(hd * tq, (hd + 1) * tq)
    o_c = [_dot(ckvt_ref[vrows[group_of(hd)], :], p_c[:, head_cols(hd)]) for hd in range(N_HEADS)]
    p_sum = []
    for g in range(N_KV):
        acc = p_c[:, head_cols(g * HPG)]
        for hd in range(g * HPG + 1, (g + 1) * HPG):
            acc = acc + p_c[:, head_cols(hd)]
        p_sum.append(acc)
    imp = _dot_f32(covert_ref[...], jnp.concatenate(p_sum, axis=1))
    score = _block_scores(imp, jnp.concatenate([qpos] * N_KV, axis=1), nsel, block_axis=0)
    rank = jnp.zeros(score.shape, jnp.int32)
    for i in range(nsel):
        s_i = score[i:i + 1, :]
        beats = (s_i > score) | ((s_i == score) & (i < jblk))
        rank = rank + beats.astype(jnp.int32)
    chosen = (rank < N_SEL).astype(BF16)

    def picked(k0):
        hit = jnp.dot(expandt_ref[pl.ds(k0, tq), :], chosen, preferred_element_type=F32)
        return [hit[:, g * tq:(g + 1) * tq] > 0.5 for g in range(N_KV)]

    def attend(jobs):
        loaded = []
        for k_ref, vt_ref, kc, masks_of, _ in jobs:
            k0 = pl.multiple_of(kc * tq, tq)
            keys = k_ref[pl.ds(k0, tq), :]
            values_t = [vt_ref[rows, pl.ds(k0, tq)].astype(BF16) for rows in vrows]
            loaded.append((keys, values_t, masks_of(k0)))
        scores = [[jnp.dot(keys, q_rot[hd], preferred_element_type=F32) for hd in range(N_HEADS)]
                  for keys, _, _ in loaded]
        updates = []
        for job, (_, _, masks), sc in zip(jobs, loaded, scores):
            carry = job[4]
            stats, probs = [], []
            for hd in range(N_HEADS):
                m, l, _ = carry[hd]
                s = sc[hd] if masks is None else jnp.where(masks[group_of(hd)], sc[hd], NEG_INF)
                m_new = jnp.maximum(m, jnp.max(s, axis=0, keepdims=True))
                m_safe = m_new if masks is None else jnp.where(m_new > NEG_INF, m_new, 0.0)
                alpha = jnp.exp2(m - m_safe)
                p = jnp.exp2(s - m_safe)
                stats.append((m_new, l * alpha + jnp.sum(p, axis=0, keepdims=True), alpha))
                probs.append(p.astype(BF16))
            updates.append((stats, probs))
        out = []
        for job, (_, values_t, _), (stats, probs) in zip(jobs, loaded, updates):
            carry = job[4]
            pv = [jnp.dot(values_t[group_of(hd)], probs[hd], preferred_element_type=F32) for hd in range(N_HEADS)]
            out.append(tuple((stats[hd][0], stats[hd][1], carry[hd][2] * stats[hd][2] + pv[hd])
                             for hd in range(N_HEADS)))
        return out

    def finish(carry):
        return [acc * (1.0 / jnp.where(l > 0, l, 1.0)) for _, l, acc in carry]

    init = tuple((jnp.full((1, tq), NEG_INF, F32), jnp.zeros((1, tq), F32), jnp.zeros((HEAD_DIM, tq), F32))
                 for _ in range(N_HEADS))
    no_mask = lambda k0: None
    sel = lax.fori_loop(0, tile, lambda kc, c: attend([(selk_ref, selt_ref, kc, picked, c)])[0], init)
    edge = tile - n_win_chunks
    win = lax.fori_loop(jnp.maximum(edge, 0), jnp.maximum(edge + 1, 0),
                        lambda kc, c: attend([(wink_ref, wint_ref, kc, lambda k0: [window_edge] * N_KV, c)])[0], init)
    win = lax.fori_loop(jnp.maximum(edge + 1, 0), tile,
                        lambda kc, c: attend([(wink_ref, wint_ref, kc, no_mask, c)])[0], win)
    sel, win = attend([(selk_ref, selt_ref, tile, lambda k0: [hit & causal_diag for hit in picked(k0)], sel),
                       (wink_ref, wint_ref, tile, lambda k0: [causal_diag] * N_KV, win)])
    o_s, o_w = finish(sel), finish(win)
    out_heads = []
    for hd in range(N_HEADS):
        gate = lambda branch: gatest_ref[3 * hd + branch:3 * hd + branch + 1, :]
        out_heads.append(gate(0) * o_c[hd] + gate(1) * o_s[hd] + gate(2) * o_w[hd])
    o_ref[...] = jnp.concatenate(out_heads, axis=0).T


def _nsa_prompt(q_raw_t, q_rot_t, gates_t, ckv, ckv_t, sel_k, sel_t, win_k, win_t, cover_t, expand_t, *, tq, nc):
    b, _, t = q_raw_t.shape
    nchp = ckv.shape[1]
    nsel = cover_t.shape[0]
    assert WINDOW % tq == 0
    tile_t = lambda w: pl.BlockSpec((None, w, tq), lambda i, j: (i, 0, j))
    whole = lambda r, w: pl.BlockSpec((None, r, w), lambda i, j: (i, 0, 0))
    return pl.pallas_call(
        functools.partial(_nsa_prompt_body, tq=tq, nc=nc),
        grid=(b, t // tq),
        in_specs=[tile_t(Q_W), tile_t(Q_W), tile_t(GATE_PAD), whole(nchp, ROW_W), whole(ROW_W, nchp),
                  whole(t, KV_W), whole(ROW_W, t), whole(t, KV_W), whole(ROW_W, t),
                  _const_spec((nsel, nchp)), _const_spec((t, nsel))],
        out_specs=pl.BlockSpec((None, tq, Q_W), lambda i, j: (i, j, 0)),
        out_shape=jax.ShapeDtypeStruct((b, t, Q_W), F32),
        compiler_params=_params("parallel", "arbitrary"),
        name="nsa_prompt",
    )(q_raw_t, q_rot_t, gates_t, ckv, ckv_t, sel_k, sel_t, win_k, win_t, cover_t, expand_t)


def _softplus(x):
    return jnp.maximum(x, 0.0) + jnp.log1p(jnp.exp(-jnp.abs(x)))


def _rg_gates(xc, wgate_ref, ba, bx, lam):
    xcb = xc.astype(BF16)
    n_grp = wgate_ref.shape[0]
    gw = wgate_ref.shape[1]
    za, zx = [], []
    for k in range(n_grp):
        z = jnp.dot(xcb[:, k * gw:(k + 1) * gw], wgate_ref[k], preferred_element_type=F32)
        za.append(z[:, :gw])
        zx.append(z[:, gw:])
    r = _sigmoid(jnp.concatenate(za, axis=1) + ba)
    i = _sigmoid(jnp.concatenate(zx, axis=1) + bx)
    log_a = -RG_C * r * _softplus(-lam)
    a = jnp.exp(log_a)
    u = jnp.sqrt(-jnp.tanh(log_a) * (1.0 + a * a)) * i * xc
    return a, u


def _merge(h_res, hs, gr, ga, gb, o_attn, wbra_ref, wbrr_ref, wout_ref, g2, b2, alpha):
    y_rnn = hs * _gelu(gr)
    m = _sigmoid(ga) * _dot(o_attn, wbra_ref[...]) + _sigmoid(gb) * _dot(y_rnn, wbrr_ref[...])
    return _layer_norm(alpha * h_res + _dot(m, wout_ref[...]), g2, b2)


def _mix_prompt_body(xr_ref, gr_ref, ga_ref, gb_ref, oat_ref, h1_ref, convw_ref, convb_ref, wgate_ref,
                     ba_ref, bx_ref, lam_ref, wbra_ref, wbrr_ref, wout_ref, g2_ref, b2_ref,
                     o_ref, hlast_ref, hc_ref, tail_ref, *, tt, alpha):
    @pl.when(pl.program_id(1) == 0)
    def _():
        hc_ref[...] = jnp.zeros_like(hc_ref)
        tail_ref[...] = jnp.zeros_like(tail_ref)

    x = xr_ref[...]
    convw = convw_ref[...]
    sub = lax.broadcasted_iota(jnp.int32, (SUBLANES, x.shape[1]), 0)
    shifts = range(1, CONV_W)
    prev_rot = [pltpu.roll(tail_ref[...], d, 0) for d in shifts]
    xc_groups = []
    for k in range(tt // SUBLANES):
        x_grp = x[k * SUBLANES:(k + 1) * SUBLANES]
        rot = [pltpu.roll(x_grp, d, 0) for d in shifts]
        xc_grp = convb_ref[...] + convw[CONV_W - 1:CONV_W] * x_grp
        for d, r_prev, r_cur in zip(shifts, prev_rot, rot):
            xc_grp = xc_grp + convw[CONV_W - 1 - d:CONV_W - d] * jnp.where(sub < d, r_prev, r_cur)
        xc_groups.append(xc_grp)
        prev_rot = rot
    xc = jnp.concatenate(xc_groups, axis=0)
    tail_ref[...] = x[tt - SUBLANES:tt]

    a, u = _rg_gates(xc, wgate_ref, ba_ref[...], bx_ref[...], lam_ref[...])
    sub = lax.broadcasted_iota(jnp.int32, (SUBLANES, a.shape[1]), 0)
    h_prev = jnp.broadcast_to(hc_ref[...], sub.shape)
    groups = []
    for k in range(tt // SUBLANES):
        rows = slice(k * SUBLANES, (k + 1) * SUBLANES)
        a_grp, u_grp = a[rows], u[rows]
        d = 1
        while d < SUBLANES:
            inside = sub >= d
            u_grp = jnp.where(inside, a_grp * pltpu.roll(u_grp, d, 0) + u_grp, u_grp)
            a_grp = jnp.where(inside, a_grp * pltpu.roll(a_grp, d, 0), a_grp)
            d *= 2
        h_grp = a_grp * h_prev + u_grp
        groups.append(h_grp)
        h_prev = jnp.broadcast_to(h_grp[SUBLANES - 1:SUBLANES], h_grp.shape)
    hs = jnp.concatenate(groups, axis=0)
    hc_ref[...] = hs[tt - 1:tt]
    hlast_ref[...] = hs[tt - 1:tt]
    o_ref[...] = _merge(h1_ref[...], hs, gr_ref[...], ga_ref[...], gb_ref[...], oat_ref[...],
                        wbra_ref, wbrr_ref, wout_ref, g2_ref[...], b2_ref[...], alpha)


def _mix_weight_specs(d, n_grp, gw):
    return [_const_spec((SUBLANES, d)), _const_spec((1, d)), _const_spec((n_grp, gw, 2 * gw)),
            _const_spec((1, d)), _const_spec((1, d)), _const_spec((1, d)),
            _const_spec((Q_W, d)), _const_spec((d, d)), _const_spec((d, d)),
            _const_spec((1, d)), _const_spec((1, d))]


def _mix_prompt(xr, gr, ga, gb, o_attn, h1, mixw, *, tt, alpha):
    b, t, d = xr.shape
    n_grp, gw = mixw[2].shape[:2]
    tile = lambda w: pl.BlockSpec((None, tt, w), lambda i, j: (i, j, 0))
    return pl.pallas_call(
        functools.partial(_mix_prompt_body, tt=tt, alpha=alpha),
        grid=(b, t // tt),
        in_specs=[tile(d), tile(d), tile(d), tile(d), tile(Q_W), tile(d)] + _mix_weight_specs(d, n_grp, gw),
        out_specs=[tile(d), pl.BlockSpec((None, 1, d), lambda i, j: (i, 0, 0))],
        out_shape=[jax.ShapeDtypeStruct((b, t, d), F32), jax.ShapeDtypeStruct((b, 1, d), F32)],
        scratch_shapes=[pltpu.VMEM((1, d), F32), pltpu.VMEM((SUBLANES, d), F32)],
        compiler_params=_params("parallel", "arbitrary"),
        name="mix_prompt",
    )(xr, gr, ga, gb, o_attn, h1, *mixw)


def _mix_sample_body(xr_ref, c0_ref, c1_ref, c2_ref, h0_ref, gr_ref, ga_ref, gb_ref, oat_ref, h1_ref,
                     convw_ref, convb_ref, wgate_ref, ba_ref, bx_ref, lam_ref, wbra_ref, wbrr_ref, wout_ref,
                     g2_ref, b2_ref, o_ref, hnew_ref, *, alpha):
    convw = convw_ref[...]
    xc = (convb_ref[...] + convw[0:1] * c0_ref[...] + convw[1:2] * c1_ref[...] + convw[2:3] * c2_ref[...]
          + convw[3:4] * xr_ref[...])
    a, u = _rg_gates(xc, wgate_ref, ba_ref[...], bx_ref[...], lam_ref[...])
    hs = a * h0_ref[...] + u
    hnew_ref[...] = hs
    o_ref[...] = _merge(h1_ref[...], hs, gr_ref[...], ga_ref[...], gb_ref[...], oat_ref[...],
                        wbra_ref, wbrr_ref, wout_ref, g2_ref[...], b2_ref[...], alpha)


def _mix_sample(xr, conv_rows, h0, gr, ga, gb, o_attn, h1, mixw, *, alpha):
    b, d = xr.shape
    n_grp, gw = mixw[2].shape[:2]
    full = lambda w: _const_spec((b, w))
    return pl.pallas_call(
        functools.partial(_mix_sample_body, alpha=alpha),
        grid=(1,),
        in_specs=[full(d)] * 8 + [full(Q_W), full(d)] + _mix_weight_specs(d, n_grp, gw),
        out_specs=[full(d), full(d)],
        out_shape=[jax.ShapeDtypeStruct((b, d), F32), jax.ShapeDtypeStruct((b, d), F32)],
        compiler_params=_params("arbitrary"),
        name="mix_sample",
    )(xr, *conv_rows, h0, gr, ga, gb, o_attn, h1, *mixw)


def _head_rows(row, g):
    parts = [row[:, (g * HPG + h) * HEAD_DIM:(g * HPG + h + 1) * HEAD_DIM] for h in range(HPG)]
    return jnp.concatenate(parts + [jnp.zeros((SUBLANES - HPG, HEAD_DIM), F32)], axis=0)


def _nsa_sample_scores_body(qraw_ref, ckv_ref, cover_ref, oc_ref, score_ref, *, nc, nsel, q_pos):
    nchp = ckv_ref.shape[0]
    nselp = cover_ref.shape[1]
    cidx = lax.broadcasted_iota(jnp.int32, (1, nchp), 1)
    qrow = qraw_ref[...] * SCALE
    m_c = (cidx * CMP_STRIDE + (CMP_BLOCK - 1) <= q_pos) & (cidx < nc)
    oc_parts, score_rows = [], []
    for g in range(N_KV):
        q = _head_rows(qrow, g)
        s_c = _dot_nt(q, ckv_ref[:, g * HEAD_DIM:(g + 1) * HEAD_DIM])
        e_c, d_c = _softmax_parts(s_c, m_c)
        p_c = e_c / d_c
        o_c = _dot(p_c, ckv_ref[:, KV_W + g * HEAD_DIM:KV_W + (g + 1) * HEAD_DIM])
        oc_parts += [o_c[h:h + 1] for h in range(HPG)]
        p_sum = jnp.sum(p_c[0:HPG], axis=0, keepdims=True)
        imp = _dot_f32(jnp.broadcast_to(p_sum, (SUBLANES, nchp)), cover_ref[...])[0:1]
        score_rows.append(_block_scores(imp, q_pos, nsel))
    oc_ref[...] = jnp.concatenate(oc_parts, axis=1)
    score_ref[...] = jnp.concatenate(score_rows + [jnp.full((SUBLANES - N_KV, nselp), NEG_INF, F32)], axis=0)


def _nsa_sample_scores(q_raw, ckv, cover, *, nc, nsel, q_pos):
    b = q_raw.shape[0]
    nchp = ckv.shape[1]
    nselp = cover.shape[1]
    return pl.pallas_call(
        functools.partial(_nsa_sample_scores_body, nc=nc, nsel=nsel, q_pos=q_pos),
        grid=(b,),
        in_specs=[pl.BlockSpec((None, 1, Q_W), lambda i: (i, 0, 0)),
                  pl.BlockSpec((None, nchp, ROW_W), lambda i: (i, 0, 0)), _const_spec((nchp, nselp))],
        out_specs=[pl.BlockSpec((None, 1, Q_W), lambda i: (i, 0, 0)),
                   pl.BlockSpec((None, SUBLANES, nselp), lambda i: (i, 0, 0))],
        out_shape=[jax.ShapeDtypeStruct((b, 1, Q_W), F32), jax.ShapeDtypeStruct((b, SUBLANES, nselp), F32)],
        compiler_params=_params("parallel"),
        name="nsa_sample_scores",
    )(q_raw, ckv, cover)


def _top_blocks_body(score_ref, idx_ref):
    score = score_ref[...]
    n, nselp = score.shape
    lane = lax.broadcasted_iota(jnp.int32, (n, nselp), 1)
    lane_out = lax.broadcasted_iota(jnp.int32, (n, LANES), 1)
    picked = jnp.zeros((n, LANES), jnp.int32)
    for r in range(N_SEL):
        m = jnp.max(score, axis=1, keepdims=True)
        j = jnp.min(jnp.where(score == m, lane, nselp), axis=1, keepdims=True)
        picked = jnp.where(lane_out == r, j, picked)
        score = jnp.where(lane == j, NEG_INF, score)
    idx_ref[...] = picked


def _top_blocks(score):
    n, nselp = score.shape
    return pl.pallas_call(
        _top_blocks_body,
        grid=(1,),
        in_specs=[_const_spec((n, nselp))],
        out_specs=_const_spec((n, LANES)),
        out_shape=jax.ShapeDtypeStruct((n, LANES), jnp.int32),
        compiler_params=_params("arbitrary"),
        name="top_blocks",
    )(score)


def _nsa_sample_attend_body(idx_ref, pt_ref, qrot_ref, gates_ref, oc_ref, selnew_ref, winnew_ref, wincache_ref,
                            *refs, q_pos, n_past_blocks):
    del pt_ref
    page_refs, o_ref = refs[:-1], refs[-1]
    b = pl.program_id(0)
    wb = wincache_ref.shape[2]
    qrow = qrot_ref[...] * SCALE
    gates = gates_ref[...]
    oc = oc_ref[...]
    sel_new = selnew_ref[...]
    win_new = winnew_ref[...]
    page_lane = lax.broadcasted_iota(jnp.int32, (1, PAGE_SIZE), 1)
    wpos = (q_pos - wb) + lax.broadcasted_iota(jnp.int32, (1, wb), 1)
    m_w = (q_pos - wpos <= WINDOW) & (wpos >= 0) & (wpos <= q_pos)
    cur = q_pos // SEL_BLOCK
    out_parts = []
    for g in range(N_KV):
        q = _head_rows(qrow, g)
        feat = slice(g * HEAD_DIM, (g + 1) * HEAD_DIM)
        kcol = slice(g * HEAD_DIM, (g + 1) * HEAD_DIM)
        vcol = slice(KV_W + g * HEAD_DIM, KV_W + (g + 1) * HEAD_DIM)
        kts, vts, kps = [], [], []
        new_chosen = None
        for n in range(N_SEL):
            j = idx_ref[b, g * N_SEL + n]
            page = page_refs[g * N_SEL + n]
            kts.append(page[0, feat, :])
            vts.append(page[1, feat, :])
            in_block = (page_lane // SEL_BLOCK == j % BLOCKS_PER_PAGE) & (j < n_past_blocks)
            kps.append(jnp.where(in_block, (j // BLOCKS_PER_PAGE) * PAGE_SIZE + page_lane, q_pos + 1))
            hit = j == cur
            new_chosen = hit if new_chosen is None else (new_chosen | hit)
        s_s = _dot(q, jnp.concatenate(kts, axis=1))
        s_new = jnp.where(new_chosen, jnp.sum(q * sel_new[:, kcol], axis=1, keepdims=True), NEG_INF)
        s_s = jnp.where(jnp.concatenate(kps, axis=1) <= q_pos, s_s, NEG_INF)
        m = jnp.maximum(jnp.max(s_s, axis=1, keepdims=True), s_new)
        m = jnp.where(m > NEG_INF, m, 0.0)
        e_s = jnp.exp(s_s - m)
        e_new = jnp.exp(s_new - m)
        d_s = jnp.sum(e_s, axis=1, keepdims=True) + e_new
        d_s = jnp.where(d_s > 0, d_s, 1.0)
        o_s = (_dot_nt(e_s, jnp.concatenate(vts, axis=1)) + e_new * sel_new[:, vcol]) / d_s
        s_w = jnp.where(m_w, _dot(q, wincache_ref[0, feat, :]), NEG_INF)
        s_wn = jnp.sum(q * win_new[:, kcol], axis=1, keepdims=True)
        m = jnp.maximum(jnp.max(s_w, axis=1, keepdims=True), s_wn)
        e_w = jnp.exp(s_w - m)
        e_wn = jnp.exp(s_wn - m)
        d_w = jnp.sum(e_w, axis=1, keepdims=True) + e_wn
        o_w = (_dot_nt(e_w, wincache_ref[1, feat, :]) + e_wn * win_new[:, vcol]) / d_w
        for h in range(HPG):
            hd = g * HPG + h
            out_parts.append(gates[:, 3 * hd:3 * hd + 1] * oc[:, hd * HEAD_DIM:(hd + 1) * HEAD_DIM]
                             + gates[:, 3 * hd + 1:3 * hd + 2] * o_s[h:h + 1]
                             + gates[:, 3 * hd + 2:3 * hd + 3] * o_w[h:h + 1])
    o_ref[...] = jnp.concatenate(out_parts, axis=1)


def _nsa_sample_attend(idx, page_table, q_rot, gates, o_c, sel_new, win_new, win_cache_t, sel_cache_t,
                       *, q_pos, n_past_blocks):
    b = q_rot.shape[0]
    wb = win_cache_t.shape[3]

    def page_spec(k):
        def index(i, idx_ref, pt_ref):
            j = jnp.minimum(idx_ref[i, k], n_past_blocks - 1)
            return (pt_ref[i, j // BLOCKS_PER_PAGE], 0, 0, 0)
        return pl.BlockSpec((None, 2, KV_W, PAGE_SIZE), index)

    one = lambda w: pl.BlockSpec((None, 1, w), lambda i, idx_ref, pt_ref: (i, 0, 0))
    return pl.pallas_call(
        functools.partial(_nsa_sample_attend_body, q_pos=q_pos, n_past_blocks=n_past_blocks),
        grid_spec=pltpu.PrefetchScalarGridSpec(
            num_scalar_prefetch=2,
            grid=(b,),
            in_specs=[one(Q_W), one(GATE_PAD), one(Q_W), one(ROW_W), one(ROW_W),
                      pl.BlockSpec((None, 2, KV_W, wb), lambda i, idx_ref, pt_ref: (i, 0, 0, 0))]
            + [page_spec(k) for k in range(N_KV * N_SEL)],
            out_specs=one(Q_W),
        ),
        out_shape=jax.ShapeDtypeStruct((b, 1, Q_W), F32),
        compiler_params=_params("arbitrary"),
        name="nsa_sample_attend",
    )(idx, page_table, q_rot, gates, o_c, sel_new, win_new, win_cache_t, *([sel_cache_t] * (N_KV * N_SEL)))


def _rope_tables(pos):
    half = HEAD_DIM // 2
    freqs = ROPE_THETA ** (-jnp.arange(half, dtype=F32) / half)
    ang = pos.astype(F32)[:, None] * freqs[None, :]
    cos, sin = jnp.cos(ang), jnp.sin(ang)
    reps = LANES // HEAD_DIM
    return jnp.tile(jnp.concatenate([cos, cos], axis=1), (1, reps)), jnp.tile(jnp.concatenate([-sin, sin], axis=1), (1, reps))


def _cover(nc, nsel, rows, cols):
    start = np.arange(rows)[:, None] * CMP_STRIDE
    j = np.arange(cols)[None, :]
    hit = (start < (j + 1) * SEL_BLOCK) & (start + CMP_BLOCK > j * SEL_BLOCK) & (np.arange(rows)[:, None] < nc) & (j < nsel)
    return jnp.asarray(hit.astype(np.float32))


def _cmp_chunk_weight(w1_k, w1_v):
    per = jnp.stack([w.reshape(CMP_RATIO, CMP_STRIDE, HEAD_DIM, CMP_HID) for w in (w1_k, w1_v)])
    eye = jnp.eye(N_KV, dtype=F32)
    big = jnp.einsum("ab,krsdh->ksadbrh", eye, per)
    return big.reshape(2, CMP_STRIDE, KV_W, N_KV * CMP_RATIO * CMP_HID).astype(BF16)


def _rg_gate_weight(w_a, w_x):
    nb, bw, _ = w_a.shape
    per = MXU_DIM // bw
    n_grp = nb // per
    eye = jnp.eye(per, dtype=F32)

    def group(w):
        w = w.reshape(n_grp, per, bw, bw)
        return jnp.einsum("pq,gpde->gpdqe", eye, w).reshape(n_grp, per * bw, per * bw)

    return jnp.concatenate([group(w_a), group(w_x)], axis=2).astype(BF16)


def _feature_major(kv_rows):
    lead = kv_rows.shape[:-4]
    rows = kv_rows.shape[-4]
    nd = len(lead)
    perm = tuple(range(nd)) + (nd + 1, nd + 2, nd + 3, nd)
    return jnp.transpose(kv_rows, perm).reshape(lead + (2, KV_W, rows))


def _row_major_view(kv_t, b, rows):
    return jnp.transpose(kv_t.reshape(b, 2, N_KV, HEAD_DIM, rows), (0, 4, 1, 2, 3))


def kernel(x_prompt, x_sample, cache_cmp_kv, cache_sel_kv, cache_win_kv, state_conv, state_h, page_table, ffn1_w_gate, ffn1_w_up, ffn1_w_down, ln1_g, ln1_b, w_in, cmp_pe_k, cmp_w1_k, cmp_w2_k, cmp_pe_v, cmp_w1_v, cmp_w2_v, conv_w, conv_b, rg_w_a, rg_b_a, rg_w_x, rg_b_x, rg_lam, w_br_attn, w_br_rnn, w_out, ln2_g, ln2_b, ffn2_w_gate, ffn2_w_up, ffn2_w_down, ln3_g, ln3_b):
    bp, tp, d = x_prompt.shape
    bs, ts, _ = x_sample.shape
    depth = w_in.shape[0]
    d_rnn = conv_w.shape[2]
    n_pages = page_table.shape[1]
    past_len = n_pages * PAGE_SIZE
    assert ts == 1 and tp % CMP_STRIDE == 0 and tp % SEL_BLOCK == 0
    alpha = (2.0 * depth) ** 0.25
    wb = cache_win_kv.shape[2]
    wbp = min(WINDOW, tp)

    nch_p = tp // CMP_STRIDE
    nc_p = nch_p - CMP_RATIO + 1
    nsel_p = tp // SEL_BLOCK
    nch_s = past_len // CMP_STRIDE
    nc_s = nch_s - CMP_RATIO + 1
    nsel_s = -(-(past_len + ts) // SEL_BLOCK)
    n_past_blocks = past_len // SEL_BLOCK
    assert nsel_s >= N_SEL and n_pages % PAGES_PER_STEP == 0
    nselp_s = -(-nsel_s // LANES) * LANES

    tm = min(256, tp)
    tq = min(256, tp)
    tt = min(256, tp)

    cos_p, sin_p = _rope_tables(jnp.arange(tp))
    cos_s, sin_s = _rope_tables(jnp.full((bs,), past_len))
    cover_p = _cover(nc_p, nsel_p, nch_p, nsel_p).T
    cover_s = _cover(nc_s, nsel_s, nch_s, nselp_s)
    expand_p = jnp.asarray((np.arange(tp)[:, None] // SEL_BLOCK == np.arange(nsel_p)[None, :]).astype(np.float32)).astype(BF16)

    xp = x_prompt.reshape(bp * tp, d)
    xs = x_sample.reshape(bs * ts, d)
    outs = [[] for _ in range(10)]
    for l in range(depth):
        row = lambda v: v[l].reshape(1, -1)
        ffn1 = (ffn1_w_gate[l].astype(BF16), ffn1_w_up[l].astype(BF16), ffn1_w_down[l].astype(BF16), row(ln1_g), row(ln1_b))
        ffn2 = (ffn2_w_gate[l].astype(BF16), ffn2_w_up[l].astype(BF16), ffn2_w_down[l].astype(BF16), row(ln3_g), row(ln3_b))
        n_a = Q_W + 6 * KV_W + 3 * N_HEADS
        w_a = jnp.pad(w_in[l][:, :n_a], ((0, 0), (0, GATE_PAD - 3 * N_HEADS))).astype(BF16)
        w_b = w_in[l][:, n_a:].astype(BF16)
        w_cmp = _cmp_chunk_weight(cmp_w1_k[l], cmp_w1_v[l])
        cmpw = (cmp_pe_k[l].reshape(1, -1), cmp_w1_k[l], cmp_w2_k[l], cmp_pe_v[l].reshape(1, -1), cmp_w1_v[l], cmp_w2_v[l])
        mixw = (jnp.pad(conv_w[l], ((0, SUBLANES - CONV_W), (0, 0))), row(conv_b), _rg_gate_weight(rg_w_a[l], rg_w_x[l]),
                row(rg_b_a), row(rg_b_x), row(rg_lam), w_br_attn[l].astype(BF16), w_br_rnn[l].astype(BF16),
                w_out[l].astype(BF16), row(ln2_g), row(ln2_b))

        h1 = _ffn_half_step(xp, *ffn1, alpha=alpha, tm=tm)
        (cmp, sel_k, win_k, xr, gr, ga, gb, q_raw_t, q_rot_t, gates_t, cmp_t, sel_t, win_t) = _in_proj(
            h1, cos_p, sin_p, w_a, w_b, tm=tm, seq_len=tp, feature_major=True)
        b3 = lambda a: a.reshape(bp, tp, a.shape[-1])
        p1 = _cmp_stage1(b3(cmp), w_cmp)
        ckv, ckv_t = _cmp_finish(p1, *cmpw, nc=nc_p)
        o_attn = _nsa_prompt(q_raw_t, q_rot_t, gates_t, ckv, ckv_t, b3(sel_k), sel_t, b3(win_k), win_t,
                             cover_p, expand_p, tq=tq, nc=nc_p)
        h2, h_last = _mix_prompt(b3(xr), b3(gr), b3(ga), b3(gb), o_attn, b3(h1), mixw, tt=tt, alpha=alpha)
        xp = _ffn_half_step(h2.reshape(bp * tp, d), *ffn2, alpha=alpha, tm=tm)
        outs[0].append(_row_major_view(cmp_t, bp, tp))
        outs[2].append(_row_major_view(sel_t, bp, tp))
        outs[4].append(_row_major_view(win_t[:, :, tp - wbp:], bp, wbp))
        outs[6].append(b3(xr)[:, tp - (CONV_W - 1):])
        outs[8].append(h_last.reshape(bp, d_rnn))

        h1s = _ffn_half_step(xs, *ffn1, alpha=alpha, tm=bs)
        q_raw, q_rot, cmp, sel, win, gates, xr, gr, ga, gb = _in_proj(
            h1s, cos_s, sin_s, w_a, w_b, tm=bs, seq_len=bs, feature_major=False)
        p1 = _cmp_stage1_paged(_feature_major(cache_cmp_kv[l]), page_table, w_cmp)
        ckv, _ = _cmp_finish(p1, *cmpw, nc=nc_s)
        b1 = lambda a: a.reshape(bs, 1, a.shape[-1])
        q_pos = past_len
        o_c, score = _nsa_sample_scores(b1(q_raw), ckv, cover_s, nc=nc_s, nsel=nsel_s, q_pos=q_pos)
        idx = _top_blocks(score.reshape(bs * SUBLANES, nselp_s))
        idx = idx.reshape(bs, SUBLANES, LANES)[:, :N_KV, :N_SEL].reshape(bs, N_KV * N_SEL)
        o_attn = _nsa_sample_attend(idx, page_table, b1(q_rot), b1(gates), o_c, b1(sel), b1(win),
                                    _feature_major(cache_win_kv[l]), _feature_major(cache_sel_kv[l]),
                                    q_pos=q_pos, n_past_blocks=n_past_blocks)
        conv_rows = [state_conv[l][:, k] for k in range(CONV_W - 1)]
        h2s, h_new = _mix_sample(xr, conv_rows, state_h[l], gr, ga, gb, o_attn.reshape(bs, Q_W), h1s, mixw, alpha=alpha)
        xs = _ffn_half_step(h2s, *ffn2, alpha=alpha, tm=bs)
        outs[1].append(cmp.reshape(bs, ts, 2, N_KV, HEAD_DIM))
        outs[3].append(sel.reshape(bs, ts, 2, N_KV, HEAD_DIM))
        win_all = jnp.concatenate([cache_win_kv[l], win.reshape(bs, ts, 2, N_KV, HEAD_DIM)], axis=1)
        outs[5].append(win_all[:, ts:])
        outs[7].append(jnp.concatenate([state_conv[l], xr[:, None, :]], axis=1)[:, ts:])
        outs[9].append(h_new)

    stacked = [jnp.stack(o) for o in outs]
    cmp_p, cmp_s, sel_p, sel_s, win_p, win_s, conv_p, conv_s, h_p, h_s = stacked
    return (xp.reshape(bp, tp, d), xs.reshape(bs, ts, d), cmp_p, cmp_s, sel_p, sel_s, win_p, win_s,
            conv_p, conv_s, h_p, h_s)
```

```python
import functools
import math

import numpy as np
import jax
import jax.numpy as jnp
from jax import lax
from jax.experimental import pallas as pl
from jax.experimental.pallas import tpu as pltpu

F32 = jnp.float32
BF16 = jnp.bfloat16

N_HEADS = 8
HEAD_DIM = 64
N_KV = 2
HPG = N_HEADS // N_KV
Q_W = N_HEADS * HEAD_DIM
KV_W = N_KV * HEAD_DIM
CMP_BLOCK = 32
CMP_STRIDE = 16
CMP_RATIO = CMP_BLOCK // CMP_STRIDE
CMP_HID = 64
SEL_BLOCK = 64
N_SEL = 16
SEL_BONUS = 1.0e4
WINDOW = 512
ROPE_THETA = 10000.0
RG_BLOCKS = 16
RG_C = 8.0
CONV_W = 4
PAGE_SIZE = 128
LN_EPS = 1e-5
SCALE = HEAD_DIM ** -0.5
QK_SCALE_LOG2 = SCALE * math.log2(math.e)
NEG_INF = float("-inf")

LANES = 128
SUBLANES = 8
MXU_DIM = 256
VMEM_LIMIT_BYTES = 56 * 1024 * 1024

ROW_W = 2 * KV_W
P_W = 2 * N_KV * CMP_RATIO * CMP_HID
GATE_PAD = LANES
PAGES_PER_STEP = 32
BLOCKS_PER_PAGE = PAGE_SIZE // SEL_BLOCK


def _dot(a, b):
    return jnp.dot(a.astype(BF16), b.astype(BF16), preferred_element_type=F32)


def _dot_nt(a, b):
    return lax.dot_general(a.astype(BF16), b.astype(BF16), (((1,), (1,)), ((), ())),
                           preferred_element_type=F32)


def _dot_f32(a, b):
    return jnp.dot(a, b, preferred_element_type=F32, precision=lax.Precision.HIGHEST)


def _sigmoid(x):
    return 0.5 * jnp.tanh(0.5 * x) + 0.5


def _gelu(x):
    return 0.5 * x * (1.0 + jnp.tanh(math.sqrt(2.0 / math.pi) * (x + 0.044715 * (x * x * x))))


def _layer_norm(y, g, b):
    mu = jnp.mean(y, axis=-1, keepdims=True)
    yc = y - mu
    var = jnp.mean(yc * yc, axis=-1, keepdims=True)
    return yc * lax.rsqrt(var + LN_EPS) * g + b


def _softmax_parts(s, mask):
    s = jnp.where(mask, s, NEG_INF)
    m = jnp.max(s, axis=-1, keepdims=True)
    m = jnp.where(m > NEG_INF, m, 0.0)
    e = jnp.exp(s - m)
    d = jnp.sum(e, axis=-1, keepdims=True)
    return e, jnp.where(d > 0, d, 1.0)


def _const_spec(shape):
    nd = len(shape)
    return pl.BlockSpec(shape, lambda *_: (0,) * nd)


def _params(*sem):
    return pltpu.CompilerParams(dimension_semantics=sem, vmem_limit_bytes=VMEM_LIMIT_BYTES)


def _ffn_body(x_ref, *refs, alpha, pre_norm):
    x = x_ref[...]
    if pre_norm:
        gin_ref, bin_ref, *refs = refs
        x = _layer_norm(x, gin_ref[...], bin_ref[...])
    wg_ref, wu_ref, wd_ref, g_ref, b_ref, o_ref = refs
    xb = x.astype(BF16)
    gate = jnp.dot(xb, wg_ref[...], preferred_element_type=F32)
    up = jnp.dot(xb, wu_ref[...], preferred_element_type=F32)
    hmid = gate * _sigmoid(gate) * up
    f = _dot(hmid, wd_ref[...])
    o_ref[...] = _layer_norm(alpha * x + 0.5 * f, g_ref[...], b_ref[...])


def _ffn_half_step(x, wg, wu, wd, g, b, *, alpha, tm, norm_in=None):
    n, d = x.shape
    ff = wg.shape[1]
    norm_in = tuple(norm_in or ())
    return pl.pallas_call(
        functools.partial(_ffn_body, alpha=alpha, pre_norm=bool(norm_in)),
        grid=(n // tm,),
        in_specs=[pl.BlockSpec((tm, d), lambda i: (i, 0))] + [_const_spec((1, d))] * len(norm_in)
        + [_const_spec((d, ff)), _const_spec((d, ff)), _const_spec((ff, d)), _const_spec((1, d)), _const_spec((1, d))],
        out_specs=pl.BlockSpec((tm, d), lambda i: (i, 0)),
        out_shape=jax.ShapeDtypeStruct((n, d), F32),
        compiler_params=_params("parallel"),
        name="ffn_half_step",
    )(x, *norm_in, wg, wu, wd, g, b)


def _rope(x, cos, sin):
    w = x.shape[1]
    reps = w // LANES
    c = jnp.tile(cos, (1, reps)) if reps > 1 else cos
    s = jnp.tile(sin, (1, reps)) if reps > 1 else sin
    lane = lax.broadcasted_iota(jnp.int32, x.shape, 1)
    first_half = (lane % HEAD_DIM) < (HEAD_DIM // 2)
    swapped = jnp.where(first_half, pltpu.roll(x, w - HEAD_DIM // 2, 1), pltpu.roll(x, HEAD_DIM // 2, 1))
    return x * c + swapped * s


def _inproj_body(h_ref, cos_ref, sin_ref, wa_ref, wb_ref, *refs, feature_major, tiles):
    if feature_major:
        convw_ref, convb_ref, wgate_ref, ba_ref, bx_ref, lam_ref, *out_refs, tail_ref = refs

        @pl.when(pl.program_id(0) % tiles == 0)
        def _():
            tail_ref[...] = jnp.zeros_like(tail_ref)
    else:
        out_refs = refs
    hb = h_ref[...].astype(BF16)
    cos = cos_ref[...]
    sin = sin_ref[...]
    d = wb_ref.shape[1] // 4
    wb_dot = lambda k: jnp.dot(hb, wb_ref[:, k * d:(k + 1) * d], preferred_element_type=F32)
    xr = wb_dot(0)
    if feature_major:
        (cmp_ref, selk_ref, wink_ref, a_ref, u_ref, gr_ref, ga_ref, gb_ref, xtail_ref,
         qrawt_ref, qrott_ref, gatest_ref, cmpt_ref, selt_ref, wint_ref) = out_refs
        xc = _causal_conv(xr, tail_ref[...], convw_ref[...], convb_ref[...])
        tail = xr[xr.shape[0] - SUBLANES:]
        tail_ref[...] = tail
        xtail_ref[...] = tail
    za = jnp.dot(hb, wa_ref[...], preferred_element_type=F32)
    if feature_major:
        a, u = _rg_gates(xc, wgate_ref, ba_ref[...], bx_ref[...], lam_ref[...])
        a_ref[...] = a
        u_ref[...] = u
    gr = wb_dot(1)
    q = za[:, :Q_W]
    kv = za[:, Q_W:Q_W + 6 * KV_W]
    gates = _sigmoid(za[:, Q_W + 6 * KV_W:])
    q_rot = _rope(q, cos, sin)
    cmp = kv[:, :2 * KV_W]
    sel_k = _rope(kv[:, 2 * KV_W:3 * KV_W], cos, sin)
    win_k = _rope(kv[:, 4 * KV_W:5 * KV_W], cos, sin)
    sel = jnp.concatenate([sel_k, kv[:, 3 * KV_W:4 * KV_W]], axis=1)
    win = jnp.concatenate([win_k, kv[:, 5 * KV_W:6 * KV_W]], axis=1)
    ga = wb_dot(2)
    if feature_major:
        cmp_ref[...] = cmp
        selk_ref[...] = sel_k.astype(BF16)
        wink_ref[...] = win_k.astype(BF16)
        qrawt_ref[...] = q.T
        qrott_ref[...] = q_rot.T
        gatest_ref[...] = gates.T
        cmpt_ref[...] = cmp.T
        selt_ref[...] = sel.T
        wint_ref[...] = win.T
    else:
        qraw_ref, qrot_ref, cmp_ref, sel_ref, win_ref, gates_ref, xr_ref, gr_ref, ga_ref, gb_ref = out_refs
        qraw_ref[...] = q
        qrot_ref[...] = q_rot
        cmp_ref[...] = cmp
        sel_ref[...] = sel
        win_ref[...] = win
        gates_ref[...] = gates
    gb = wb_dot(3)
    if feature_major:
        gr_ref[...] = _gelu(gr)
        ga_ref[...] = _sigmoid(ga)
        gb_ref[...] = _sigmoid(gb)
    else:
        xr_ref[...] = xr
        gr_ref[...] = gr
        ga_ref[...] = ga
        gb_ref[...] = gb


def _in_proj(h, cos, sin, wa, wb, rgw=(), *, tm, seq_len, feature_major):
    n, d = h.shape
    wa_w, wb_w = wa.shape[1], wb.shape[1]
    tiles = seq_len // tm
    row = lambda i: (i, 0)
    tab = lambda i: (i % tiles, 0)
    if feature_major:
        rows = [(ROW_W, F32), (KV_W, BF16), (KV_W, BF16)] + [(d, F32)] * 5
        cols = [Q_W, Q_W, GATE_PAD, ROW_W, ROW_W, ROW_W]
    else:
        rows = [(w, F32) for w in (Q_W, Q_W, ROW_W, ROW_W, ROW_W, GATE_PAD, d, d, d, d)]
        cols = []
    out_specs = [pl.BlockSpec((tm, w), row) for w, _ in rows]
    out_shape = [jax.ShapeDtypeStruct((n, w), dt) for w, dt in rows]
    if feature_major:
        out_specs.append(pl.BlockSpec((None, SUBLANES, d), lambda i: (i // tiles, 0, 0)))
        out_shape.append(jax.ShapeDtypeStruct((n // seq_len, SUBLANES, d), F32))
    out_specs += [pl.BlockSpec((None, w, tm), lambda i: (i // tiles, 0, i % tiles)) for w in cols]
    out_shape += [jax.ShapeDtypeStruct((n // seq_len, w, seq_len), F32) for w in cols]
    return pl.pallas_call(
        functools.partial(_inproj_body, feature_major=feature_major, tiles=tiles),
        grid=(n // tm,),
        in_specs=[pl.BlockSpec((tm, d), row), pl.BlockSpec((tm, LANES), tab), pl.BlockSpec((tm, LANES), tab),
                  _const_spec((d, wa_w)), _const_spec((d, wb_w))] + _rg_weight_specs(rgw),
        out_specs=out_specs,
        out_shape=out_shape,
        scratch_shapes=[pltpu.VMEM((SUBLANES, d), F32)] if feature_major else [],
        compiler_params=_params("arbitrary" if feature_major else "parallel"),
        name="in_proj",
    )(h, cos, sin, wa, wb, *rgw)


def _cmp_chunks(rows_of, w_ref, kv, nch):
    acc = None
    for s in range(CMP_STRIDE):
        part = _dot(rows_of(s, nch), w_ref[kv, s])
        acc = part if acc is None else acc + part
    return acc


def _cmp_stage1_body(x_ref, w_ref, o_ref, rows_ref):
    nch = x_ref.shape[0] // CMP_STRIDE
    half = P_W // 2
    for kv in range(2):
        rows_ref[kv] = x_ref[:, kv * KV_W:(kv + 1) * KV_W]
        rows_of = lambda s, n, kv=kv: rows_ref[kv, pl.ds(s, n, stride=CMP_STRIDE), :]
        o_ref[:, kv * half:(kv + 1) * half] = _cmp_chunks(rows_of, w_ref, kv, nch)


def _cmp_stage1(x, w):
    b, t, _ = x.shape
    nch = t // CMP_STRIDE
    return pl.pallas_call(
        _cmp_stage1_body,
        grid=(b,),
        in_specs=[pl.BlockSpec((None, t, ROW_W), lambda i: (i, 0, 0)), _const_spec(w.shape)],
        out_specs=pl.BlockSpec((None, nch, P_W), lambda i: (i, 0, 0)),
        out_shape=jax.ShapeDtypeStruct((b, nch, P_W), F32),
        scratch_shapes=[pltpu.VMEM((2, t, KV_W), F32)],
        compiler_params=_params("parallel"),
        name="cmp_stage1",
    )(x, w)


def _cmp_paged_body(pt_ref, *refs):
    del pt_ref
    x_refs = refs[:PAGES_PER_STEP]
    perm_ref, w_ref, o_ref, planes_ref, stage_ref = refs[PAGES_PER_STEP:]
    cpp = PAGE_SIZE // CMP_STRIDE
    perm = perm_ref[...]
    for k, x_ref in enumerate(x_refs):
        for kv in range(2):
            stage_ref[k, kv] = jnp.dot(x_ref[kv].astype(BF16), perm, preferred_element_type=F32)
            rows = stage_ref[k, kv].T
            for s in range(CMP_STRIDE):
                planes_ref[kv, s, k * cpp:(k + 1) * cpp, :] = rows[s * cpp:(s + 1) * cpp]
    half = P_W // 2
    for kv in range(2):
        acc = None
        for j in range(CMP_STRIDE // 2):
            pair = jnp.concatenate([planes_ref[kv, 2 * j], planes_ref[kv, 2 * j + 1]], axis=1)
            part = _dot(pair, w_ref[kv, j])
            acc = part if acc is None else acc + part
        o_ref[:, kv * half:(kv + 1) * half] = acc


def _cmp_stage1_paged(cache_t, page_table, w):
    b, n_pages = page_table.shape
    steps = n_pages // PAGES_PER_STEP
    cpp = PAGE_SIZE // CMP_STRIDE
    nch = n_pages * cpp
    w = w.reshape(2, CMP_STRIDE // 2, 2 * KV_W, w.shape[-1])
    row = np.arange(PAGE_SIZE)
    perm = np.zeros((PAGE_SIZE, PAGE_SIZE), np.float32)
    perm[row, (row % CMP_STRIDE) * cpp + row // CMP_STRIDE] = 1.0
    perm = jnp.asarray(perm).astype(BF16)

    def page_spec(k):
        return pl.BlockSpec((None, 2, KV_W, PAGE_SIZE), lambda i, s, pt: (pt[i, s * PAGES_PER_STEP + k], 0, 0, 0))

    return pl.pallas_call(
        _cmp_paged_body,
        grid_spec=pltpu.PrefetchScalarGridSpec(
            num_scalar_prefetch=1,
            grid=(b, steps),
            in_specs=[page_spec(k) for k in range(PAGES_PER_STEP)]
            + [pl.BlockSpec(perm.shape, lambda i, s, pt: (0, 0)),
               pl.BlockSpec(w.shape, lambda i, s, pt: (0,) * w.ndim)],
            out_specs=pl.BlockSpec((None, PAGES_PER_STEP * cpp, P_W), lambda i, s, pt: (i, s, 0)),
            scratch_shapes=[pltpu.VMEM((2, CMP_STRIDE, PAGES_PER_STEP * cpp, KV_W), F32),
                            pltpu.VMEM((PAGES_PER_STEP, 2, KV_W, PAGE_SIZE), F32)],
        ),
        out_shape=jax.ShapeDtypeStruct((b, nch, P_W), F32),
        compiler_params=_params("parallel", "parallel"),
        name="cmp_stage1_paged",
    )(page_table, *([cache_t] * PAGES_PER_STEP), perm, w)


def _cmp_finish_body(p_ref, pek_ref, w1k_ref, w2k_ref, pev_ref, w1v_ref, w2v_ref, o_ref, ot_ref, *, nc):
    p = p_ref[...]
    nch = p.shape[0]
    row = lax.broadcasted_iota(jnp.int32, (nch, CMP_HID), 0)
    outs = []
    for kv, (pe_ref, w1_ref, w2_ref) in enumerate(((pek_ref, w1k_ref, w2k_ref), (pev_ref, w1v_ref, w2v_ref))):
        pe = jnp.broadcast_to(pe_ref[...], (SUBLANES, pe_ref.shape[1]))
        bias = _dot(pe, w1_ref[...])[0:1]
        for g in range(N_KV):
            c0 = (kv * N_KV + g) * CMP_RATIO * CMP_HID
            u = bias + p[:, c0:c0 + CMP_HID]
            for r in range(1, CMP_RATIO):
                u = u + pltpu.roll(p[:, c0 + r * CMP_HID:c0 + (r + 1) * CMP_HID], nch - r, 0)
            out = _dot(_gelu(u), w2_ref[...])
            outs.append(jnp.where(row < nc, out, 0.0))
    ckv = jnp.concatenate(outs, axis=1)
    o_ref[...] = ckv
    ot_ref[...] = ckv.T


def _cmp_finish(p, pe_k, w1_k, w2_k, pe_v, w1_v, w2_v, *, nc):
    b, nch, _ = p.shape
    flat = CMP_BLOCK * HEAD_DIM
    return pl.pallas_call(
        functools.partial(_cmp_finish_body, nc=nc),
        grid=(b,),
        in_specs=[pl.BlockSpec((None, nch, P_W), lambda i: (i, 0, 0)),
                  _const_spec((1, flat)), _const_spec((flat, CMP_HID)), _const_spec((CMP_HID, HEAD_DIM)),
                  _const_spec((1, flat)), _const_spec((flat, CMP_HID)), _const_spec((CMP_HID, HEAD_DIM))],
        out_specs=[pl.BlockSpec((None, nch, ROW_W), lambda i: (i, 0, 0)),
                   pl.BlockSpec((None, ROW_W, nch), lambda i: (i, 0, 0))],
        out_shape=[jax.ShapeDtypeStruct((b, nch, ROW_W), F32), jax.ShapeDtypeStruct((b, ROW_W, nch), F32)],
        compiler_params=_params("parallel"),
        name="cmp_finish",
    )(p, pe_k, w1_k, w2_k, pe_v, w1_v, w2_v)


def _block_scores(imp, q_pos, nsel_valid, block_axis=1):
    jblk = lax.broadcasted_iota(jnp.int32, imp.shape, block_axis)
    cur = q_pos // SEL_BLOCK
    valid = (jblk * SEL_BLOCK <= q_pos) & (jblk < nsel_valid)
    forced = (jblk == 0) | (jblk == cur) | (jblk == cur - 1)
    return jnp.where(valid, imp + jnp.where(forced, SEL_BONUS, 0.0), NEG_INF)


def _nsa_prompt_body(qrawt_ref, qrott_ref, gatest_ref, ckv_ref, ckvt_ref, selk_ref, selt_ref, wink_ref, wint_ref,
                     covert_ref, expandt_ref, o_ref, *, tq, nc):
    nchp = ckv_ref.shape[0]
    nsel = covert_ref.shape[0]
    tile = pl.program_id(1)
    s0 = tile * tq
    lane_t = lax.broadcasted_iota(jnp.int32, (1, tq), 1)
    key_sub = lax.broadcasted_iota(jnp.int32, (tq, 1), 0)
    qpos = s0 + lane_t
    causal_diag = key_sub <= lane_t
    window_edge = lane_t <= key_sub
    cidx = lax.broadcasted_iota(jnp.int32, (nchp, 1), 0)
    jblk = lax.broadcasted_iota(jnp.int32, (nsel, 1), 0)
    n_win_chunks = WINDOW // tq
    vrows = [slice(KV_W + g * HEAD_DIM, KV_W + (g + 1) * HEAD_DIM) for g in range(N_KV)]
    group_of = lambda hd: hd // HPG

    def group_rows(qt, g):
        z = jnp.zeros_like(qt)
        return jnp.concatenate([qt, z] if g == 0 else [z, qt], axis=0)

    def head_rows(ref, hd):
        return group_rows((ref[hd * HEAD_DIM:(hd + 1) * HEAD_DIM, :] * QK_SCALE_LOG2).astype(BF16), group_of(hd))

    q_rot = [head_rows(qrott_ref, hd) for hd in range(N_HEADS)]
    q_raw = jnp.concatenate([head_rows(qrawt_ref, hd) for hd in range(N_HEADS)], axis=1)
    s_c = jnp.dot(ckv_ref[:, :KV_W].astype(BF16), q_raw, preferred_element_type=F32)
    m_c = (cidx * CMP_STRIDE + (CMP_BLOCK - 1) <= jnp.concatenate([qpos] * N_HEADS, axis=1)) & (cidx < nc)
    s_c = jnp.where(m_c, s_c, NEG_INF)
    mx = jnp.max(s_c, axis=0, keepdims=True)
    e_c = jnp.exp2(s_c - jnp.where(mx > NEG_INF, mx, 0.0))
    d_c = jnp.sum(e_c, axis=0, keepdims=True)
    p_c = e_c * (1.0 / jnp.where(d_c > 0, d_c, 1.0))
    head_cols = lambda hd: slice(hd * tq, (hd + 1) * tq)
    o_c = [_dot(ckvt_ref[vrows[group_of(hd)], :], p_c[:, head_cols(hd)]) for hd in range(N_HEADS)]
    p_sum = []
    for g in range(N_KV):
        acc = p_c[:, head_cols(g * HPG)]
        for hd in range(g * HPG + 1, (g + 1) * HPG):
            acc = acc + p_c[:, head_cols(hd)]
        p_sum.append(acc)
    imp = _dot_f32(covert_ref[...], jnp.concatenate(p_sum, axis=1))
    score = _block_scores(imp, jnp.concatenate([qpos] * N_KV, axis=1), nsel, block_axis=0)
    rank = jnp.zeros(score.shape, jnp.int32)
    for i in range(nsel):
        s_i = score[i:i + 1, :]
        beats = (s_i > score) | ((s_i == score) & (i < jblk))
        rank = rank + beats.astype(jnp.int32)
    chosen = (rank < N_SEL).astype(BF16)

    def picked(k0):
        hit = jnp.dot(expandt_ref[pl.ds(k0, tq), :], chosen, preferred_element_type=F32)
        return [hit[:, g * tq:(g + 1) * tq] > 0.5 for g in range(N_KV)]

    def attend(jobs):
        loaded = []
        for k_ref, vt_ref, kc, masks_of, _ in jobs:
            k0 = pl.multiple_of(kc * tq, tq)
            keys = k_ref[pl.ds(k0, tq), :]
            values_t = [vt_ref[rows, pl.ds(k0, tq)].astype(BF16) for rows in vrows]
            loaded.append((keys, values_t, masks_of(k0)))
        scores = [[jnp.dot(keys, q_rot[hd], preferred_element_type=F32) for hd in range(N_HEADS)]
                  for keys, _, _ in loaded]
        updates = []
        for job, (_, _, masks), sc in zip(jobs, loaded, scores):
            carry = job[4]
            stats, probs = [], []
            for hd in range(N_HEADS):
                m, l, _ = carry[hd]
                s = sc[hd] if masks is None else jnp.where(masks[group_of(hd)], sc[hd], NEG_INF)
                m_new = jnp.maximum(m, jnp.max(s, axis=0, keepdims=True))
                m_safe = m_new if masks is None else jnp.where(m_new > NEG_INF, m_new, 0.0)
                alpha = jnp.exp2(m - m_safe)
                p = jnp.exp2(s - m_safe)
                stats.append((m_new, l * alpha + jnp.sum(p, axis=0, keepdims=True), alpha))
                probs.append(p.astype(BF16))
            updates.append((stats, probs))
        out = []
        for job, (_, values_t, _), (stats, probs) in zip(jobs, loaded, updates):
            carry = job[4]
            pv = [jnp.dot(values_t[group_of(hd)], probs[hd], preferred_element_type=F32) for hd in range(N_HEADS)]
            out.append(tuple((stats[hd][0], stats[hd][1], carry[hd][2] * stats[hd][2] + pv[hd])
                             for hd in range(N_HEADS)))
        return out

    def finish(carry):
        return [acc * (1.0 / jnp.where(l > 0, l, 1.0)) for _, l, acc in carry]

    init = tuple((jnp.full((1, tq), NEG_INF, F32), jnp.zeros((1, tq), F32), jnp.zeros((HEAD_DIM, tq), F32))
                 for _ in range(N_HEADS))
    no_mask = lambda k0: None
    sel = lax.fori_loop(0, tile, lambda kc, c: attend([(selk_ref, selt_ref, kc, picked, c)])[0], init)
    edge = tile - n_win_chunks
    win = lax.fori_loop(jnp.maximum(edge, 0), jnp.maximum(edge + 1, 0),
                        lambda kc, c: attend([(wink_ref, wint_ref, kc, lambda k0: [window_edge] * N_KV, c)])[0], init)
    win = lax.fori_loop(jnp.maximum(edge + 1, 0), tile,
                        lambda kc, c: attend([(wink_ref, wint_ref, kc, no_mask, c)])[0], win)
    sel, win = attend([(selk_ref, selt_ref, tile, lambda k0: [hit & causal_diag for hit in picked(k0)], sel),
                       (wink_ref, wint_ref, tile, lambda k0: [causal_diag] * N_KV, win)])
    o_s, o_w = finish(sel), finish(win)
    out_heads = []
    for hd in range(N_HEADS):
        gate = lambda branch: gatest_ref[3 * hd + branch:3 * hd + branch + 1, :]
        out_heads.append(gate(0) * o_c[hd] + gate(1) * o_s[hd] + gate(2) * o_w[hd])
    o_ref[...] = jnp.concatenate(out_heads, axis=0).T


def _nsa_prompt(q_raw_t, q_rot_t, gates_t, ckv, ckv_t, sel_k, sel_t, win_k, win_t, cover_t, expand_t, *, tq, nc):
    b, _, t = q_raw_t.shape
    nchp = ckv.shape[1]
    nsel = cover_t.shape[0]
    assert WINDOW % tq == 0
    tile_t = lambda w: pl.BlockSpec((None, w, tq), lambda i, j: (i, 0, j))
    whole = lambda r, w: pl.BlockSpec((None, r, w), lambda i, j: (i, 0, 0))
    return pl.pallas_call(
        functools.partial(_nsa_prompt_body, tq=tq, nc=nc),
        grid=(b, t // tq),
        in_specs=[tile_t(Q_W), tile_t(Q_W), tile_t(GATE_PAD), whole(nchp, ROW_W), whole(ROW_W, nchp),
                  whole(t, KV_W), whole(ROW_W, t), whole(t, KV_W), whole(ROW_W, t),
                  _const_spec((nsel, nchp)), _const_spec((t, nsel))],
        out_specs=pl.BlockSpec((None, tq, Q_W), lambda i, j: (i, j, 0)),
        out_shape=jax.ShapeDtypeStruct((b, t, Q_W), F32),
        compiler_params=_params("parallel", "arbitrary"),
        name="nsa_prompt",
    )(q_raw_t, q_rot_t, gates_t, ckv, ckv_t, sel_k, sel_t, win_k, win_t, cover_t, expand_t)


def _softplus(x):
    return jnp.maximum(x, 0.0) + jnp.log1p(jnp.exp(-jnp.abs(x)))


def _rg_gates(xc, wgate_ref, ba, bx, lam):
    xcb = xc.astype(BF16)
    n_grp = wgate_ref.shape[0]
    gw = wgate_ref.shape[1]
    za, zx = [], []
    for k in range(n_grp):
        z = jnp.dot(xcb[:, k * gw:(k + 1) * gw], wgate_ref[k], preferred_element_type=F32)
        za.append(z[:, :gw])
        zx.append(z[:, gw:])
    r = _sigmoid(jnp.concatenate(za, axis=1) + ba)
    i = _sigmoid(jnp.concatenate(zx, axis=1) + bx)
    log_a = -RG_C * r * _softplus(-lam)
    a = jnp.exp(log_a)
    u = jnp.sqrt(-jnp.tanh(log_a) * (1.0 + a * a)) * i * xc
    return a, u


def _merge(h_res, hs, gr_act, ga_act, gb_act, o_attn, wbra_ref, wbrr_ref, wout_ref, alpha):
    y_rnn = hs * gr_act
    m = ga_act * _dot(o_attn, wbra_ref[...]) + gb_act * _dot(y_rnn, wbrr_ref[...])
    return alpha * h_res + _dot(m, wout_ref[...])


def _causal_conv(x, tail, convw, convb):
    sub = lax.broadcasted_iota(jnp.int32, (SUBLANES, x.shape[1]), 0)
    shifts = range(1, CONV_W)
    prev_rot = [pltpu.roll(tail, d, 0) for d in shifts]
    groups = []
    for k in range(x.shape[0] // SUBLANES):
        x_grp = x[k * SUBLANES:(k + 1) * SUBLANES]
        rot = [pltpu.roll(x_grp, d, 0) for d in shifts]
        xc_grp = convb + convw[CONV_W - 1:CONV_W] * x_grp
        for d, r_prev, r_cur in zip(shifts, prev_rot, rot):
            xc_grp = xc_grp + convw[CONV_W - 1 - d:CONV_W - d] * jnp.where(sub < d, r_prev, r_cur)
        groups.append(xc_grp)
        prev_rot = rot
    return jnp.concatenate(groups, axis=0)


def _mix_prompt_body(a_ref, u_ref, gr_ref, ga_ref, gb_ref, oat_ref, h1_ref, wbra_ref, wbrr_ref, wout_ref,
                     o_ref, hlast_ref, hc_ref, *, tt, alpha):
    @pl.when(pl.program_id(1) == 0)
    def _():
        hc_ref[...] = jnp.zeros_like(hc_ref)

    a, u = a_ref[...], u_ref[...]
    sub = lax.broadcasted_iota(jnp.int32, (SUBLANES, a.shape[1]), 0)
    h_prev = jnp.broadcast_to(hc_ref[...], sub.shape)
    groups = []
    for k in range(tt // SUBLANES):
        rows = slice(k * SUBLANES, (k + 1) * SUBLANES)
        a_grp, u_grp = a[rows], u[rows]
        d = 1
        while d < SUBLANES:
            inside = sub >= d
            u_grp = jnp.where(inside, a_grp * pltpu.roll(u_grp, d, 0) + u_grp, u_grp)
            a_grp = jnp.where(inside, a_grp * pltpu.roll(a_grp, d, 0), a_grp)
            d *= 2
        h_grp = a_grp * h_prev + u_grp
        groups.append(h_grp)
        h_prev = jnp.broadcast_to(h_grp[SUBLANES - 1:SUBLANES], h_grp.shape)
    hs = jnp.concatenate(groups, axis=0)
    hc_ref[...] = hs[tt - 1:tt]
    hlast_ref[...] = hs[tt - 1:tt]
    o_ref[...] = _merge(h1_ref[...], hs, gr_ref[...], ga_ref[...], gb_ref[...], oat_ref[...],
                        wbra_ref, wbrr_ref, wout_ref, alpha)


def _rg_weight_specs(rgw):
    return [_const_spec(w.shape) for w in rgw]


def _mix_prompt(a, u, gr_act, ga_act, gb_act, o_attn, h1, mergew, *, tt, alpha):
    b, t, d = a.shape
    tile = lambda w: pl.BlockSpec((None, tt, w), lambda i, j: (i, j, 0))
    return pl.pallas_call(
        functools.partial(_mix_prompt_body, tt=tt, alpha=alpha),
        grid=(b, t // tt),
        in_specs=[tile(d)] * 5 + [tile(Q_W), tile(d)] + [_const_spec(w.shape) for w in mergew],
        out_specs=[tile(d), pl.BlockSpec((None, 1, d), lambda i, j: (i, 0, 0))],
        out_shape=[jax.ShapeDtypeStruct((b, t, d), F32), jax.ShapeDtypeStruct((b, 1, d), F32)],
        scratch_shapes=[pltpu.VMEM((1, d), F32)],
        compiler_params=_params("parallel", "arbitrary"),
        name="mix_prompt",
    )(a, u, gr_act, ga_act, gb_act, o_attn, h1, *mergew)


def _mix_sample_body(xr_ref, c0_ref, c1_ref, c2_ref, h0_ref, gr_ref, ga_ref, gb_ref, oat_ref, h1_ref,
                     convw_ref, convb_ref, wgate_ref, ba_ref, bx_ref, lam_ref, wbra_ref, wbrr_ref, wout_ref,
                     o_ref, hnew_ref, *, alpha):
    convw = convw_ref[...]
    xc = (convb_ref[...] + convw[0:1] * c0_ref[...] + convw[1:2] * c1_ref[...] + convw[2:3] * c2_ref[...]
          + convw[3:4] * xr_ref[...])
    a, u = _rg_gates(xc, wgate_ref, ba_ref[...], bx_ref[...], lam_ref[...])
    hs = a * h0_ref[...] + u
    hnew_ref[...] = hs
    o_ref[...] = _merge(h1_ref[...], hs, _gelu(gr_ref[...]), _sigmoid(ga_ref[...]), _sigmoid(gb_ref[...]),
                        oat_ref[...], wbra_ref, wbrr_ref, wout_ref, alpha)


def _mix_sample(xr, conv_rows, h0, gr, ga, gb, o_attn, h1, rgw, mergew, *, alpha):
    b, d = xr.shape
    full = lambda w: _const_spec((b, w))
    return pl.pallas_call(
        functools.partial(_mix_sample_body, alpha=alpha),
        grid=(1,),
        in_specs=[full(d)] * 8 + [full(Q_W), full(d)] + _rg_weight_specs(rgw) + [_const_spec(w.shape) for w in mergew],
        out_specs=[full(d), full(d)],
        out_shape=[jax.ShapeDtypeStruct((b, d), F32), jax.ShapeDtypeStruct((b, d), F32)],
        compiler_params=_params("arbitrary"),
        name="mix_sample",
    )(xr, *conv_rows, h0, gr, ga, gb, o_attn, h1, *rgw, *mergew)


def _head_rows(row, g):
    parts = [row[:, (g * HPG + h) * HEAD_DIM:(g * HPG + h + 1) * HEAD_DIM] for h in range(HPG)]
    return jnp.concatenate(parts + [jnp.zeros((SUBLANES - HPG, HEAD_DIM), F32)], axis=0)


def _nsa_sample_scores_body(qraw_ref, ckv_ref, cover_ref, oc_ref, score_ref, *, nc, nsel, q_pos):
    nchp = ckv_ref.shape[0]
    nselp = cover_ref.shape[1]
    cidx = lax.broadcasted_iota(jnp.int32, (1, nchp), 1)
    qrow = qraw_ref[...] * SCALE
    m_c = (cidx * CMP_STRIDE + (CMP_BLOCK - 1) <= q_pos) & (cidx < nc)
    oc_parts, score_rows = [], []
    for g in range(N_KV):
        q = _head_rows(qrow, g)
        s_c = _dot_nt(q, ckv_ref[:, g * HEAD_DIM:(g + 1) * HEAD_DIM])
        e_c, d_c = _softmax_parts(s_c, m_c)
        p_c = e_c / d_c
        o_c = _dot(p_c, ckv_ref[:, KV_W + g * HEAD_DIM:KV_W + (g + 1) * HEAD_DIM])
        oc_parts += [o_c[h:h + 1] for h in range(HPG)]
        p_sum = jnp.sum(p_c[0:HPG], axis=0, keepdims=True)
        imp = _dot_f32(jnp.broadcast_to(p_sum, (SUBLANES, nchp)), cover_ref[...])[0:1]
        score_rows.append(_block_scores(imp, q_pos, nsel))
    oc_ref[...] = jnp.concatenate(oc_parts, axis=1)
    score_ref[...] = jnp.concatenate(score_rows + [jnp.full((SUBLANES - N_KV, nselp), NEG_INF, F32)], axis=0)


def _nsa_sample_scores(q_raw, ckv, cover, *, nc, nsel, q_pos):
    b = q_raw.shape[0]
    nchp = ckv.shape[1]
    nselp = cover.shape[1]
    return pl.pallas_call(
        functools.partial(_nsa_sample_scores_body, nc=nc, nsel=nsel, q_pos=q_pos),
        grid=(b,),
        in_specs=[pl.BlockSpec((None, 1, Q_W), lambda i: (i, 0, 0)),
                  pl.BlockSpec((None, nchp, ROW_W), lambda i: (i, 0, 0)), _const_spec((nchp, nselp))],
        out_specs=[pl.BlockSpec((None, 1, Q_W), lambda i: (i, 0, 0)),
                   pl.BlockSpec((None, SUBLANES, nselp), lambda i: (i, 0, 0))],
        out_shape=[jax.ShapeDtypeStruct((b, 1, Q_W), F32), jax.ShapeDtypeStruct((b, SUBLANES, nselp), F32)],
        compiler_params=_params("parallel"),
        name="nsa_sample_scores",
    )(q_raw, ckv, cover)


def _top_blocks_body(score_ref, idx_ref):
    score = score_ref[...]
    n, nselp = score.shape
    lane = lax.broadcasted_iota(jnp.int32, (n, nselp), 1)
    lane_out = lax.broadcasted_iota(jnp.int32, (n, LANES), 1)
    picked = jnp.zeros((n, LANES), jnp.int32)
    for r in range(N_SEL):
        m = jnp.max(score, axis=1, keepdims=True)
        j = jnp.min(jnp.where(score == m, lane, nselp), axis=1, keepdims=True)
        picked = jnp.where(lane_out == r, j, picked)
        score = jnp.where(lane == j, NEG_INF, score)
    idx_ref[...] = picked


def _top_blocks(score):
    n, nselp = score.shape
    return pl.pallas_call(
        _top_blocks_body,
        grid=(1,),
        in_specs=[_const_spec((n, nselp))],
        out_specs=_const_spec((n, LANES)),
        out_shape=jax.ShapeDtypeStruct((n, LANES), jnp.int32),
        compiler_params=_params("arbitrary"),
        name="top_blocks",
    )(score)


def _nsa_sample_attend_body(idx_ref, pt_ref, qrot_ref, gates_ref, oc_ref, selnew_ref, winnew_ref, wincache_ref,
                            *refs, q_pos, n_past_blocks):
    del pt_ref
    page_refs, o_ref = refs[:-1], refs[-1]
    b = pl.program_id(0)
    wb = wincache_ref.shape[2]
    qrow = qrot_ref[...] * SCALE
    gates = gates_ref[...]
    oc = oc_ref[...]
    sel_new = selnew_ref[...]
    win_new = winnew_ref[...]
    page_lane = lax.broadcasted_iota(jnp.int32, (1, PAGE_SIZE), 1)
    wpos = (q_pos - wb) + lax.broadcasted_iota(jnp.int32, (1, wb), 1)
    m_w = (q_pos - wpos <= WINDOW) & (wpos >= 0) & (wpos <= q_pos)
    cur = q_pos // SEL_BLOCK
    out_parts = []
    for g in range(N_KV):
        q = _head_rows(qrow, g)
        feat = slice(g * HEAD_DIM, (g + 1) * HEAD_DIM)
        kcol = slice(g * HEAD_DIM, (g + 1) * HEAD_DIM)
        vcol = slice(KV_W + g * HEAD_DIM, KV_W + (g + 1) * HEAD_DIM)
        kts, vts, kps = [], [], []
        new_chosen = None
        for n in range(N_SEL):
            j = idx_ref[b, g * N_SEL + n]
            page = page_refs[g * N_SEL + n]
            kts.append(page[0, feat, :])
            vts.append(page[1, feat, :])
            in_block = (page_lane // SEL_BLOCK == j % BLOCKS_PER_PAGE) & (j < n_past_blocks)
            kps.append(jnp.where(in_block, (j // BLOCKS_PER_PAGE) * PAGE_SIZE + page_lane, q_pos + 1))
            hit = j == cur
            new_chosen = hit if new_chosen is None else (new_chosen | hit)
        s_s = _dot(q, jnp.concatenate(kts, axis=1))
        s_new = jnp.where(new_chosen, jnp.sum(q * sel_new[:, kcol], axis=1, keepdims=True), NEG_INF)
        s_s = jnp.where(jnp.concatenate(kps, axis=1) <= q_pos, s_s, NEG_INF)
        m = jnp.maximum(jnp.max(s_s, axis=1, keepdims=True), s_new)
        m = jnp.where(m > NEG_INF, m, 0.0)
        e_s = jnp.exp(s_s - m)
        e_new = jnp.exp(s_new - m)
        d_s = jnp.sum(e_s, axis=1, keepdims=True) + e_new
        d_s = jnp.where(d_s > 0, d_s, 1.0)
        o_s = (_dot_nt(e_s, jnp.concatenate(vts, axis=1)) + e_new * sel_new[:, vcol]) / d_s
        s_w = jnp.where(m_w, _dot(q, wincache_ref[0, feat, :]), NEG_INF)
        s_wn = jnp.sum(q * win_new[:, kcol], axis=1, keepdims=True)
        m = jnp.maximum(jnp.max(s_w, axis=1, keepdims=True), s_wn)
        e_w = jnp.exp(s_w - m)
        e_wn = jnp.exp(s_wn - m)
        d_w = jnp.sum(e_w, axis=1, keepdims=True) + e_wn
        o_w = (_dot_nt(e_w, wincache_ref[1, feat, :]) + e_wn * win_new[:, vcol]) / d_w
        for h in range(HPG):
            hd = g * HPG + h
            out_parts.append(gates[:, 3 * hd:3 * hd + 1] * oc[:, hd * HEAD_DIM:(hd + 1) * HEAD_DIM]
                             + gates[:, 3 * hd + 1:3 * hd + 2] * o_s[h:h + 1]
                             + gates[:, 3 * hd + 2:3 * hd + 3] * o_w[h:h + 1])
    o_ref[...] = jnp.concatenate(out_parts, axis=1)


def _nsa_sample_attend(idx, page_table, q_rot, gates, o_c, sel_new, win_new, win_cache_t, sel_cache_t,
                       *, q_pos, n_past_blocks):
    b = q_rot.shape[0]
    wb = win_cache_t.shape[3]

    def page_spec(k):
        def index(i, idx_ref, pt_ref):
            j = jnp.minimum(idx_ref[i, k], n_past_blocks - 1)
            return (pt_ref[i, j // BLOCKS_PER_PAGE], 0, 0, 0)
        return pl.BlockSpec((None, 2, KV_W, PAGE_SIZE), index)

    one = lambda w: pl.BlockSpec((None, 1, w), lambda i, idx_ref, pt_ref: (i, 0, 0))
    return pl.pallas_call(
        functools.partial(_nsa_sample_attend_body, q_pos=q_pos, n_past_blocks=n_past_blocks),
        grid_spec=pltpu.PrefetchScalarGridSpec(
            num_scalar_prefetch=2,
            grid=(b,),
            in_specs=[one(Q_W), one(GATE_PAD), one(Q_W), one(ROW_W), one(ROW_W),
                      pl.BlockSpec((None, 2, KV_W, wb), lambda i, idx_ref, pt_ref: (i, 0, 0, 0))]
            + [page_spec(k) for k in range(N_KV * N_SEL)],
            out_specs=one(Q_W),
        ),
        out_shape=jax.ShapeDtypeStruct((b, 1, Q_W), F32),
        compiler_params=_params("arbitrary"),
        name="nsa_sample_attend",
    )(idx, page_table, q_rot, gates, o_c, sel_new, win_new, win_cache_t, *([sel_cache_t] * (N_KV * N_SEL)))


def _rope_tables(pos):
    half = HEAD_DIM // 2
    freqs = ROPE_THETA ** (-jnp.arange(half, dtype=F32) / half)
    ang = pos.astype(F32)[:, None] * freqs[None, :]
    cos, sin = jnp.cos(ang), jnp.sin(ang)
    reps = LANES // HEAD_DIM
    return jnp.tile(jnp.concatenate([cos, cos], axis=1), (1, reps)), jnp.tile(jnp.concatenate([-sin, sin], axis=1), (1, reps))


def _cover(nc, nsel, rows, cols):
    start = np.arange(rows)[:, None] * CMP_STRIDE
    j = np.arange(cols)[None, :]
    hit = (start < (j + 1) * SEL_BLOCK) & (start + CMP_BLOCK > j * SEL_BLOCK) & (np.arange(rows)[:, None] < nc) & (j < nsel)
    return jnp.asarray(hit.astype(np.float32))


def _cmp_chunk_weight(w1_k, w1_v):
    per = jnp.stack([w.reshape(CMP_RATIO, CMP_STRIDE, HEAD_DIM, CMP_HID) for w in (w1_k, w1_v)])
    eye = jnp.eye(N_KV, dtype=F32)
    big = jnp.einsum("ab,krsdh->ksadbrh", eye, per)
    return big.reshape(2, CMP_STRIDE, KV_W, N_KV * CMP_RATIO * CMP_HID).astype(BF16)


def _rg_gate_weight(w_a, w_x):
    nb, bw, _ = w_a.shape
    per = MXU_DIM // bw
    n_grp = nb // per
    eye = jnp.eye(per, dtype=F32)

    def group(w):
        w = w.reshape(n_grp, per, bw, bw)
        return jnp.einsum("pq,gpde->gpdqe", eye, w).reshape(n_grp, per * bw, per * bw)

    return jnp.concatenate([group(w_a), group(w_x)], axis=2).astype(BF16)


def _feature_major(kv_rows):
    lead = kv_rows.shape[:-4]
    rows = kv_rows.shape[-4]
    nd = len(lead)
    perm = tuple(range(nd)) + (nd + 1, nd + 2, nd + 3, nd)
    return jnp.transpose(kv_rows, perm).reshape(lead + (2, KV_W, rows))


def _row_major_view(kv_t, b, rows):
    return jnp.transpose(kv_t.reshape(b, 2, N_KV, HEAD_DIM, rows), (0, 4, 1, 2, 3))


def kernel(x_prompt, x_sample, cache_cmp_kv, cache_sel_kv, cache_win_kv, state_conv, state_h, page_table, ffn1_w_gate, ffn1_w_up, ffn1_w_down, ln1_g, ln1_b, w_in, cmp_pe_k, cmp_w1_k, cmp_w2_k, cmp_pe_v, cmp_w1_v, cmp_w2_v, conv_w, conv_b, rg_w_a, rg_b_a, rg_w_x, rg_b_x, rg_lam, w_br_attn, w_br_rnn, w_out, ln2_g, ln2_b, ffn2_w_gate, ffn2_w_up, ffn2_w_down, ln3_g, ln3_b):
    bp, tp, d = x_prompt.shape
    bs, ts, _ = x_sample.shape
    depth = w_in.shape[0]
    d_rnn = conv_w.shape[2]
    n_pages = page_table.shape[1]
    past_len = n_pages * PAGE_SIZE
    assert ts == 1 and tp % CMP_STRIDE == 0 and tp % SEL_BLOCK == 0
    alpha = (2.0 * depth) ** 0.25
    wb = cache_win_kv.shape[2]
    wbp = min(WINDOW, tp)

    nch_p = tp // CMP_STRIDE
    nc_p = nch_p - CMP_RATIO + 1
    nsel_p = tp // SEL_BLOCK
    nch_s = past_len // CMP_STRIDE
    nc_s = nch_s - CMP_RATIO + 1
    nsel_s = -(-(past_len + ts) // SEL_BLOCK)
    n_past_blocks = past_len // SEL_BLOCK
    assert nsel_s >= N_SEL and n_pages % PAGES_PER_STEP == 0
    nselp_s = -(-nsel_s // LANES) * LANES

    tm = min(256, tp)
    tq = min(512, tp)
    tt = min(256, tp)

    cos_p, sin_p = _rope_tables(jnp.arange(tp))
    cos_s, sin_s = _rope_tables(jnp.full((bs,), past_len))
    cover_p = _cover(nc_p, nsel_p, nch_p, nsel_p).T
    cover_s = _cover(nc_s, nsel_s, nch_s, nselp_s)
    expand_p = jnp.asarray((np.arange(tp)[:, None] // SEL_BLOCK == np.arange(nsel_p)[None, :]).astype(np.float32)).astype(BF16)

    xp = x_prompt.reshape(bp * tp, d)
    xs = x_sample.reshape(bs * ts, d)
    outs = [[] for _ in range(10)]
    for l in range(depth):
        row = lambda v: v[l].reshape(1, -1)
        ffn1 = (ffn1_w_gate[l].astype(BF16), ffn1_w_up[l].astype(BF16), ffn1_w_down[l].astype(BF16), row(ln1_g), row(ln1_b))
        ffn2 = (ffn2_w_gate[l].astype(BF16), ffn2_w_up[l].astype(BF16), ffn2_w_down[l].astype(BF16), row(ln3_g), row(ln3_b))
        n_a = Q_W + 6 * KV_W + 3 * N_HEADS
        w_a = jnp.pad(w_in[l][:, :n_a], ((0, 0), (0, GATE_PAD - 3 * N_HEADS))).astype(BF16)
        w_b = w_in[l][:, n_a:].astype(BF16)
        w_cmp = _cmp_chunk_weight(cmp_w1_k[l], cmp_w1_v[l])
        cmpw = (cmp_pe_k[l].reshape(1, -1), cmp_w1_k[l], cmp_w2_k[l], cmp_pe_v[l].reshape(1, -1), cmp_w1_v[l], cmp_w2_v[l])
        rgw = (jnp.pad(conv_w[l], ((0, SUBLANES - CONV_W), (0, 0))), row(conv_b), _rg_gate_weight(rg_w_a[l], rg_w_x[l]),
               row(rg_b_a), row(rg_b_x), row(rg_lam))
        mergew = (w_br_attn[l].astype(BF16), w_br_rnn[l].astype(BF16), w_out[l].astype(BF16))
        ln2 = (row(ln2_g), row(ln2_b))

        h1 = _ffn_half_step(xp, *ffn1, alpha=alpha, tm=tm)
        (cmp, sel_k, win_k, rg_a, rg_u, gr, ga, gb, xr_tail, q_raw_t, q_rot_t, gates_t, cmp_t, sel_t, win_t) = _in_proj(
            h1, cos_p, sin_p, w_a, w_b, rgw, tm=tm, seq_len=tp, feature_major=True)
        b3 = lambda a: a.reshape(bp, tp, a.shape[-1])
        p1 = _cmp_stage1(b3(cmp), w_cmp)
        ckv, ckv_t = _cmp_finish(p1, *cmpw, nc=nc_p)
        o_attn = _nsa_prompt(q_raw_t, q_rot_t, gates_t, ckv, ckv_t, b3(sel_k), sel_t, b3(win_k), win_t,
                             cover_p, expand_p, tq=tq, nc=nc_p)
        y2, h_last = _mix_prompt(b3(rg_a), b3(rg_u), b3(gr), b3(ga), b3(gb), o_attn, b3(h1), mergew, tt=tt, alpha=alpha)
        xp = _ffn_half_step(y2.reshape(bp * tp, d), *ffn2, alpha=alpha, tm=tm, norm_in=ln2)
        outs[0].append(_row_major_view(cmp_t, bp, tp))
        outs[2].append(_row_major_view(sel_t, bp, tp))
        outs[4].append(_row_major_view(win_t[:, :, tp - wbp:], bp, wbp))
        outs[6].append(xr_tail[:, SUBLANES - (CONV_W - 1):])
        outs[8].append(h_last.reshape(bp, d_rnn))

        h1s = _ffn_half_step(xs, *ffn1, alpha=alpha, tm=bs)
        q_raw, q_rot, cmp, sel, win, gates, xr, gr, ga, gb = _in_proj(
            h1s, cos_s, sin_s, w_a, w_b, tm=bs, seq_len=bs, feature_major=False)
        p1 = _cmp_stage1_paged(_feature_major(cache_cmp_kv[l]), page_table, w_cmp)
        ckv, _ = _cmp_finish(p1, *cmpw, nc=nc_s)
        b1 = lambda a: a.reshape(bs, 1, a.shape[-1])
        q_pos = past_len
        o_c, score = _nsa_sample_scores(b1(q_raw), ckv, cover_s, nc=nc_s, nsel=nsel_s, q_pos=q_pos)
        idx = _top_blocks(score.reshape(bs * SUBLANES, nselp_s))
        idx = idx.reshape(bs, SUBLANES, LANES)[:, :N_KV, :N_SEL].reshape(bs, N_KV * N_SEL)
        o_attn = _nsa_sample_attend(idx, page_table, b1(q_rot), b1(gates), o_c, b1(sel), b1(win),
                                    _feature_major(cache_win_kv[l]), _feature_major(cache_sel_kv[l]),
                                    q_pos=q_pos, n_past_blocks=n_past_blocks)
        conv_rows = [state_conv[l][:, k] for k in range(CONV_W - 1)]
        y2s, h_new = _mix_sample(xr, conv_rows, state_h[l], gr, ga, gb, o_attn.reshape(bs, Q_W), h1s, rgw, mergew,
                                 alpha=alpha)
        xs = _ffn_half_step(y2s, *ffn2, alpha=alpha, tm=bs, norm_in=ln2)
        outs[1].append(cmp.reshape(bs, ts, 2, N_KV, HEAD_DIM))
        outs[3].append(sel.reshape(bs, ts, 2, N_KV, HEAD_DIM))
        win_all = jnp.concatenate([cache_win_kv[l], win.reshape(bs, ts, 2, N_KV, HEAD_DIM)], axis=1)
        outs[5].append(win_all[:, ts:])
        outs[7].append(jnp.concatenate([state_conv[l], xr[:, None, :]], axis=1)[:, ts:])
        outs[9].append(h_new)

    stacked = [jnp.stack(o) for o in outs]
    cmp_p, cmp_s, sel_p, sel_s, win_p, win_s, conv_p, conv_s, h_p, h_s = stacked
    return (xp.reshape(bp, tp, d), xs.reshape(bs, ts, d), cmp_p, cmp_s, sel_p, sel_s, win_p, win_s,
            conv_p, conv_s, h_p, h_s)
```

```python
import functools
import math

import numpy as np
import jax
import jax.numpy as jnp
from jax import lax
from jax.experimental import pallas as pl
from jax.experimental.pallas import tpu as pltpu

F32 = jnp.float32
BF16 = jnp.bfloat16

N_HEADS = 8
HEAD_DIM = 64
N_KV = 2
HPG = N_HEADS // N_KV
Q_W = N_HEADS * HEAD_DIM
KV_W = N_KV * HEAD_DIM
CMP_BLOCK = 32
CMP_STRIDE = 16
CMP_RATIO = CMP_BLOCK // CMP_STRIDE
CMP_HID = 64
SEL_BLOCK = 64
N_SEL = 16
SEL_BONUS = 1.0e4
WINDOW = 512
ROPE_THETA = 10000.0
RG_BLOCKS = 16
RG_C = 8.0
CONV_W = 4
PAGE_SIZE = 128
LN_EPS = 1e-5
SCALE = HEAD_DIM ** -0.5
QK_SCALE_LOG2 = SCALE * math.log2(math.e)
NEG_INF = float("-inf")

LANES = 128
SUBLANES = 8
MXU_DIM = 256
VMEM_LIMIT_BYTES = 56 * 1024 * 1024

ROW_W = 2 * KV_W
P_W = 2 * N_KV * CMP_RATIO * CMP_HID
GATE_PAD = LANES
PAGES_PER_STEP = 32
BLOCKS_PER_PAGE = PAGE_SIZE // SEL_BLOCK


def _dot(a, b):
    return jnp.dot(a.astype(BF16), b.astype(BF16), preferred_element_type=F32)


def _dot_nt(a, b):
    return lax.dot_general(a.astype(BF16), b.astype(BF16), (((1,), (1,)), ((), ())),
                           preferred_element_type=F32)


def _dot_f32(a, b):
    return jnp.dot(a, b, preferred_element_type=F32, precision=lax.Precision.HIGHEST)


def _sigmoid(x):
    return 0.5 * jnp.tanh(0.5 * x) + 0.5


def _gelu(x):
    return 0.5 * x * (1.0 + jnp.tanh(math.sqrt(2.0 / math.pi) * (x + 0.044715 * (x * x * x))))


def _layer_norm(y, g, b):
    mu = jnp.mean(y, axis=-1, keepdims=True)
    yc = y - mu
    var = jnp.mean(yc * yc, axis=-1, keepdims=True)
    return yc * lax.rsqrt(var + LN_EPS) * g + b


def _softmax_parts(s, mask):
    s = jnp.where(mask, s, NEG_INF)
    m = jnp.max(s, axis=-1, keepdims=True)
    m = jnp.where(m > NEG_INF, m, 0.0)
    e = jnp.exp(s - m)
    d = jnp.sum(e, axis=-1, keepdims=True)
    return e, jnp.where(d > 0, d, 1.0)


def _const_spec(shape):
    nd = len(shape)
    return pl.BlockSpec(shape, lambda *_: (0,) * nd)


def _params(*sem):
    return pltpu.CompilerParams(dimension_semantics=sem, vmem_limit_bytes=VMEM_LIMIT_BYTES)


def _ffn_body(x_ref, wg_ref, wu_ref, wd_ref, g_ref, b_ref, o_ref, *, alpha):
    x = x_ref[...]
    xb = x.astype(BF16)
    gate = jnp.dot(xb, wg_ref[...], preferred_element_type=F32)
    up = jnp.dot(xb, wu_ref[...], preferred_element_type=F32)
    hmid = gate * _sigmoid(gate) * up
    f = _dot(hmid, wd_ref[...])
    o_ref[...] = _layer_norm(alpha * x + 0.5 * f, g_ref[...], b_ref[...])


def _ffn_half_step(x, wg, wu, wd, g, b, *, alpha, tm):
    n, d = x.shape
    ff = wg.shape[1]
    return pl.pallas_call(
        functools.partial(_ffn_body, alpha=alpha),
        grid=(n // tm,),
        in_specs=[pl.BlockSpec((tm, d), lambda i: (i, 0)),
                  _const_spec((d, ff)), _const_spec((d, ff)), _const_spec((ff, d)),
                  _const_spec((1, d)), _const_spec((1, d))],
        out_specs=pl.BlockSpec((tm, d), lambda i: (i, 0)),
        out_shape=jax.ShapeDtypeStruct((n, d), F32),
        compiler_params=_params("parallel"),
        name="ffn_half_step",
    )(x, wg, wu, wd, g, b)


def _rope(x, cos, sin):
    w = x.shape[1]
    reps = w // LANES
    c = jnp.tile(cos, (1, reps)) if reps > 1 else cos
    s = jnp.tile(sin, (1, reps)) if reps > 1 else sin
    lane = lax.broadcasted_iota(jnp.int32, x.shape, 1)
    first_half = (lane % HEAD_DIM) < (HEAD_DIM // 2)
    swapped = jnp.where(first_half, pltpu.roll(x, w - HEAD_DIM // 2, 1), pltpu.roll(x, HEAD_DIM // 2, 1))
    return x * c + swapped * s


def _inproj_body(h_ref, cos_ref, sin_ref, wa_ref, wb_ref, *out_refs, feature_major):
    hb = h_ref[...].astype(BF16)
    cos = cos_ref[...]
    sin = sin_ref[...]
    za = jnp.dot(hb, wa_ref[...], preferred_element_type=F32)
    q = za[:, :Q_W]
    kv = za[:, Q_W:Q_W + 6 * KV_W]
    gates = _sigmoid(za[:, Q_W + 6 * KV_W:])
    q_rot = _rope(q, cos, sin)
    cmp = kv[:, :2 * KV_W]
    sel_k = _rope(kv[:, 2 * KV_W:3 * KV_W], cos, sin)
    win_k = _rope(kv[:, 4 * KV_W:5 * KV_W], cos, sin)
    sel = jnp.concatenate([sel_k, kv[:, 3 * KV_W:4 * KV_W]], axis=1)
    win = jnp.concatenate([win_k, kv[:, 5 * KV_W:6 * KV_W]], axis=1)
    if feature_major:
        (cmp_ref, selk_ref, wink_ref, xr_ref, gr_ref, ga_ref, gb_ref,
         qrawt_ref, qrott_ref, gatest_ref, cmpt_ref, selt_ref, wint_ref) = out_refs
        cmp_ref[...] = cmp
        selk_ref[...] = sel_k.astype(BF16)
        wink_ref[...] = win_k.astype(BF16)
        qrawt_ref[...] = q.T
        qrott_ref[...] = q_rot.T
        gatest_ref[...] = gates.T
        cmpt_ref[...] = cmp.T
        selt_ref[...] = sel.T
        wint_ref[...] = win.T
    else:
        qraw_ref, qrot_ref, cmp_ref, sel_ref, win_ref, gates_ref, xr_ref, gr_ref, ga_ref, gb_ref = out_refs
        qraw_ref[...] = q
        qrot_ref[...] = q_rot
        cmp_ref[...] = cmp
        sel_ref[...] = sel
        win_ref[...] = win
        gates_ref[...] = gates
    zb = jnp.dot(hb, wb_ref[...], preferred_element_type=F32)
    d = xr_ref.shape[1]
    xr_ref[...] = zb[:, :d]
    gr_ref[...] = zb[:, d:2 * d]
    ga_ref[...] = zb[:, 2 * d:3 * d]
    gb_ref[...] = zb[:, 3 * d:]


def _in_proj(h, cos, sin, wa, wb, *, tm, seq_len, feature_major):
    n, d = h.shape
    wa_w, wb_w = wa.shape[1], wb.shape[1]
    tiles = seq_len // tm
    row = lambda i: (i, 0)
    tab = lambda i: (i % tiles, 0)
    if feature_major:
        rows = [(ROW_W, F32), (KV_W, BF16), (KV_W, BF16), (d, F32), (d, F32), (d, F32), (d, F32)]
        cols = [Q_W, Q_W, GATE_PAD, ROW_W, ROW_W, ROW_W]
    else:
        rows = [(w, F32) for w in (Q_W, Q_W, ROW_W, ROW_W, ROW_W, GATE_PAD, d, d, d, d)]
        cols = []
    out_specs = [pl.BlockSpec((tm, w), row) for w, _ in rows]
    out_shape = [jax.ShapeDtypeStruct((n, w), dt) for w, dt in rows]
    out_specs += [pl.BlockSpec((None, w, tm), lambda i: (i // tiles, 0, i % tiles)) for w in cols]
    out_shape += [jax.ShapeDtypeStruct((n // seq_len, w, seq_len), F32) for w in cols]
    return pl.pallas_call(
        functools.partial(_inproj_body, feature_major=feature_major),
        grid=(n // tm,),
        in_specs=[pl.BlockSpec((tm, d), row), pl.BlockSpec((tm, LANES), tab), pl.BlockSpec((tm, LANES), tab),
                  _const_spec((d, wa_w)), _const_spec((d, wb_w))],
        out_specs=out_specs,
        out_shape=out_shape,
        compiler_params=_params("parallel"),
        name="in_proj",
    )(h, cos, sin, wa, wb)


def _cmp_chunks(rows_of, w_ref, kv, nch):
    acc = None
    for s in range(CMP_STRIDE):
        part = _dot(rows_of(s, nch), w_ref[kv, s])
        acc = part if acc is None else acc + part
    return acc


def _cmp_stage1_body(x_ref, w_ref, o_ref, rows_ref):
    nch = x_ref.shape[0] // CMP_STRIDE
    half = P_W // 2
    for kv in range(2):
        rows_ref[kv] = x_ref[:, kv * KV_W:(kv + 1) * KV_W]
        rows_of = lambda s, n, kv=kv: rows_ref[kv, pl.ds(s, n, stride=CMP_STRIDE), :]
        o_ref[:, kv * half:(kv + 1) * half] = _cmp_chunks(rows_of, w_ref, kv, nch)


def _cmp_stage1(x, w):
    b, t, _ = x.shape
    nch = t // CMP_STRIDE
    return pl.pallas_call(
        _cmp_stage1_body,
        grid=(b,),
        in_specs=[pl.BlockSpec((None, t, ROW_W), lambda i: (i, 0, 0)), _const_spec(w.shape)],
        out_specs=pl.BlockSpec((None, nch, P_W), lambda i: (i, 0, 0)),
        out_shape=jax.ShapeDtypeStruct((b, nch, P_W), F32),
        scratch_shapes=[pltpu.VMEM((2, t, KV_W), F32)],
        compiler_params=_params("parallel"),
        name="cmp_stage1",
    )(x, w)


def _cmp_sample_body(pt_ref, *refs, nc, nsel, q_pos):
    del pt_ref
    x_refs = refs[:PAGES_PER_STEP]
    (perm_ref, w_ref, pek_ref, w1k_ref, w2k_ref, pev_ref, w1v_ref, w2v_ref, qraw_ref, cover_ref,
     oc_ref, score_ref, planes_ref, stage_ref, p_ref) = refs[PAGES_PER_STEP:]
    step = pl.program_id(1)
    cpp = PAGE_SIZE // CMP_STRIDE
    nch_step = PAGES_PER_STEP * cpp
    perm = perm_ref[...]
    for k, x_ref in enumerate(x_refs):
        for kv in range(2):
            stage_ref[k, kv] = jnp.dot(x_ref[kv].astype(BF16), perm, preferred_element_type=F32)
            rows = stage_ref[k, kv].T
            for s in range(CMP_STRIDE):
                planes_ref[kv, s, k * cpp:(k + 1) * cpp, :] = rows[s * cpp:(s + 1) * cpp]
    half = P_W // 2
    row0 = pl.multiple_of(step * nch_step, nch_step)
    for kv in range(2):
        acc = None
        for j in range(CMP_STRIDE // 2):
            pair = jnp.concatenate([planes_ref[kv, 2 * j], planes_ref[kv, 2 * j + 1]], axis=1)
            part = _dot(pair, w_ref[kv, j])
            acc = part if acc is None else acc + part
        p_ref[pl.ds(row0, nch_step), kv * half:(kv + 1) * half] = acc

    @pl.when(step == pl.num_programs(1) - 1)
    def _():
        ckv = _cmp_second_layer(p_ref[...], pek_ref, w1k_ref, w2k_ref, pev_ref, w1v_ref, w2v_ref, nc)
        oc_ref[...], score_ref[...] = _sample_scores(qraw_ref[...], ckv, cover_ref, nc, nsel, q_pos)


def _cmp_sample(cache_t, page_table, w, cmpw, q_raw, cover, *, nc, nsel, q_pos):
    b, n_pages = page_table.shape
    steps = n_pages // PAGES_PER_STEP
    cpp = PAGE_SIZE // CMP_STRIDE
    nch = n_pages * cpp
    nselp = cover.shape[1]
    w = w.reshape(2, CMP_STRIDE // 2, 2 * KV_W, w.shape[-1])
    row = np.arange(PAGE_SIZE)
    perm = np.zeros((PAGE_SIZE, PAGE_SIZE), np.float32)
    perm[row, (row % CMP_STRIDE) * cpp + row // CMP_STRIDE] = 1.0
    perm = jnp.asarray(perm).astype(BF16)

    def page_spec(k):
        return pl.BlockSpec((None, 2, KV_W, PAGE_SIZE), lambda i, s, pt: (pt[i, s * PAGES_PER_STEP + k], 0, 0, 0))

    const = lambda a: pl.BlockSpec(a.shape, lambda i, s, pt: (0,) * a.ndim)
    per_request = lambda r, c: pl.BlockSpec((None, r, c), lambda i, s, pt: (i, 0, 0))
    return pl.pallas_call(
        functools.partial(_cmp_sample_body, nc=nc, nsel=nsel, q_pos=q_pos),
        grid_spec=pltpu.PrefetchScalarGridSpec(
            num_scalar_prefetch=1,
            grid=(b, steps),
            in_specs=[page_spec(k) for k in range(PAGES_PER_STEP)]
            + [const(perm), const(w)] + [const(a) for a in cmpw] + [per_request(1, Q_W), const(cover)],
            out_specs=[per_request(1, Q_W), per_request(SUBLANES, nselp)],
            scratch_shapes=[pltpu.VMEM((2, CMP_STRIDE, PAGES_PER_STEP * cpp, KV_W), F32),
                            pltpu.VMEM((PAGES_PER_STEP, 2, KV_W, PAGE_SIZE), F32),
                            pltpu.VMEM((nch, P_W), F32)],
        ),
        out_shape=[jax.ShapeDtypeStruct((b, 1, Q_W), F32), jax.ShapeDtypeStruct((b, SUBLANES, nselp), F32)],
        compiler_params=_params("parallel", "arbitrary"),
        name="cmp_sample",
    )(page_table, *([cache_t] * PAGES_PER_STEP), perm, w, *cmpw, q_raw, cover)


def _cmp_second_layer(p, pek_ref, w1k_ref, w2k_ref, pev_ref, w1v_ref, w2v_ref, nc):
    nch = p.shape[0]
    row = lax.broadcasted_iota(jnp.int32, (nch, CMP_HID), 0)
    outs = []
    for kv, (pe_ref, w1_ref, w2_ref) in enumerate(((pek_ref, w1k_ref, w2k_ref), (pev_ref, w1v_ref, w2v_ref))):
        pe = jnp.broadcast_to(pe_ref[...], (SUBLANES, pe_ref.shape[1]))
        bias = _dot(pe, w1_ref[...])[0:1]
        for g in range(N_KV):
            c0 = (kv * N_KV + g) * CMP_RATIO * CMP_HID
            u = bias + p[:, c0:c0 + CMP_HID]
            for r in range(1, CMP_RATIO):
                u = u + pltpu.roll(p[:, c0 + r * CMP_HID:c0 + (r + 1) * CMP_HID], nch - r, 0)
            out = _dot(_gelu(u), w2_ref[...])
            outs.append(jnp.where(row < nc, out, 0.0))
    return jnp.concatenate(outs, axis=1)


def _cmp_finish_body(p_ref, pek_ref, w1k_ref, w2k_ref, pev_ref, w1v_ref, w2v_ref, o_ref, ot_ref, *, nc):
    ckv = _cmp_second_layer(p_ref[...], pek_ref, w1k_ref, w2k_ref, pev_ref, w1v_ref, w2v_ref, nc)
    o_ref[...] = ckv
    ot_ref[...] = ckv.T


def _cmp_finish(p, pe_k, w1_k, w2_k, pe_v, w1_v, w2_v, *, nc):
    b, nch, _ = p.shape
    flat = CMP_BLOCK * HEAD_DIM
    return pl.pallas_call(
        functools.partial(_cmp_finish_body, nc=nc),
        grid=(b,),
        in_specs=[pl.BlockSpec((None, nch, P_W), lambda i: (i, 0, 0)),
                  _const_spec((1, flat)), _const_spec((flat, CMP_HID)), _const_spec((CMP_HID, HEAD_DIM)),
                  _const_spec((1, flat)), _const_spec((flat, CMP_HID)), _const_spec((CMP_HID, HEAD_DIM))],
        out_specs=[pl.BlockSpec((None, nch, ROW_W), lambda i: (i, 0, 0)),
                   pl.BlockSpec((None, ROW_W, nch), lambda i: (i, 0, 0))],
        out_shape=[jax.ShapeDtypeStruct((b, nch, ROW_W), F32), jax.ShapeDtypeStruct((b, ROW_W, nch), F32)],
        compiler_params=_params("parallel"),
        name="cmp_finish",
    )(p, pe_k, w1_k, w2_k, pe_v, w1_v, w2_v)


def _block_scores(imp, q_pos, nsel_valid, block_axis=1):
    jblk = lax.broadcasted_iota(jnp.int32, imp.shape, block_axis)
    cur = q_pos // SEL_BLOCK
    valid = (jblk * SEL_BLOCK <= q_pos) & (jblk < nsel_valid)
    forced = (jblk == 0) | (jblk == cur) | (jblk == cur - 1)
    return jnp.where(valid, imp + jnp.where(forced, SEL_BONUS, 0.0), NEG_INF)


def _nsa_prompt_body(qrawt_ref, qrott_ref, gatest_ref, ckv_ref, ckvt_ref, selk_ref, selt_ref, wink_ref, wint_ref,
                     covert_ref, expandt_ref, o_ref, *, tq, nc):
    nchp = ckv_ref.shape[0]
    nsel = covert_ref.shape[0]
    tile = pl.program_id(1)
    s0 = tile * tq
    lane_t = lax.broadcasted_iota(jnp.int32, (1, tq), 1)
    key_sub = lax.broadcasted_iota(jnp.int32, (tq, 1), 0)
    qpos = s0 + lane_t
    causal_diag = key_sub <= lane_t
    window_edge = lane_t <= key_sub
    cidx = lax.broadcasted_iota(jnp.int32, (nchp, 1), 0)
    jblk = lax.broadcasted_iota(jnp.int32, (nsel, 1), 0)
    n_win_chunks = WINDOW // tq
    vrows = [slice(KV_W + g * HEAD_DIM, KV_W + (g + 1) * HEAD_DIM) for g in range(N_KV)]
    group_of = lambda hd: hd // HPG

    def group_rows(qt, g):
        z = jnp.zeros_like(qt)
        return jnp.concatenate([qt, z] if g == 0 else [z, qt], axis=0)

    def head_rows(ref, hd):
        return group_rows((ref[hd * HEAD_DIM:(hd + 1) * HEAD_DIM, :] * QK_SCALE_LOG2).astype(BF16), group_of(hd))

    q_rot = [head_rows(qrott_ref, hd) for hd in range(N_HEADS)]
    q_raw = jnp.concatenate([head_rows(qrawt_ref, hd) for hd in range(N_HEADS)], axis=1)
    s_c = jnp.dot(ckv_ref[:, :KV_W].astype(BF16), q_raw, preferred_element_type=F32)
    m_c = (cidx * CMP_STRIDE + (CMP_BLOCK - 1) <= jnp.concatenate([qpos] * N_HEADS, axis=1)) & (cidx < nc)
    s_c = jnp.where(m_c, s_c, NEG_INF)
    mx = jnp.max(s_c, axis=0, keepdims=True)
    e_c = jnp.exp2(s_c - jnp.where(mx > NEG_INF, mx, 0.0))
    d_c = jnp.sum(e_c, axis=0, keepdims=True)
    p_c = e_c * (1.0 / jnp.where(d_c > 0, d_c, 1.0))
    head_cols = lambda hd: slice(hd * tq, (hd + 1) * tq)
    o_c = [_dot(ckvt_ref[vrows[group_of(hd)], :], p_c[:, head_cols(hd)]) for hd in range(N_HEADS)]
    p_sum = []
    for g in range(N_KV):
        acc = p_c[:, head_cols(g * HPG)]
        for hd in range(g * HPG + 1, (g + 1) * HPG):
            acc = acc + p_c[:, head_cols(hd)]
        p_sum.append(acc)
    imp = _dot_f32(covert_ref[...], jnp.concatenate(p_sum, axis=1))
    score = _block_scores(imp, jnp.concatenate([qpos] * N_KV, axis=1), nsel, block_axis=0)
    rank = jnp.zeros(score.shape, jnp.int32)
    for i in range(nsel):
        s_i = score[i:i + 1, :]
        beats = (s_i > score) | ((s_i == score) & (i < jblk))
        rank = rank + beats.astype(jnp.int32)
    chosen = (rank < N_SEL).astype(BF16)

    def picked(k0):
        hit = jnp.dot(expandt_ref[pl.ds(k0, tq), :], chosen, preferred_element_type=F32)
        return [hit[:, g * tq:(g + 1) * tq] > 0.5 for g in range(N_KV)]

    def attend(jobs):
        loaded = []
        for k_ref, vt_ref, kc, masks_of, _ in jobs:
            k0 = pl.multiple_of(kc * tq, tq)
            keys = k_ref[pl.ds(k0, tq), :]
            values_t = [vt_ref[rows, pl.ds(k0, tq)].astype(BF16) for rows in vrows]
            loaded.append((keys, values_t, masks_of(k0)))
        scores = [[jnp.dot(keys, q_rot[hd], preferred_element_type=F32) for hd in range(N_HEADS)]
                  for keys, _, _ in loaded]
        updates = []
        for job, (_, _, masks), sc in zip(jobs, loaded, scores):
            carry = job[4]
            stats, probs = [], []
            for hd in range(N_HEADS):
                m, l, _ = carry[hd]
                s = sc[hd] if masks is None else jnp.where(masks[group_of(hd)], sc[hd], NEG_INF)
                m_new = jnp.maximum(m, jnp.max(s, axis=0, keepdims=True))
                m_safe = m_new if masks is None else jnp.where(m_new > NEG_INF, m_new, 0.0)
                alpha = jnp.exp2(m - m_safe)
                p = jnp.exp2(s - m_safe)
                stats.append((m_new, l * alpha + jnp.sum(p, axis=0, keepdims=True), alpha))
                probs.append(p.astype(BF16))
            updates.append((stats, probs))
        out = []
        for job, (_, values_t, _), (stats, probs) in zip(jobs, loaded, updates):
            carry = job[4]
            pv = [jnp.dot(values_t[group_of(hd)], probs[hd], preferred_element_type=F32) for hd in range(N_HEADS)]
            out.append(tuple((stats[hd][0], stats[hd][1], carry[hd][2] * stats[hd][2] + pv[hd])
                             for hd in range(N_HEADS)))
        return out

    def finish(carry):
        return [acc * (1.0 / jnp.where(l > 0, l, 1.0)) for _, l, acc in carry]

    init = tuple((jnp.full((1, tq), NEG_INF, F32), jnp.zeros((1, tq), F32), jnp.zeros((HEAD_DIM, tq), F32))
                 for _ in range(N_HEADS))
    no_mask = lambda k0: None
    sel = lax.fori_loop(0, tile, lambda kc, c: attend([(selk_ref, selt_ref, kc, picked, c)])[0], init)
    edge = tile - n_win_chunks
    win = lax.fori_loop(jnp.maximum(edge, 0), jnp.maximum(edge + 1, 0),
                        lambda kc, c: attend([(wink_ref, wint_ref, kc, lambda k0: [window_edge] * N_KV, c)])[0], init)
    win = lax.fori_loop(jnp.maximum(edge + 1, 0), tile,
                        lambda kc, c: attend([(wink_ref, wint_ref, kc, no_mask, c)])[0], win)
    sel, win = attend([(selk_ref, selt_ref, tile, lambda k0: [hit & causal_diag for hit in picked(k0)], sel),
                       (wink_ref, wint_ref, tile, lambda k0: [causal_diag] * N_KV, win)])
    o_s, o_w = finish(sel), finish(win)
    out_heads = []
    for hd in range(N_HEADS):
        gate = lambda branch: gatest_ref[3 * hd + branch:3 * hd + branch + 1, :]
        out_heads.append(gate(0) * o_c[hd] + gate(1) * o_s[hd] + gate(2) * o_w[hd])
    o_ref[...] = jnp.concatenate(out_heads, axis=0).T


def _nsa_prompt(q_raw_t, q_rot_t, gates_t, ckv, ckv_t, sel_k, sel_t, win_k, win_t, cover_t, expand_t, *, tq, nc):
    b, _, t = q_raw_t.shape
    nchp = ckv.shape[1]
    nsel = cover_t.shape[0]
    assert WINDOW % tq == 0
    tile_t = lambda w: pl.BlockSpec((None, w, tq), lambda i, j: (i, 0, j))
    whole = lambda r, w: pl.BlockSpec((None, r, w), lambda i, j: (i, 0, 0))
    return pl.pallas_call(
        functools.partial(_nsa_prompt_body, tq=tq, nc=nc),
        grid=(b, t // tq),
        in_specs=[tile_t(Q_W), tile_t(Q_W), tile_t(GATE_PAD), whole(nchp, ROW_W), whole(ROW_W, nchp),
                  whole(t, KV_W), whole(ROW_W, t), whole(t, KV_W), whole(ROW_W, t),
                  _const_spec((nsel, nchp)), _const_spec((t, nsel))],
        out_specs=pl.BlockSpec((None, tq, Q_W), lambda i, j: (i, j, 0)),
        out_shape=jax.ShapeDtypeStruct((b, t, Q_W), F32),
        compiler_params=_params("parallel", "arbitrary"),
        name="nsa_prompt",
    )(q_raw_t, q_rot_t, gates_t, ckv, ckv_t, sel_k, sel_t, win_k, win_t, cover_t, expand_t)


def _softplus(x):
    return jnp.maximum(x, 0.0) + jnp.log1p(jnp.exp(-jnp.abs(x)))


def _rg_gates(xc, wgate_ref, ba, bx, lam):
    xcb = xc.astype(BF16)
    n_grp = wgate_ref.shape[0]
    gw = wgate_ref.shape[1]
    za, zx = [], []
    for k in range(n_grp):
        z = jnp.dot(xcb[:, k * gw:(k + 1) * gw], wgate_ref[k], preferred_element_type=F32)
        za.append(z[:, :gw])
        zx.append(z[:, gw:])
    r = _sigmoid(jnp.concatenate(za, axis=1) + ba)
    i = _sigmoid(jnp.concatenate(zx, axis=1) + bx)
    log_a = -RG_C * r * _softplus(-lam)
    a = jnp.exp(log_a)
    u = jnp.sqrt(-jnp.tanh(log_a) * (1.0 + a * a)) * i * xc
    return a, u


def _merge(h_res, hs, gr, ga, gb, o_attn, wbra_ref, wbrr_ref, wout_ref, g2, b2, alpha):
    y_rnn = hs * _gelu(gr)
    m = _sigmoid(ga) * _dot(o_attn, wbra_ref[...]) + _sigmoid(gb) * _dot(y_rnn, wbrr_ref[...])
    return _layer_norm(alpha * h_res + _dot(m, wout_ref[...]), g2, b2)


def _mix_prompt_body(xr_ref, gr_ref, ga_ref, gb_ref, oat_ref, h1_ref, convw_ref, convb_ref, wgate_ref,
                     ba_ref, bx_ref, lam_ref, wbra_ref, wbrr_ref, wout_ref, g2_ref, b2_ref,
                     o_ref, hlast_ref, hc_ref, tail_ref, *, tt, alpha):
    @pl.when(pl.program_id(1) == 0)
    def _():
        hc_ref[...] = jnp.zeros_like(hc_ref)
        tail_ref[...] = jnp.zeros_like(tail_ref)

    x = xr_ref[...]
    convw = convw_ref[...]
    sub = lax.broadcasted_iota(jnp.int32, (SUBLANES, x.shape[1]), 0)
    shifts = range(1, CONV_W)
    prev_rot = [pltpu.roll(tail_ref[...], d, 0) for d in shifts]
    xc_groups = []
    for k in range(tt // SUBLANES):
        x_grp = x[k * SUBLANES:(k + 1) * SUBLANES]
        rot = [pltpu.roll(x_grp, d, 0) for d in shifts]
        xc_grp = convb_ref[...] + convw[CONV_W - 1:CONV_W] * x_grp
        for d, r_prev, r_cur in zip(shifts, prev_rot, rot):
            xc_grp = xc_grp + convw[CONV_W - 1 - d:CONV_W - d] * jnp.where(sub < d, r_prev, r_cur)
        xc_groups.append(xc_grp)
        prev_rot = rot
    xc = jnp.concatenate(xc_groups, axis=0)
    tail_ref[...] = x[tt - SUBLANES:tt]

    a, u = _rg_gates(xc, wgate_ref, ba_ref[...], bx_ref[...], lam_ref[...])
    sub = lax.broadcasted_iota(jnp.int32, (SUBLANES, a.shape[1]), 0)
    h_prev = jnp.broadcast_to(hc_ref[...], sub.shape)
    groups = []
    for k in range(tt // SUBLANES):
        rows = slice(k * SUBLANES, (k + 1) * SUBLANES)
        a_grp, u_grp = a[rows], u[rows]
        d = 1
        while d < SUBLANES:
            inside = sub >= d
            u_grp = jnp.where(inside, a_grp * pltpu.roll(u_grp, d, 0) + u_grp, u_grp)
            a_grp = jnp.where(inside, a_grp * pltpu.roll(a_grp, d, 0), a_grp)
            d *= 2
        h_grp = a_grp * h_prev + u_grp
        groups.append(h_grp)
        h_prev = jnp.broadcast_to(h_grp[SUBLANES - 1:SUBLANES], h_grp.shape)
    hs = jnp.concatenate(groups, axis=0)
    hc_ref[...] = hs[tt - 1:tt]
    hlast_ref[...] = hs[tt - 1:tt]
    o_ref[...] = _merge(h1_ref[...], hs, gr_ref[...], ga_ref[...], gb_ref[...], oat_ref[...],
                        wbra_ref, wbrr_ref, wout_ref, g2_ref[...], b2_ref[...], alpha)


def _mix_weight_specs(d, n_grp, gw):
    return [_const_spec((SUBLANES, d)), _const_spec((1, d)), _const_spec((n_grp, gw, 2 * gw)),
            _const_spec((1, d)), _const_spec((1, d)), _const_spec((1, d)),
            _const_spec((Q_W, d)), _const_spec((d, d)), _const_spec((d, d)),
            _const_spec((1, d)), _const_spec((1, d))]


def _mix_prompt(xr, gr, ga, gb, o_attn, h1, mixw, *, tt, alpha):
    b, t, d = xr.shape
    n_grp, gw = mixw[2].shape[:2]
    tile = lambda w: pl.BlockSpec((None, tt, w), lambda i, j: (i, j, 0))
    return pl.pallas_call(
        functools.partial(_mix_prompt_body, tt=tt, alpha=alpha),
        grid=(b, t // tt),
        in_specs=[tile(d), tile(d), tile(d), tile(d), tile(Q_W), tile(d)] + _mix_weight_specs(d, n_grp, gw),
        out_specs=[tile(d), pl.BlockSpec((None, 1, d), lambda i, j: (i, 0, 0))],
        out_shape=[jax.ShapeDtypeStruct((b, t, d), F32), jax.ShapeDtypeStruct((b, 1, d), F32)],
        scratch_shapes=[pltpu.VMEM((1, d), F32), pltpu.VMEM((SUBLANES, d), F32)],
        compiler_params=_params("parallel", "arbitrary"),
        name="mix_prompt",
    )(xr, gr, ga, gb, o_attn, h1, *mixw)


def _mix_sample_body(xr_ref, c0_ref, c1_ref, c2_ref, h0_ref, gr_ref, ga_ref, gb_ref, oat_ref, h1_ref,
                     convw_ref, convb_ref, wgate_ref, ba_ref, bx_ref, lam_ref, wbra_ref, wbrr_ref, wout_ref,
                     g2_ref, b2_ref, o_ref, hnew_ref, *, alpha):
    convw = convw_ref[...]
    xc = (convb_ref[...] + convw[0:1] * c0_ref[...] + convw[1:2] * c1_ref[...] + convw[2:3] * c2_ref[...]
          + convw[3:4] * xr_ref[...])
    a, u = _rg_gates(xc, wgate_ref, ba_ref[...], bx_ref[...], lam_ref[...])
    hs = a * h0_ref[...] + u
    hnew_ref[...] = hs
    o_ref[...] = _merge(h1_ref[...], hs, gr_ref[...], ga_ref[...], gb_ref[...], oat_ref[...],
                        wbra_ref, wbrr_ref, wout_ref, g2_ref[...], b2_ref[...], alpha)


def _mix_sample(xr, conv_rows, h0, gr, ga, gb, o_attn, h1, mixw, *, alpha):
    b, d = xr.shape
    n_grp, gw = mixw[2].shape[:2]
    full = lambda w: _const_spec((b, w))
    return pl.pallas_call(
        functools.partial(_mix_sample_body, alpha=alpha),
        grid=(1,),
        in_specs=[full(d)] * 8 + [full(Q_W), full(d)] + _mix_weight_specs(d, n_grp, gw),
        out_specs=[full(d), full(d)],
        out_shape=[jax.ShapeDtypeStruct((b, d), F32), jax.ShapeDtypeStruct((b, d), F32)],
        compiler_params=_params("arbitrary"),
        name="mix_sample",
    )(xr, *conv_rows, h0, gr, ga, gb, o_attn, h1, *mixw)


def _head_rows(row, g):
    parts = [row[:, (g * HPG + h) * HEAD_DIM:(g * HPG + h + 1) * HEAD_DIM] for h in range(HPG)]
    return jnp.concatenate(parts + [jnp.zeros((SUBLANES - HPG, HEAD_DIM), F32)], axis=0)


def _sample_scores(qrow, ckv, cover_ref, nc, nsel, q_pos):
    nchp = ckv.shape[0]
    nselp = cover_ref.shape[1]
    cidx = lax.broadcasted_iota(jnp.int32, (1, nchp), 1)
    qrow = qrow * SCALE
    m_c = (cidx * CMP_STRIDE + (CMP_BLOCK - 1) <= q_pos) & (cidx < nc)
    oc_parts, score_rows = [], []
    for g in range(N_KV):
        q = _head_rows(qrow, g)
        s_c = _dot_nt(q, ckv[:, g * HEAD_DIM:(g + 1) * HEAD_DIM])
        e_c, d_c = _softmax_parts(s_c, m_c)
        p_c = e_c / d_c
        o_c = _dot(p_c, ckv[:, KV_W + g * HEAD_DIM:KV_W + (g + 1) * HEAD_DIM])
        oc_parts += [o_c[h:h + 1] for h in range(HPG)]
        p_sum = jnp.sum(p_c[0:HPG], axis=0, keepdims=True)
        imp = _dot_f32(jnp.broadcast_to(p_sum, (SUBLANES, nchp)), cover_ref[...])[0:1]
        score_rows.append(_block_scores(imp, q_pos, nsel))
    score = jnp.concatenate(score_rows + [jnp.full((SUBLANES - N_KV, nselp), NEG_INF, F32)], axis=0)
    return jnp.concatenate(oc_parts, axis=1), score


def _top_blocks_body(score_ref, idx_ref):
    score = score_ref[...]
    n, nselp = score.shape
    lane = lax.broadcasted_iota(jnp.int32, (n, nselp), 1)
    lane_out = lax.broadcasted_iota(jnp.int32, (n, LANES), 1)
    picked = jnp.zeros((n, LANES), jnp.int32)
    for r in range(N_SEL):
        m = jnp.max(score, axis=1, keepdims=True)
        j = jnp.min(jnp.where(score == m, lane, nselp), axis=1, keepdims=True)
        picked = jnp.where(lane_out == r, j, picked)
        score = jnp.where(lane == j, NEG_INF, score)
    idx_ref[...] = picked


def _top_blocks(score):
    n, nselp = score.shape
    return pl.pallas_call(
        _top_blocks_body,
        grid=(1,),
        in_specs=[_const_spec((n, nselp))],
        out_specs=_const_spec((n, LANES)),
        out_shape=jax.ShapeDtypeStruct((n, LANES), jnp.int32),
        compiler_params=_params("arbitrary"),
        name="top_blocks",
    )(score)


def _nsa_sample_attend_body(idx_ref, pt_ref, qrot_ref, gates_ref, oc_ref, selnew_ref, winnew_ref, wincache_ref,
                            *refs, q_pos, n_past_blocks):
    del pt_ref
    page_refs, o_ref = refs[:-1], refs[-1]
    b = pl.program_id(0)
    wb = wincache_ref.shape[2]
    qrow = qrot_ref[...] * SCALE
    gates = gates_ref[...]
    oc = oc_ref[...]
    sel_new = selnew_ref[...]
    win_new = winnew_ref[...]
    page_lane = lax.broadcasted_iota(jnp.int32, (1, PAGE_SIZE), 1)
    wpos = (q_pos - wb) + lax.broadcasted_iota(jnp.int32, (1, wb), 1)
    m_w = (q_pos - wpos <= WINDOW) & (wpos >= 0) & (wpos <= q_pos)
    cur = q_pos // SEL_BLOCK
    out_parts = []
    for g in range(N_KV):
        q = _head_rows(qrow, g)
        feat = slice(g * HEAD_DIM, (g + 1) * HEAD_DIM)
        kcol = slice(g * HEAD_DIM, (g + 1) * HEAD_DIM)
        vcol = slice(KV_W + g * HEAD_DIM, KV_W + (g + 1) * HEAD_DIM)
        kts, vts, kps = [], [], []
        new_chosen = None
        for n in range(N_SEL):
            j = idx_ref[b, g * N_SEL + n]
            page = page_refs[g * N_SEL + n]
            kts.append(page[0, feat, :])
            vts.append(page[1, feat, :])
            in_block = (page_lane // SEL_BLOCK == j % BLOCKS_PER_PAGE) & (j < n_past_blocks)
            kps.append(jnp.where(in_block, (j // BLOCKS_PER_PAGE) * PAGE_SIZE + page_lane, q_pos + 1))
            hit = j == cur
            new_chosen = hit if new_chosen is None else (new_chosen | hit)
        s_s = _dot(q, jnp.concatenate(kts, axis=1))
        s_new = jnp.where(new_chosen, jnp.sum(q * sel_new[:, kcol], axis=1, keepdims=True), NEG_INF)
        s_s = jnp.where(jnp.concatenate(kps, axis=1) <= q_pos, s_s, NEG_INF)
        m = jnp.maximum(jnp.max(s_s, axis=1, keepdims=True), s_new)
        m = jnp.where(m > NEG_INF, m, 0.0)
        e_s = jnp.exp(s_s - m)
        e_new = jnp.exp(s_new - m)
        d_s = jnp.sum(e_s, axis=1, keepdims=True) + e_new
        d_s = jnp.where(d_s > 0, d_s, 1.0)
        o_s = (_dot_nt(e_s, jnp.concatenate(vts, axis=1)) + e_new * sel_new[:, vcol]) / d_s
        s_w = jnp.where(m_w, _dot(q, wincache_ref[0, feat, :]), NEG_INF)
        s_wn = jnp.sum(q * win_new[:, kcol], axis=1, keepdims=True)
        m = jnp.maximum(jnp.max(s_w, axis=1, keepdims=True), s_wn)
        e_w = jnp.exp(s_w - m)
        e_wn = jnp.exp(s_wn - m)
        d_w = jnp.sum(e_w, axis=1, keepdims=True) + e_wn
        o_w = (_dot_nt(e_w, wincache_ref[1, feat, :]) + e_wn * win_new[:, vcol]) / d_w
        for h in range(HPG):
            hd = g * HPG + h
            out_parts.append(gates[:, 3 * hd:3 * hd + 1] * oc[:, hd * HEAD_DIM:(hd + 1) * HEAD_DIM]
                             + gates[:, 3 * hd + 1:3 * hd + 2] * o_s[h:h + 1]
                             + gates[:, 3 * hd + 2:3 * hd + 3] * o_w[h:h + 1])
    o_ref[...] = jnp.concatenate(out_parts, axis=1)


def _nsa_sample_attend(idx, page_table, q_rot, gates, o_c, sel_new, win_new, win_cache_t, sel_cache_t,
                       *, q_pos, n_past_blocks):
    b = q_rot.shape[0]
    wb = win_cache_t.shape[3]

    def page_spec(k):
        def index(i, idx_ref, pt_ref):
            j = jnp.minimum(idx_ref[i, k], n_past_blocks - 1)
            return (pt_ref[i, j // BLOCKS_PER_PAGE], 0, 0, 0)
        return pl.BlockSpec((None, 2, KV_W, PAGE_SIZE), index)

    one = lambda w: pl.BlockSpec((None, 1, w), lambda i, idx_ref, pt_ref: (i, 0, 0))
    return pl.pallas_call(
        functools.partial(_nsa_sample_attend_body, q_pos=q_pos, n_past_blocks=n_past_blocks),
        grid_spec=pltpu.PrefetchScalarGridSpec(
            num_scalar_prefetch=2,
            grid=(b,),
            in_specs=[one(Q_W), one(GATE_PAD), one(Q_W), one(ROW_W), one(ROW_W),
                      pl.BlockSpec((None, 2, KV_W, wb), lambda i, idx_ref, pt_ref: (i, 0, 0, 0))]
            + [page_spec(k) for k in range(N_KV * N_SEL)],
            out_specs=one(Q_W),
        ),
        out_shape=jax.ShapeDtypeStruct((b, 1, Q_W), F32),
        compiler_params=_params("arbitrary"),
        name="nsa_sample_attend",
    )(idx, page_table, q_rot, gates, o_c, sel_new, win_new, win_cache_t, *([sel_cache_t] * (N_KV * N_SEL)))


def _rope_tables(pos):
    half = HEAD_DIM // 2
    freqs = ROPE_THETA ** (-jnp.arange(half, dtype=F32) / half)
    ang = pos.astype(F32)[:, None] * freqs[None, :]
    cos, sin = jnp.cos(ang), jnp.sin(ang)
    reps = LANES // HEAD_DIM
    return jnp.tile(jnp.concatenate([cos, cos], axis=1), (1, reps)), jnp.tile(jnp.concatenate([-sin, sin], axis=1), (1, reps))


def _cover(nc, nsel, rows, cols):
    start = np.arange(rows)[:, None] * CMP_STRIDE
    j = np.arange(cols)[None, :]
    hit = (start < (j + 1) * SEL_BLOCK) & (start + CMP_BLOCK > j * SEL_BLOCK) & (np.arange(rows)[:, None] < nc) & (j < nsel)
    return jnp.asarray(hit.astype(np.float32))


def _cmp_chunk_weight(w1_k, w1_v):
    per = jnp.stack([w.reshape(CMP_RATIO, CMP_STRIDE, HEAD_DIM, CMP_HID) for w in (w1_k, w1_v)])
    eye = jnp.eye(N_KV, dtype=F32)
    big = jnp.einsum("ab,krsdh->ksadbrh", eye, per)
    return big.reshape(2, CMP_STRIDE, KV_W, N_KV * CMP_RATIO * CMP_HID).astype(BF16)


def _rg_gate_weight(w_a, w_x):
    nb, bw, _ = w_a.shape
    per = MXU_DIM // bw
    n_grp = nb // per
    eye = jnp.eye(per, dtype=F32)

    def group(w):
        w = w.reshape(n_grp, per, bw, bw)
        return jnp.einsum("pq,gpde->gpdqe", eye, w).reshape(n_grp, per * bw, per * bw)

    return jnp.concatenate([group(w_a), group(w_x)], axis=2).astype(BF16)


def _feature_major(kv_rows):
    lead = kv_rows.shape[:-4]
    rows = kv_rows.shape[-4]
    nd = len(lead)
    perm = tuple(range(nd)) + (nd + 1, nd + 2, nd + 3, nd)
    return jnp.transpose(kv_rows, perm).reshape(lead + (2, KV_W, rows))


def _row_major_view(kv_t, b, rows):
    return jnp.transpose(kv_t.reshape(b, 2, N_KV, HEAD_DIM, rows), (0, 4, 1, 2, 3))


def kernel(x_prompt, x_sample, cache_cmp_kv, cache_sel_kv, cache_win_kv, state_conv, state_h, page_table, ffn1_w_gate, ffn1_w_up, ffn1_w_down, ln1_g, ln1_b, w_in, cmp_pe_k, cmp_w1_k, cmp_w2_k, cmp_pe_v, cmp_w1_v, cmp_w2_v, conv_w, conv_b, rg_w_a, rg_b_a, rg_w_x, rg_b_x, rg_lam, w_br_attn, w_br_rnn, w_out, ln2_g, ln2_b, ffn2_w_gate, ffn2_w_up, ffn2_w_down, ln3_g, ln3_b):
    bp, tp, d = x_prompt.shape
    bs, ts, _ = x_sample.shape
    depth = w_in.shape[0]
    d_rnn = conv_w.shape[2]
    n_pages = page_table.shape[1]
    past_len = n_pages * PAGE_SIZE
    assert ts == 1 and tp % CMP_STRIDE == 0 and tp % SEL_BLOCK == 0
    alpha = (2.0 * depth) ** 0.25
    wb = cache_win_kv.shape[2]
    wbp = min(WINDOW, tp)

    nch_p = tp // CMP_STRIDE
    nc_p = nch_p - CMP_RATIO + 1
    nsel_p = tp // SEL_BLOCK
    nch_s = past_len // CMP_STRIDE
    nc_s = nch_s - CMP_RATIO + 1
    nsel_s = -(-(past_len + ts) // SEL_BLOCK)
    n_past_blocks = past_len // SEL_BLOCK
    assert nsel_s >= N_SEL and n_pages % PAGES_PER_STEP == 0
    nselp_s = -(-nsel_s // LANES) * LANES

    tm = min(256, tp)
    tq = min(512, tp)
    tt = min(256, tp)

    cos_p, sin_p = _rope_tables(jnp.arange(tp))
    cos_s, sin_s = _rope_tables(jnp.full((bs,), past_len))
    cover_p = _cover(nc_p, nsel_p, nch_p, nsel_p).T
    cover_s = _cover(nc_s, nsel_s, nch_s, nselp_s)
    expand_p = jnp.asarray((np.arange(tp)[:, None] // SEL_BLOCK == np.arange(nsel_p)[None, :]).astype(np.float32)).astype(BF16)

    xp = x_prompt.reshape(bp * tp, d)
    xs = x_sample.reshape(bs * ts, d)
    outs = [[] for _ in range(10)]
    for l in range(depth):
        row = lambda v: v[l].reshape(1, -1)
        ffn1 = (ffn1_w_gate[l].astype(BF16), ffn1_w_up[l].astype(BF16), ffn1_w_down[l].astype(BF16), row(ln1_g), row(ln1_b))
        ffn2 = (ffn2_w_gate[l].astype(BF16), ffn2_w_up[l].astype(BF16), ffn2_w_down[l].astype(BF16), row(ln3_g), row(ln3_b))
        n_a = Q_W + 6 * KV_W + 3 * N_HEADS
        w_a = jnp.pad(w_in[l][:, :n_a], ((0, 0), (0, GATE_PAD - 3 * N_HEADS))).astype(BF16)
        w_b = w_in[l][:, n_a:].astype(BF16)
        w_cmp = _cmp_chunk_weight(cmp_w1_k[l], cmp_w1_v[l])
        cmpw = (cmp_pe_k[l].reshape(1, -1), cmp_w1_k[l], cmp_w2_k[l], cmp_pe_v[l].reshape(1, -1), cmp_w1_v[l], cmp_w2_v[l])
        mixw = (jnp.pad(conv_w[l], ((0, SUBLANES - CONV_W), (0, 0))), row(conv_b), _rg_gate_weight(rg_w_a[l], rg_w_x[l]),
                row(rg_b_a), row(rg_b_x), row(rg_lam), w_br_attn[l].astype(BF16), w_br_rnn[l].astype(BF16),
                w_out[l].astype(BF16), row(ln2_g), row(ln2_b))

        h1 = _ffn_half_step(xp, *ffn1, alpha=alpha, tm=tm)
        (cmp, sel_k, win_k, xr, gr, ga, gb, q_raw_t, q_rot_t, gates_t, cmp_t, sel_t, win_t) = _in_proj(
            h1, cos_p, sin_p, w_a, w_b, tm=tm, seq_len=tp, feature_major=True)
        b3 = lambda a: a.reshape(bp, tp, a.shape[-1])
        p1 = _cmp_stage1(b3(cmp), w_cmp)
        ckv, ckv_t = _cmp_finish(p1, *cmpw, nc=nc_p)
        o_attn = _nsa_prompt(q_raw_t, q_rot_t, gates_t, ckv, ckv_t, b3(sel_k), sel_t, b3(win_k), win_t,
                             cover_p, expand_p, tq=tq, nc=nc_p)
        h2, h_last = _mix_prompt(b3(xr), b3(gr), b3(ga), b3(gb), o_attn, b3(h1), mixw, tt=tt, alpha=alpha)
        xp = _ffn_half_step(h2.reshape(bp * tp, d), *ffn2, alpha=alpha, tm=tm)
        outs[0].append(_row_major_view(cmp_t, bp, tp))
        outs[2].append(_row_major_view(sel_t, bp, tp))
        outs[4].append(_row_major_view(win_t[:, :, tp - wbp:], bp, wbp))
        outs[6].append(b3(xr)[:, tp - (CONV_W - 1):])
        outs[8].append(h_last.reshape(bp, d_rnn))

        h1s = _ffn_half_step(xs, *ffn1, alpha=alpha, tm=bs)
        q_raw, q_rot, cmp, sel, win, gates, xr, gr, ga, gb = _in_proj(
            h1s, cos_s, sin_s, w_a, w_b, tm=bs, seq_len=bs, feature_major=False)
        b1 = lambda a: a.reshape(bs, 1, a.shape[-1])
        q_pos = past_len
        o_c, score = _cmp_sample(_feature_major(cache_cmp_kv[l]), page_table, w_cmp, cmpw, b1(q_raw), cover_s,
                                 nc=nc_s, nsel=nsel_s, q_pos=q_pos)
        idx = _top_blocks(score.reshape(bs * SUBLANES, nselp_s))
        idx = idx.reshape(bs, SUBLANES, LANES)[:, :N_KV, :N_SEL].reshape(bs, N_KV * N_SEL)
        o_attn = _nsa_sample_attend(idx, page_table, b1(q_rot), b1(gates), o_c, b1(sel), b1(win),
                                    _feature_major(cache_win_kv[l]), _feature_major(cache_sel_kv[l]),
                                    q_pos=q_pos, n_past_blocks=n_past_blocks)
        conv_rows = [state_conv[l][:, k] for k in range(CONV_W - 1)]
        h2s, h_new = _mix_sample(xr, conv_rows, state_h[l], gr, ga, gb, o_attn.reshape(bs, Q_W), h1s, mixw, alpha=alpha)
        xs = _ffn_half_step(h2s, *ffn2, alpha=alpha, tm=bs)
        outs[1].append(cmp.reshape(bs, ts, 2, N_KV, HEAD_DIM))
        outs[3].append(sel.reshape(bs, ts, 2, N_KV, HEAD_DIM))
        win_all = jnp.concatenate([cache_win_kv[l], win.reshape(bs, ts, 2, N_KV, HEAD_DIM)], axis=1)
        outs[5].append(win_all[:, ts:])
        outs[7].append(jnp.concatenate([state_conv[l], xr[:, None, :]], axis=1)[:, ts:])
        outs[9].append(h_new)

    stacked = [jnp.stack(o) for o in outs]
    cmp_p, cmp_s, sel_p, sel_s, win_p, win_s, conv_p, conv_s, h_p, h_s = stacked
    return (xp.reshape(bp, tp, d), xs.reshape(bs, ts, d), cmp_p, cmp_s, sel_p, sel_s, win_p, win_s,
            conv_p, conv_s, h_p, h_s)
```

```python
import functools
import math

import numpy as np
import jax
import jax.numpy as jnp
from jax import lax
from jax.experimental import pallas as pl
from jax.experimental.pallas import tpu as pltpu

F32 = jnp.float32
BF16 = jnp.bfloat16

N_HEADS = 8
HEAD_DIM = 64
N_KV = 2
HPG = N_HEADS // N_KV
Q_W = N_HEADS * HEAD_DIM
KV_W = N_KV * HEAD_DIM
CMP_BLOCK = 32
CMP_STRIDE = 16
CMP_RATIO = CMP_BLOCK // CMP_STRIDE
CMP_HID = 64
SEL_BLOCK = 64
N_SEL = 16
SEL_BONUS = 1.0e4
WINDOW = 512
ROPE_THETA = 10000.0
RG_BLOCKS = 16
RG_C = 8.0
CONV_W = 4
PAGE_SIZE = 128
LN_EPS = 1e-5
SCALE = HEAD_DIM ** -0.5
QK_SCALE_LOG2 = SCALE * math.log2(math.e)
NEG_INF = float("-inf")

LANES = 128
SUBLANES = 8
BF16_SUBLANES = 16
MXU_DIM = 256
VMEM_LIMIT_BYTES = 56 * 1024 * 1024

ROW_W = 2 * KV_W
P_W = 2 * N_KV * CMP_RATIO * CMP_HID
GATE_PAD = LANES
PAGES_PER_STEP = 32
BLOCKS_PER_PAGE = PAGE_SIZE // SEL_BLOCK
TOKENS_PER_BLOCK = SEL_BLOCK // CMP_STRIDE


def _dot(a, b):
    return jnp.dot(a.astype(BF16), b.astype(BF16), preferred_element_type=F32)


def _dot_nt(a, b):
    return lax.dot_general(a.astype(BF16), b.astype(BF16), (((1,), (1,)), ((), ())),
                           preferred_element_type=F32)


def _dot_f32(a, b):
    return jnp.dot(a, b, preferred_element_type=F32, precision=lax.Precision.HIGHEST)


def _sigmoid(x):
    return 0.5 * jnp.tanh(0.5 * x) + 0.5


def _gelu(x):
    return 0.5 * x * (1.0 + jnp.tanh(math.sqrt(2.0 / math.pi) * (x + 0.044715 * (x * x * x))))


def _layer_norm(y, g, b):
    mu = jnp.mean(y, axis=-1, keepdims=True)
    yc = y - mu
    var = jnp.mean(yc * yc, axis=-1, keepdims=True)
    return yc * lax.rsqrt(var + LN_EPS) * g + b


def _softmax_parts(s, mask):
    s = jnp.where(mask, s, NEG_INF)
    m = jnp.max(s, axis=-1, keepdims=True)
    m = jnp.where(m > NEG_INF, m, 0.0)
    e = jnp.exp(s - m)
    d = jnp.sum(e, axis=-1, keepdims=True)
    return e, jnp.where(d > 0, d, 1.0)


def _const_spec(shape):
    nd = len(shape)
    return pl.BlockSpec(shape, lambda *_: (0,) * nd)


def _params(*sem):
    return pltpu.CompilerParams(dimension_semantics=sem, vmem_limit_bytes=VMEM_LIMIT_BYTES)


def _ffn_body(x_ref, wg_ref, wu_ref, wd_ref, g_ref, b_ref, o_ref, *, alpha):
    x = x_ref[...]
    xb = x.astype(BF16)
    gate = jnp.dot(xb, wg_ref[...], preferred_element_type=F32)
    up = jnp.dot(xb, wu_ref[...], preferred_element_type=F32)
    hmid = gate * _sigmoid(gate) * up
    f = _dot(hmid, wd_ref[...])
    o_ref[...] = _layer_norm(alpha * x + 0.5 * f, g_ref[...], b_ref[...])


def _ffn_half_step(x, wg, wu, wd, g, b, *, alpha, tm):
    n, d = x.shape
    ff = wg.shape[1]
    return pl.pallas_call(
        functools.partial(_ffn_body, alpha=alpha),
        grid=(n // tm,),
        in_specs=[pl.BlockSpec((tm, d), lambda i: (i, 0)),
                  _const_spec((d, ff)), _const_spec((d, ff)), _const_spec((ff, d)),
                  _const_spec((1, d)), _const_spec((1, d))],
        out_specs=pl.BlockSpec((tm, d), lambda i: (i, 0)),
        out_shape=jax.ShapeDtypeStruct((n, d), F32),
        compiler_params=_params("parallel"),
        name="ffn_half_step",
    )(x, wg, wu, wd, g, b)


def _rope(x, cos, sin):
    w = x.shape[1]
    reps = w // LANES
    c = jnp.tile(cos, (1, reps)) if reps > 1 else cos
    s = jnp.tile(sin, (1, reps)) if reps > 1 else sin
    lane = lax.broadcasted_iota(jnp.int32, x.shape, 1)
    first_half = (lane % HEAD_DIM) < (HEAD_DIM // 2)
    swapped = jnp.where(first_half, pltpu.roll(x, w - HEAD_DIM // 2, 1), pltpu.roll(x, HEAD_DIM // 2, 1))
    return x * c + swapped * s


def _inproj_body(h_ref, cos_ref, sin_ref, wa_ref, wb_ref, *out_refs, feature_major):
    hb = h_ref[...].astype(BF16)
    cos = cos_ref[...]
    sin = sin_ref[...]
    za = jnp.dot(hb, wa_ref[...], preferred_element_type=F32)
    q = za[:, :Q_W]
    kv = za[:, Q_W:Q_W + 6 * KV_W]
    gates = _sigmoid(za[:, Q_W + 6 * KV_W:])
    q_rot = _rope(q, cos, sin)
    cmp = kv[:, :2 * KV_W]
    sel_k = _rope(kv[:, 2 * KV_W:3 * KV_W], cos, sin)
    win_k = _rope(kv[:, 4 * KV_W:5 * KV_W], cos, sin)
    sel = jnp.concatenate([sel_k, kv[:, 3 * KV_W:4 * KV_W]], axis=1)
    win = jnp.concatenate([win_k, kv[:, 5 * KV_W:6 * KV_W]], axis=1)
    if feature_major:
        (cmp_ref, selk_ref, wink_ref, xr_ref, gr_ref, ga_ref, gb_ref,
         qrawt_ref, qrott_ref, gatest_ref, cmpt_ref, selt_ref, wint_ref) = out_refs
        cmp_ref[...] = cmp
        selk_ref[...] = sel_k.astype(BF16)
        wink_ref[...] = win_k.astype(BF16)
        qrawt_ref[...] = q.T
        qrott_ref[...] = q_rot.T
        gatest_ref[...] = gates.T
        cmpt_ref[...] = cmp.T
        selt_ref[...] = sel.T
        wint_ref[...] = win.T
    else:
        qraw_ref, qrot_ref, cmp_ref, sel_ref, win_ref, gates_ref, xr_ref, gr_ref, ga_ref, gb_ref = out_refs
        qraw_ref[...] = q
        qrot_ref[...] = q_rot
        cmp_ref[...] = cmp
        sel_ref[...] = sel
        win_ref[...] = win
        gates_ref[...] = gates
    zb = jnp.dot(hb, wb_ref[...], preferred_element_type=F32)
    d = xr_ref.shape[1]
    xr_ref[...] = zb[:, :d]
    gr_ref[...] = zb[:, d:2 * d]
    ga_ref[...] = zb[:, 2 * d:3 * d]
    gb_ref[...] = zb[:, 3 * d:]


def _in_proj(h, cos, sin, wa, wb, *, tm, seq_len, feature_major):
    n, d = h.shape
    wa_w, wb_w = wa.shape[1], wb.shape[1]
    tiles = seq_len // tm
    row = lambda i: (i, 0)
    tab = lambda i: (i % tiles, 0)
    if feature_major:
        rows = [(ROW_W, F32), (KV_W, BF16), (KV_W, BF16), (d, F32), (d, F32), (d, F32), (d, F32)]
        cols = [Q_W, Q_W, GATE_PAD, ROW_W, ROW_W, ROW_W]
    else:
        rows = [(w, F32) for w in (Q_W, Q_W, ROW_W, ROW_W, ROW_W, GATE_PAD, d, d, d, d)]
        cols = []
    out_specs = [pl.BlockSpec((tm, w), row) for w, _ in rows]
    out_shape = [jax.ShapeDtypeStruct((n, w), dt) for w, dt in rows]
    out_specs += [pl.BlockSpec((None, w, tm), lambda i: (i // tiles, 0, i % tiles)) for w in cols]
    out_shape += [jax.ShapeDtypeStruct((n // seq_len, w, seq_len), F32) for w in cols]
    return pl.pallas_call(
        functools.partial(_inproj_body, feature_major=feature_major),
        grid=(n // tm,),
        in_specs=[pl.BlockSpec((tm, d), row), pl.BlockSpec((tm, LANES), tab), pl.BlockSpec((tm, LANES), tab),
                  _const_spec((d, wa_w)), _const_spec((d, wb_w))],
        out_specs=out_specs,
        out_shape=out_shape,
        compiler_params=_params("parallel"),
        name="in_proj",
    )(h, cos, sin, wa, wb)


def _cmp_chunks(rows_of, w_ref, kv, nch):
    acc = None
    for s in range(CMP_STRIDE):
        part = _dot(rows_of(s, nch), w_ref[kv, s])
        acc = part if acc is None else acc + part
    return acc


def _cmp_stage1_body(x_ref, w_ref, o_ref, rows_ref):
    nch = x_ref.shape[0] // CMP_STRIDE
    half = P_W // 2
    for kv in range(2):
        rows_ref[kv] = x_ref[:, kv * KV_W:(kv + 1) * KV_W]
        rows_of = lambda s, n, kv=kv: rows_ref[kv, pl.ds(s, n, stride=CMP_STRIDE), :]
        o_ref[:, kv * half:(kv + 1) * half] = _cmp_chunks(rows_of, w_ref, kv, nch)


def _cmp_stage1(x, w):
    b, t, _ = x.shape
    nch = t // CMP_STRIDE
    return pl.pallas_call(
        _cmp_stage1_body,
        grid=(b,),
        in_specs=[pl.BlockSpec((None, t, ROW_W), lambda i: (i, 0, 0)), _const_spec(w.shape)],
        out_specs=pl.BlockSpec((None, nch, P_W), lambda i: (i, 0, 0)),
        out_shape=jax.ShapeDtypeStruct((b, nch, P_W), F32),
        scratch_shapes=[pltpu.VMEM((2, t, KV_W), F32)],
        compiler_params=_params("parallel"),
        name="cmp_stage1",
    )(x, w)


def _cmp_sample_body(pt_ref, *refs, nc, nsel, q_pos):
    del pt_ref
    x_refs = refs[:PAGES_PER_STEP]
    (perm_ref, w_ref, pek_ref, w1k_ref, w2k_ref, pev_ref, w1v_ref, w2v_ref, qraw_ref,
     oc_ref, score_ref, planes_ref, stage_ref, p_ref) = refs[PAGES_PER_STEP:]
    step = pl.program_id(1)
    cpp = PAGE_SIZE // CMP_STRIDE
    nch_step = PAGES_PER_STEP * cpp
    perm = perm_ref[...]
    for k, x_ref in enumerate(x_refs):
        for kv in range(2):
            stage_ref[k, kv] = jnp.dot(x_ref[kv].astype(BF16), perm, preferred_element_type=F32)
            rows = stage_ref[k, kv].T
            for s in range(CMP_STRIDE):
                planes_ref[kv, s, k * cpp:(k + 1) * cpp, :] = rows[s * cpp:(s + 1) * cpp]
    half = P_W // 2
    row0 = pl.multiple_of(step * nch_step, nch_step)
    for kv in range(2):
        acc = None
        for j in range(CMP_STRIDE // 2):
            pair = jnp.concatenate([planes_ref[kv, 2 * j], planes_ref[kv, 2 * j + 1]], axis=1)
            part = _dot(pair, w_ref[kv, j])
            acc = part if acc is None else acc + part
        p_ref[pl.ds(row0, nch_step), kv * half:(kv + 1) * half] = acc

    @pl.when(step == pl.num_programs(1) - 1)
    def _():
        ckv = _cmp_second_layer(p_ref[...], pek_ref, w1k_ref, w2k_ref, pev_ref, w1v_ref, w2v_ref, nc)
        oc_ref[...], score_ref[...] = _sample_scores(qraw_ref[...], ckv, score_ref.shape[1], nc, nsel, q_pos)


def _cmp_sample(cache_t, page_table, w, cmpw, q_raw, *, nc, nsel, q_pos):
    b, n_pages = page_table.shape
    steps = n_pages // PAGES_PER_STEP
    cpp = PAGE_SIZE // CMP_STRIDE
    nch = n_pages * cpp
    width = max(nch + LANES, -(-(TOKENS_PER_BLOCK * nsel) // LANES) * LANES)
    w = w.reshape(2, CMP_STRIDE // 2, 2 * KV_W, w.shape[-1])
    row = np.arange(PAGE_SIZE)
    perm = np.zeros((PAGE_SIZE, PAGE_SIZE), np.float32)
    perm[row, (row % CMP_STRIDE) * cpp + row // CMP_STRIDE] = 1.0
    perm = jnp.asarray(perm).astype(BF16)

    def page_spec(k):
        return pl.BlockSpec((None, 2, KV_W, PAGE_SIZE), lambda i, s, pt: (pt[i, s * PAGES_PER_STEP + k], 0, 0, 0))

    const = lambda a: pl.BlockSpec(a.shape, lambda i, s, pt: (0,) * a.ndim)
    per_request = lambda r, c: pl.BlockSpec((None, r, c), lambda i, s, pt: (i, 0, 0))
    return pl.pallas_call(
        functools.partial(_cmp_sample_body, nc=nc, nsel=nsel, q_pos=q_pos),
        grid_spec=pltpu.PrefetchScalarGridSpec(
            num_scalar_prefetch=1,
            grid=(b, steps),
            in_specs=[page_spec(k) for k in range(PAGES_PER_STEP)]
            + [const(perm), const(w)] + [const(a) for a in cmpw] + [per_request(1, Q_W)],
            out_specs=[per_request(1, Q_W), per_request(SUBLANES, width)],
            scratch_shapes=[pltpu.VMEM((2, CMP_STRIDE, PAGES_PER_STEP * cpp, KV_W), F32),
                            pltpu.VMEM((PAGES_PER_STEP, 2, KV_W, PAGE_SIZE), F32),
                            pltpu.VMEM((nch, P_W), F32)],
        ),
        out_shape=[jax.ShapeDtypeStruct((b, 1, Q_W), F32), jax.ShapeDtypeStruct((b, SUBLANES, width), F32)],
        compiler_params=_params("parallel", "arbitrary"),
        name="cmp_sample",
    )(page_table, *([cache_t] * PAGES_PER_STEP), perm, w, *cmpw, q_raw)


def _cmp_second_layer(p, pek_ref, w1k_ref, w2k_ref, pev_ref, w1v_ref, w2v_ref, nc):
    nch = p.shape[0]
    row = lax.broadcasted_iota(jnp.int32, (nch, CMP_HID), 0)
    outs = []
    for kv, (pe_ref, w1_ref, w2_ref) in enumerate(((pek_ref, w1k_ref, w2k_ref), (pev_ref, w1v_ref, w2v_ref))):
        pe = jnp.broadcast_to(pe_ref[...], (SUBLANES, pe_ref.shape[1]))
        bias = _dot(pe, w1_ref[...])[0:1]
        for g in range(N_KV):
            c0 = (kv * N_KV + g) * CMP_RATIO * CMP_HID
            u = bias + p[:, c0:c0 + CMP_HID]
            for r in range(1, CMP_RATIO):
                u = u + pltpu.roll(p[:, c0 + r * CMP_HID:c0 + (r + 1) * CMP_HID], nch - r, 0)
            out = _dot(_gelu(u), w2_ref[...])
            outs.append(jnp.where(row < nc, out, 0.0))
    return jnp.concatenate(outs, axis=1)


def _cmp_finish_body(p_ref, pek_ref, w1k_ref, w2k_ref, pev_ref, w1v_ref, w2v_ref, o_ref, ot_ref, *, nc):
    ckv = _cmp_second_layer(p_ref[...], pek_ref, w1k_ref, w2k_ref, pev_ref, w1v_ref, w2v_ref, nc)
    o_ref[...] = ckv
    ot_ref[...] = ckv.T


def _cmp_finish(p, pe_k, w1_k, w2_k, pe_v, w1_v, w2_v, *, nc):
    b, nch, _ = p.shape
    flat = CMP_BLOCK * HEAD_DIM
    return pl.pallas_call(
        functools.partial(_cmp_finish_body, nc=nc),
        grid=(b,),
        in_specs=[pl.BlockSpec((None, nch, P_W), lambda i: (i, 0, 0)),
                  _const_spec((1, flat)), _const_spec((flat, CMP_HID)), _const_spec((CMP_HID, HEAD_DIM)),
                  _const_spec((1, flat)), _const_spec((flat, CMP_HID)), _const_spec((CMP_HID, HEAD_DIM))],
        out_specs=[pl.BlockSpec((None, nch, ROW_W), lambda i: (i, 0, 0)),
                   pl.BlockSpec((None, ROW_W, nch), lambda i: (i, 0, 0))],
        out_shape=[jax.ShapeDtypeStruct((b, nch, ROW_W), F32), jax.ShapeDtypeStruct((b, ROW_W, nch), F32)],
        compiler_params=_params("parallel"),
        name="cmp_finish",
    )(p, pe_k, w1_k, w2_k, pe_v, w1_v, w2_v)


def _block_scores(imp, q_pos, nsel_valid, block_axis=1):
    jblk = lax.broadcasted_iota(jnp.int32, imp.shape, block_axis)
    cur = q_pos // SEL_BLOCK
    valid = (jblk * SEL_BLOCK <= q_pos) & (jblk < nsel_valid)
    forced = (jblk == 0) | (jblk == cur) | (jblk == cur - 1)
    return jnp.where(valid, imp + jnp.where(forced, SEL_BONUS, 0.0), NEG_INF)


def _nsa_prompt_body(qrawt_ref, qrott_ref, gatest_ref, ckv_ref, ckvt_ref, selk_ref, selt_ref, wink_ref, wint_ref,
                     covert_ref, expandt_ref, o_ref, *, tq, nc):
    nchp = ckv_ref.shape[0]
    nsel = covert_ref.shape[0]
    tile = pl.program_id(1)
    s0 = tile * tq
    lane_t = lax.broadcasted_iota(jnp.int32, (1, tq), 1)
    key_sub = lax.broadcasted_iota(jnp.int32, (tq, 1), 0)
    qpos = s0 + lane_t
    causal_diag = key_sub <= lane_t
    window_edge = lane_t <= key_sub
    cidx = lax.broadcasted_iota(jnp.int32, (nchp, 1), 0)
    jblk = lax.broadcasted_iota(jnp.int32, (nsel, 1), 0)
    n_win_chunks = WINDOW // tq
    vrows = [slice(KV_W + g * HEAD_DIM, KV_W + (g + 1) * HEAD_DIM) for g in range(N_KV)]
    ones_rows = jnp.ones((BF16_SUBLANES, tq), BF16)
    group_of = lambda hd: hd // HPG

    def group_rows(qt, g):
        z = jnp.zeros_like(qt)
        return jnp.concatenate([qt, z] if g == 0 else [z, qt], axis=0)

    def head_rows(ref, hd):
        return group_rows((ref[hd * HEAD_DIM:(hd + 1) * HEAD_DIM, :] * QK_SCALE_LOG2).astype(BF16), group_of(hd))

    q_rot = [head_rows(qrott_ref, hd) for hd in range(N_HEADS)]
    q_raw = jnp.concatenate([head_rows(qrawt_ref, hd) for hd in range(N_HEADS)], axis=1)
    s_c = jnp.dot(ckv_ref[:, :KV_W].astype(BF16), q_raw, preferred_element_type=F32)
    m_c = (cidx * CMP_STRIDE + (CMP_BLOCK - 1) <= jnp.concatenate([qpos] * N_HEADS, axis=1)) & (cidx < nc)
    s_c = jnp.where(m_c, s_c, NEG_INF)
    mx = jnp.max(s_c, axis=0, keepdims=True)
    e_c = jnp.exp2(s_c - jnp.where(mx > NEG_INF, mx, 0.0))
    d_c = jnp.sum(e_c, axis=0, keepdims=True)
    p_c = e_c * (1.0 / jnp.where(d_c > 0, d_c, 1.0))
    head_cols = lambda hd: slice(hd * tq, (hd + 1) * tq)
    o_c = [_dot(ckvt_ref[vrows[group_of(hd)], :], p_c[:, head_cols(hd)]) for hd in range(N_HEADS)]
    p_sum = []
    for g in range(N_KV):
        acc = p_c[:, head_cols(g * HPG)]
        for hd in range(g * HPG + 1, (g + 1) * HPG):
            acc = acc + p_c[:, head_cols(hd)]
        p_sum.append(acc)
    imp = _dot_f32(covert_ref[...], jnp.concatenate(p_sum, axis=1))
    score = _block_scores(imp, jnp.concatenate([qpos] * N_KV, axis=1), nsel, block_axis=0)
    rank = jnp.zeros(score.shape, jnp.int32)
    for i in range(nsel):
        s_i = score[i:i + 1, :]
        beats = (s_i > score) | ((s_i == score) & (i < jblk))
        rank = rank + beats.astype(jnp.int32)
    chosen = (rank < N_SEL).astype(BF16)

    def picked(k0):
        hit = jnp.dot(expandt_ref[pl.ds(k0, tq), :], chosen, preferred_element_type=F32)
        return [hit[:, g * tq:(g + 1) * tq] > 0.5 for g in range(N_KV)]

    def attend(jobs):
        loaded = []
        for k_ref, vt_ref, kc, masks_of, _ in jobs:
            k0 = pl.multiple_of(kc * tq, tq)
            keys = k_ref[pl.ds(k0, tq), :]
            values_t = [jnp.concatenate([vt_ref[rows, pl.ds(k0, tq)].astype(BF16), ones_rows], axis=0)
                        for rows in vrows]
            loaded.append((keys, values_t, masks_of(k0)))
        scores = [[jnp.dot(keys, q_rot[hd], preferred_element_type=F32) for hd in range(N_HEADS)]
                  for keys, _, _ in loaded]
        updates = []
        for job, (_, _, masks), sc in zip(jobs, loaded, scores):
            carry = job[4]
            stats, probs = [], []
            for hd in range(N_HEADS):
                m = carry[hd][0]
                s = sc[hd] if masks is None else jnp.where(masks[group_of(hd)], sc[hd], NEG_INF)
                m_new = jnp.maximum(m, jnp.max(s, axis=0, keepdims=True))
                m_safe = m_new if masks is None else jnp.where(m_new > NEG_INF, m_new, 0.0)
                alpha = jnp.exp2(m - m_safe)
                stats.append((m_new, alpha))
                probs.append(jnp.exp2(s - m_safe).astype(BF16))
            updates.append((stats, probs))
        out = []
        for job, (_, values_t, _), (stats, probs) in zip(jobs, loaded, updates):
            carry = job[4]
            pv = [jnp.dot(values_t[group_of(hd)], probs[hd], preferred_element_type=F32) for hd in range(N_HEADS)]
            out.append(tuple((stats[hd][0],
                              carry[hd][1] * stats[hd][1] + pv[hd][HEAD_DIM:HEAD_DIM + 1],
                              carry[hd][2] * stats[hd][1] + pv[hd][:HEAD_DIM]) for hd in range(N_HEADS)))
        return out

    def finish(carry):
        return [acc * (1.0 / jnp.where(l > 0, l, 1.0)) for _, l, acc in carry]

    init = tuple((jnp.full((1, tq), NEG_INF, F32), jnp.zeros((1, tq), F32), jnp.zeros((HEAD_DIM, tq), F32))
                 for _ in range(N_HEADS))
    no_mask = lambda k0: None
    sel = lax.fori_loop(0, tile, lambda kc, c: attend([(selk_ref, selt_ref, kc, picked, c)])[0], init)
    edge = tile - n_win_chunks
    win = lax.fori_loop(jnp.maximum(edge, 0), jnp.maximum(edge + 1, 0),
                        lambda kc, c: attend([(wink_ref, wint_ref, kc, lambda k0: [window_edge] * N_KV, c)])[0], init)
    win = lax.fori_loop(jnp.maximum(edge + 1, 0), tile,
                        lambda kc, c: attend([(wink_ref, wint_ref, kc, no_mask, c)])[0], win)
    sel, win = attend([(selk_ref, selt_ref, tile, lambda k0: [hit & causal_diag for hit in picked(k0)], sel),
                       (wink_ref, wint_ref, tile, lambda k0: [causal_diag] * N_KV, win)])
    o_s, o_w = finish(sel), finish(win)
    out_heads = []
    for hd in range(N_HEADS):
        gate = lambda branch: gatest_ref[3 * hd + branch:3 * hd + branch + 1, :]
        out_heads.append(gate(0) * o_c[hd] + gate(1) * o_s[hd] + gate(2) * o_w[hd])
    o_ref[...] = jnp.concatenate(out_heads, axis=0).T


def _nsa_prompt(q_raw_t, q_rot_t, gates_t, ckv, ckv_t, sel_k, sel_t, win_k, win_t, cover_t, expand_t, *, tq, nc):
    b, _, t = q_raw_t.shape
    nchp = ckv.shape[1]
    nsel = cover_t.shape[0]
    assert WINDOW % tq == 0
    tile_t = lambda w: pl.BlockSpec((None, w, tq), lambda i, j: (i, 0, j))
    whole = lambda r, w: pl.BlockSpec((None, r, w), lambda i, j: (i, 0, 0))
    return pl.pallas_call(
        functools.partial(_nsa_prompt_body, tq=tq, nc=nc),
        grid=(b, t // tq),
        in_specs=[tile_t(Q_W), tile_t(Q_W), tile_t(GATE_PAD), whole(nchp, ROW_W), whole(ROW_W, nchp),
                  whole(t, KV_W), whole(ROW_W, t), whole(t, KV_W), whole(ROW_W, t),
                  _const_spec((nsel, nchp)), _const_spec((t, nsel))],
        out_specs=pl.BlockSpec((None, tq, Q_W), lambda i, j: (i, j, 0)),
        out_shape=jax.ShapeDtypeStruct((b, t, Q_W), F32),
        compiler_params=_params("parallel", "arbitrary"),
        name="nsa_prompt",
    )(q_raw_t, q_rot_t, gates_t, ckv, ckv_t, sel_k, sel_t, win_k, win_t, cover_t, expand_t)


def _softplus(x):
    return jnp.maximum(x, 0.0) + jnp.log1p(jnp.exp(-jnp.abs(x)))


def _rg_gates(xc, wgate_ref, ba, bx, lam):
    xcb = xc.astype(BF16)
    n_grp = wgate_ref.shape[0]
    gw = wgate_ref.shape[1]
    za, zx = [], []
    for k in range(n_grp):
        z = jnp.dot(xcb[:, k * gw:(k + 1) * gw], wgate_ref[k], preferred_element_type=F32)
        za.append(z[:, :gw])
        zx.append(z[:, gw:])
    r = _sigmoid(jnp.concatenate(za, axis=1) + ba)
    i = _sigmoid(jnp.concatenate(zx, axis=1) + bx)
    log_a = r * (-RG_C * _softplus(-lam))
    a = jnp.exp(log_a)
    u = jnp.sqrt(-jnp.tanh(log_a) * (1.0 + a * a)) * i * xc
    return a, u


def _merge(h_res, hs, gr, ga, gb, o_attn, wbra_ref, wbrr_ref, wout_ref, g2, b2, alpha):
    y_rnn = hs * _gelu(gr)
    m = _sigmoid(ga) * _dot(o_attn, wbra_ref[...]) + _sigmoid(gb) * _dot(y_rnn, wbrr_ref[...])
    return _layer_norm(alpha * h_res + _dot(m, wout_ref[...]), g2, b2)


def _mix_prompt_body(xr_ref, gr_ref, ga_ref, gb_ref, oat_ref, h1_ref, convw_ref, convb_ref, wgate_ref,
                     ba_ref, bx_ref, lam_ref, wbra_ref, wbrr_ref, wout_ref, g2_ref, b2_ref,
                     o_ref, hlast_ref, hc_ref, tail_ref, *, tt, alpha):
    @pl.when(pl.program_id(1) == 0)
    def _():
        hc_ref[...] = jnp.zeros_like(hc_ref)
        tail_ref[...] = jnp.zeros_like(tail_ref)

    x = xr_ref[...]
    convw = convw_ref[...]
    sub = lax.broadcasted_iota(jnp.int32, (SUBLANES, x.shape[1]), 0)
    shifts = range(1, CONV_W)
    prev_rot = [pltpu.roll(tail_ref[...], d, 0) for d in shifts]
    xc_groups = []
    for k in range(tt // SUBLANES):
        x_grp = x[k * SUBLANES:(k + 1) * SUBLANES]
        rot = [pltpu.roll(x_grp, d, 0) for d in shifts]
        xc_grp = convb_ref[...] + convw[CONV_W - 1:CONV_W] * x_grp
        for d, r_prev, r_cur in zip(shifts, prev_rot, rot):
            xc_grp = xc_grp + convw[CONV_W - 1 - d:CONV_W - d] * jnp.where(sub < d, r_prev, r_cur)
        xc_groups.append(xc_grp)
        prev_rot = rot
    xc = jnp.concatenate(xc_groups, axis=0)
    tail_ref[...] = x[tt - SUBLANES:tt]

    a, u = _rg_gates(xc, wgate_ref, ba_ref[...], bx_ref[...], lam_ref[...])
    sub = lax.broadcasted_iota(jnp.int32, (SUBLANES, a.shape[1]), 0)
    h_prev = jnp.broadcast_to(hc_ref[...], sub.shape)
    groups = []
    for k in range(tt // SUBLANES):
        rows = slice(k * SUBLANES, (k + 1) * SUBLANES)
        a_grp, u_grp = a[rows], u[rows]
        d = 1
        while d < SUBLANES:
            inside = sub >= d
            u_grp = jnp.where(inside, a_grp * pltpu.roll(u_grp, d, 0) + u_grp, u_grp)
            a_grp = jnp.where(inside, a_grp * pltpu.roll(a_grp, d, 0), a_grp)
            d *= 2
        h_grp = a_grp * h_prev + u_grp
        groups.append(h_grp)
        h_prev = jnp.broadcast_to(h_grp[SUBLANES - 1:SUBLANES], h_grp.shape)
    hs = jnp.concatenate(groups, axis=0)
    hc_ref[...] = hs[tt - 1:tt]
    hlast_ref[...] = hs[tt - 1:tt]
    o_ref[...] = _merge(h1_ref[...], hs, gr_ref[...], ga_ref[...], gb_ref[...], oat_ref[...],
                        wbra_ref, wbrr_ref, wout_ref, g2_ref[...], b2_ref[...], alpha)


def _mix_weight_specs(d, n_grp, gw):
    return [_const_spec((SUBLANES, d)), _const_spec((1, d)), _const_spec((n_grp, gw, 2 * gw)),
            _const_spec((1, d)), _const_spec((1, d)), _const_spec((1, d)),
            _const_spec((Q_W, d)), _const_spec((d, d)), _const_spec((d, d)),
            _const_spec((1, d)), _const_spec((1, d))]


def _mix_prompt(xr, gr, ga, gb, o_attn, h1, mixw, *, tt, alpha):
    b, t, d = xr.shape
    n_grp, gw = mixw[2].shape[:2]
    tile = lambda w: pl.BlockSpec((None, tt, w), lambda i, j: (i, j, 0))
    return pl.pallas_call(
        functools.partial(_mix_prompt_body, tt=tt, alpha=alpha),
        grid=(b, t // tt),
        in_specs=[tile(d), tile(d), tile(d), tile(d), tile(Q_W), tile(d)] + _mix_weight_specs(d, n_grp, gw),
        out_specs=[tile(d), pl.BlockSpec((None, 1, d), lambda i, j: (i, 0, 0))],
        out_shape=[jax.ShapeDtypeStruct((b, t, d), F32), jax.ShapeDtypeStruct((b, 1, d), F32)],
        scratch_shapes=[pltpu.VMEM((1, d), F32), pltpu.VMEM((SUBLANES, d), F32)],
        compiler_params=_params("parallel", "arbitrary"),
        name="mix_prompt",
    )(xr, gr, ga, gb, o_attn, h1, *mixw)


def _mix_sample_body(xr_ref, c0_ref, c1_ref, c2_ref, h0_ref, gr_ref, ga_ref, gb_ref, oat_ref, h1_ref,
                     convw_ref, convb_ref, wgate_ref, ba_ref, bx_ref, lam_ref, wbra_ref, wbrr_ref, wout_ref,
                     g2_ref, b2_ref, o_ref, hnew_ref, *, alpha):
    convw = convw_ref[...]
    xc = (convb_ref[...] + convw[0:1] * c0_ref[...] + convw[1:2] * c1_ref[...] + convw[2:3] * c2_ref[...]
          + convw[3:4] * xr_ref[...])
    a, u = _rg_gates(xc, wgate_ref, ba_ref[...], bx_ref[...], lam_ref[...])
    hs = a * h0_ref[...] + u
    hnew_ref[...] = hs
    o_ref[...] = _merge(h1_ref[...], hs, gr_ref[...], ga_ref[...], gb_ref[...], oat_ref[...],
                        wbra_ref, wbrr_ref, wout_ref, g2_ref[...], b2_ref[...], alpha)


def _mix_sample(xr, conv_rows, h0, gr, ga, gb, o_attn, h1, mixw, *, alpha):
    b, d = xr.shape
    n_grp, gw = mixw[2].shape[:2]
    full = lambda w: _const_spec((b, w))
    return pl.pallas_call(
        functools.partial(_mix_sample_body, alpha=alpha),
        grid=(1,),
        in_specs=[full(d)] * 8 + [full(Q_W), full(d)] + _mix_weight_specs(d, n_grp, gw),
        out_specs=[full(d), full(d)],
        out_shape=[jax.ShapeDtypeStruct((b, d), F32), jax.ShapeDtypeStruct((b, d), F32)],
        compiler_params=_params("arbitrary"),
        name="mix_sample",
    )(xr, *conv_rows, h0, gr, ga, gb, o_attn, h1, *mixw)


def _head_rows(row, g):
    parts = [row[:, (g * HPG + h) * HEAD_DIM:(g * HPG + h + 1) * HEAD_DIM] for h in range(HPG)]
    return jnp.concatenate(parts + [jnp.zeros((SUBLANES - HPG, HEAD_DIM), F32)], axis=0)


def _sample_scores(qrow, ckv, width, nc, nsel, q_pos):
    nchp = ckv.shape[0]
    cidx = lax.broadcasted_iota(jnp.int32, (1, nchp), 1)
    qrow = qrow * SCALE
    m_c = (cidx * CMP_STRIDE + (CMP_BLOCK - 1) <= q_pos) & (cidx < nc)
    lane = lax.broadcasted_iota(jnp.int32, (1, width), 1)
    jblk = lane // TOKENS_PER_BLOCK
    cur = q_pos // SEL_BLOCK
    on_block = (lane % TOKENS_PER_BLOCK == 0) & (jblk < nsel) & (jblk * SEL_BLOCK <= q_pos)
    bonus = jnp.where((jblk == 0) | (jblk == cur) | (jblk == cur - 1), SEL_BONUS, 0.0)
    oc_parts, score_rows = [], []
    for g in range(N_KV):
        q = _head_rows(qrow, g)
        s_c = _dot_nt(q, ckv[:, g * HEAD_DIM:(g + 1) * HEAD_DIM])
        e_c, d_c = _softmax_parts(s_c, m_c)
        p_c = e_c / d_c
        o_c = _dot(p_c, ckv[:, KV_W + g * HEAD_DIM:KV_W + (g + 1) * HEAD_DIM])
        oc_parts += [o_c[h:h + 1] for h in range(HPG)]
        p_sum = jnp.broadcast_to(jnp.sum(p_c[0:HPG], axis=0, keepdims=True), (SUBLANES, nchp))
        p_sum = jnp.concatenate([p_sum, jnp.zeros((SUBLANES, width - nchp), F32)], axis=1)
        imp = p_sum
        for back in range(1, CMP_RATIO):
            imp = imp + pltpu.roll(p_sum, back, 1)
        for ahead in range(1, TOKENS_PER_BLOCK):
            imp = imp + pltpu.roll(p_sum, width - ahead, 1)
        score_rows.append(jnp.where(on_block, imp[0:1] + bonus, NEG_INF))
    score = jnp.concatenate(score_rows + [jnp.full((SUBLANES - N_KV, width), NEG_INF, F32)], axis=0)
    return jnp.concatenate(oc_parts, axis=1), score


def _top_blocks_body(score_ref, idx_ref):
    score = score_ref[...]
    n, nselp = score.shape
    lane = lax.broadcasted_iota(jnp.int32, (n, nselp), 1)
    lane_out = lax.broadcasted_iota(jnp.int32, (n, LANES), 1)
    picked = jnp.zeros((n, LANES), jnp.int32)
    for r in range(N_SEL):
        m = jnp.max(score, axis=1, keepdims=True)
        j = jnp.min(jnp.where(score == m, lane, nselp), axis=1, keepdims=True)
        picked = jnp.where(lane_out == r, j, picked)
        score = jnp.where(lane == j, NEG_INF, score)
    idx_ref[...] = picked


def _top_blocks(score):
    n, nselp = score.shape
    return pl.pallas_call(
        _top_blocks_body,
        grid=(1,),
        in_specs=[_const_spec((n, nselp))],
        out_specs=_const_spec((n, LANES)),
        out_shape=jax.ShapeDtypeStruct((n, LANES), jnp.int32),
        compiler_params=_params("arbitrary"),
        name="top_blocks",
    )(score)


def _nsa_sample_attend_body(idx_ref, pt_ref, qrot_ref, gates_ref, oc_ref, selnew_ref, winnew_ref, wincache_ref,
                            *refs, q_pos, n_past_blocks):
    del pt_ref
    page_refs, o_ref = refs[:-1], refs[-1]
    b = pl.program_id(0)
    wb = wincache_ref.shape[2]
    qrow = qrot_ref[...] * SCALE
    gates = gates_ref[...]
    oc = oc_ref[...]
    sel_new = selnew_ref[...]
    win_new = winnew_ref[...]
    page_lane = lax.broadcasted_iota(jnp.int32, (1, PAGE_SIZE), 1)
    wpos = (q_pos - wb) + lax.broadcasted_iota(jnp.int32, (1, wb), 1)
    m_w = (q_pos - wpos <= WINDOW) & (wpos >= 0) & (wpos <= q_pos)
    cur = q_pos // SEL_BLOCK
    out_parts = []
    for g in range(N_KV):
        q = _head_rows(qrow, g)
        feat = slice(g * HEAD_DIM, (g + 1) * HEAD_DIM)
        kcol = slice(g * HEAD_DIM, (g + 1) * HEAD_DIM)
        vcol = slice(KV_W + g * HEAD_DIM, KV_W + (g + 1) * HEAD_DIM)
        kts, vts, kps = [], [], []
        new_chosen = None
        for n in range(N_SEL):
            j = idx_ref[b, g * N_SEL + n]
            page = page_refs[g * N_SEL + n]
            kts.append(page[0, feat, :])
            vts.append(page[1, feat, :])
            in_block = (page_lane // SEL_BLOCK == j % BLOCKS_PER_PAGE) & (j < n_past_blocks)
            kps.append(jnp.where(in_block, (j // BLOCKS_PER_PAGE) * PAGE_SIZE + page_lane, q_pos + 1))
            hit = j == cur
            new_chosen = hit if new_chosen is None else (new_chosen | hit)
        s_s = _dot(q, jnp.concatenate(kts, axis=1))
        s_new = jnp.where(new_chosen, jnp.sum(q * sel_new[:, kcol], axis=1, keepdims=True), NEG_INF)
        s_s = jnp.where(jnp.concatenate(kps, axis=1) <= q_pos, s_s, NEG_INF)
        m = jnp.maximum(jnp.max(s_s, axis=1, keepdims=True), s_new)
        m = jnp.where(m > NEG_INF, m, 0.0)
        e_s = jnp.exp(s_s - m)
        e_new = jnp.exp(s_new - m)
        d_s = jnp.sum(e_s, axis=1, keepdims=True) + e_new
        d_s = jnp.where(d_s > 0, d_s, 1.0)
        o_s = (_dot_nt(e_s, jnp.concatenate(vts, axis=1)) + e_new * sel_new[:, vcol]) / d_s
        s_w = jnp.where(m_w, _dot(q, wincache_ref[0, feat, :]), NEG_INF)
        s_wn = jnp.sum(q * win_new[:, kcol], axis=1, keepdims=True)
        m = jnp.maximum(jnp.max(s_w, axis=1, keepdims=True), s_wn)
        e_w = jnp.exp(s_w - m)
        e_wn = jnp.exp(s_wn - m)
        d_w = jnp.sum(e_w, axis=1, keepdims=True) + e_wn
        o_w = (_dot_nt(e_w, wincache_ref[1, feat, :]) + e_wn * win_new[:, vcol]) / d_w
        for h in range(HPG):
            hd = g * HPG + h
            out_parts.append(gates[:, 3 * hd:3 * hd + 1] * oc[:, hd * HEAD_DIM:(hd + 1) * HEAD_DIM]
                             + gates[:, 3 * hd + 1:3 * hd + 2] * o_s[h:h + 1]
                             + gates[:, 3 * hd + 2:3 * hd + 3] * o_w[h:h + 1])
    o_ref[...] = jnp.concatenate(out_parts, axis=1)


def _nsa_sample_attend(idx, page_table, q_rot, gates, o_c, sel_new, win_new, win_cache_t, sel_cache_t,
                       *, q_pos, n_past_blocks):
    b = q_rot.shape[0]
    wb = win_cache_t.shape[3]

    def page_spec(k):
        def index(i, idx_ref, pt_ref):
            j = jnp.minimum(idx_ref[i, k], n_past_blocks - 1)
            return (pt_ref[i, j // BLOCKS_PER_PAGE], 0, 0, 0)
        return pl.BlockSpec((None, 2, KV_W, PAGE_SIZE), index)

    one = lambda w: pl.BlockSpec((None, 1, w), lambda i, idx_ref, pt_ref: (i, 0, 0))
    return pl.pallas_call(
        functools.partial(_nsa_sample_attend_body, q_pos=q_pos, n_past_blocks=n_past_blocks),
        grid_spec=pltpu.PrefetchScalarGridSpec(
            num_scalar_prefetch=2,
            grid=(b,),
            in_specs=[one(Q_W), one(GATE_PAD), one(Q_W), one(ROW_W), one(ROW_W),
                      pl.BlockSpec((None, 2, KV_W, wb), lambda i, idx_ref, pt_ref: (i, 0, 0, 0))]
            + [page_spec(k) for k in range(N_KV * N_SEL)],
            out_specs=one(Q_W),
        ),
        out_shape=jax.ShapeDtypeStruct((b, 1, Q_W), F32),
        compiler_params=_params("arbitrary"),
        name="nsa_sample_attend",
    )(idx, page_table, q_rot, gates, o_c, sel_new, win_new, win_cache_t, *([sel_cache_t] * (N_KV * N_SEL)))


def _rope_tables(pos):
    half = HEAD_DIM // 2
    freqs = ROPE_THETA ** (-jnp.arange(half, dtype=F32) / half)
    ang = pos.astype(F32)[:, None] * freqs[None, :]
    cos, sin = jnp.cos(ang), jnp.sin(ang)
    reps = LANES // HEAD_DIM
    return jnp.tile(jnp.concatenate([cos, cos], axis=1), (1, reps)), jnp.tile(jnp.concatenate([-sin, sin], axis=1), (1, reps))


def _cover(nc, nsel, rows, cols):
    start = np.arange(rows)[:, None] * CMP_STRIDE
    j = np.arange(cols)[None, :]
    hit = (start < (j + 1) * SEL_BLOCK) & (start + CMP_BLOCK > j * SEL_BLOCK) & (np.arange(rows)[:, None] < nc) & (j < nsel)
    return jnp.asarray(hit.astype(np.float32))


def _cmp_chunk_weight(w1_k, w1_v):
    per = jnp.stack([w.reshape(CMP_RATIO, CMP_STRIDE, HEAD_DIM, CMP_HID) for w in (w1_k, w1_v)])
    eye = jnp.eye(N_KV, dtype=F32)
    big = jnp.einsum("ab,krsdh->ksadbrh", eye, per)
    return big.reshape(2, CMP_STRIDE, KV_W, N_KV * CMP_RATIO * CMP_HID).astype(BF16)


def _rg_gate_weight(w_a, w_x):
    nb, bw, _ = w_a.shape
    per = MXU_DIM // bw
    n_grp = nb // per
    eye = jnp.eye(per, dtype=F32)

    def group(w):
        w = w.reshape(n_grp, per, bw, bw)
        return jnp.einsum("pq,gpde->gpdqe", eye, w).reshape(n_grp, per * bw, per * bw)

    return jnp.concatenate([group(w_a), group(w_x)], axis=2).astype(BF16)


def _feature_major(kv_rows):
    lead = kv_rows.shape[:-4]
    rows = kv_rows.shape[-4]
    nd = len(lead)
    perm = tuple(range(nd)) + (nd + 1, nd + 2, nd + 3, nd)
    return jnp.transpose(kv_rows, perm).reshape(lead + (2, KV_W, rows))


def _row_major_view(kv_t, b, rows):
    return jnp.transpose(kv_t.reshape(b, 2, N_KV, HEAD_DIM, rows), (0, 4, 1, 2, 3))


def kernel(x_prompt, x_sample, cache_cmp_kv, cache_sel_kv, cache_win_kv, state_conv, state_h, page_table, ffn1_w_gate, ffn1_w_up, ffn1_w_down, ln1_g, ln1_b, w_in, cmp_pe_k, cmp_w1_k, cmp_w2_k, cmp_pe_v, cmp_w1_v, cmp_w2_v, conv_w, conv_b, rg_w_a, rg_b_a, rg_w_x, rg_b_x, rg_lam, w_br_attn, w_br_rnn, w_out, ln2_g, ln2_b, ffn2_w_gate, ffn2_w_up, ffn2_w_down, ln3_g, ln3_b):
    bp, tp, d = x_prompt.shape
    bs, ts, _ = x_sample.shape
    depth = w_in.shape[0]
    d_rnn = conv_w.shape[2]
    n_pages = page_table.shape[1]
    past_len = n_pages * PAGE_SIZE
    assert ts == 1 and tp % CMP_STRIDE == 0 and tp % SEL_BLOCK == 0
    alpha = (2.0 * depth) ** 0.25
    wb = cache_win_kv.shape[2]
    wbp = min(WINDOW, tp)

    nch_p = tp // CMP_STRIDE
    nc_p = nch_p - CMP_RATIO + 1
    nsel_p = tp // SEL_BLOCK
    nch_s = past_len // CMP_STRIDE
    nc_s = nch_s - CMP_RATIO + 1
    nsel_s = -(-(past_len + ts) // SEL_BLOCK)
    n_past_blocks = past_len // SEL_BLOCK
    assert nsel_s >= N_SEL and n_pages % PAGES_PER_STEP == 0

    tm = min(256, tp)
    tq = min(512, tp)
    tt = min(256, tp)

    cos_p, sin_p = _rope_tables(jnp.arange(tp))
    cos_s, sin_s = _rope_tables(jnp.full((bs,), past_len))
    cover_p = _cover(nc_p, nsel_p, nch_p, nsel_p).T
    expand_p = jnp.asarray((np.arange(tp)[:, None] // SEL_BLOCK == np.arange(nsel_p)[None, :]).astype(np.float32)).astype(BF16)

    xp = x_prompt.reshape(bp * tp, d)
    xs = x_sample.reshape(bs * ts, d)
    outs = [[] for _ in range(10)]
    for l in range(depth):
        row = lambda v: v[l].reshape(1, -1)
        ffn1 = (ffn1_w_gate[l].astype(BF16), ffn1_w_up[l].astype(BF16), ffn1_w_down[l].astype(BF16), row(ln1_g), row(ln1_b))
        ffn2 = (ffn2_w_gate[l].astype(BF16), ffn2_w_up[l].astype(BF16), ffn2_w_down[l].astype(BF16), row(ln3_g), row(ln3_b))
        n_a = Q_W + 6 * KV_W + 3 * N_HEADS
        w_a = jnp.pad(w_in[l][:, :n_a], ((0, 0), (0, GATE_PAD - 3 * N_HEADS))).astype(BF16)
        w_b = w_in[l][:, n_a:].astype(BF16)
        w_cmp = _cmp_chunk_weight(cmp_w1_k[l], cmp_w1_v[l])
        cmpw = (cmp_pe_k[l].reshape(1, -1), cmp_w1_k[l], cmp_w2_k[l], cmp_pe_v[l].reshape(1, -1), cmp_w1_v[l], cmp_w2_v[l])
        mixw = (jnp.pad(conv_w[l], ((0, SUBLANES - CONV_W), (0, 0))), row(conv_b), _rg_gate_weight(rg_w_a[l], rg_w_x[l]),
                row(rg_b_a), row(rg_b_x), row(rg_lam), w_br_attn[l].astype(BF16), w_br_rnn[l].astype(BF16),
                w_out[l].astype(BF16), row(ln2_g), row(ln2_b))

        h1 = _ffn_half_step(xp, *ffn1, alpha=alpha, tm=tm)
        (cmp, sel_k, win_k, xr, gr, ga, gb, q_raw_t, q_rot_t, gates_t, cmp_t, sel_t, win_t) = _in_proj(
            h1, cos_p, sin_p, w_a, w_b, tm=tm, seq_len=tp, feature_major=True)
        b3 = lambda a: a.reshape(bp, tp, a.shape[-1])
        p1 = _cmp_stage1(b3(cmp), w_cmp)
        ckv, ckv_t = _cmp_finish(p1, *cmpw, nc=nc_p)
        o_attn = _nsa_prompt(q_raw_t, q_rot_t, gates_t, ckv, ckv_t, b3(sel_k), sel_t, b3(win_k), win_t,
                             cover_p, expand_p, tq=tq, nc=nc_p)
        h2, h_last = _mix_prompt(b3(xr), b3(gr), b3(ga), b3(gb), o_attn, b3(h1), mixw, tt=tt, alpha=alpha)
        xp = _ffn_half_step(h2.reshape(bp * tp, d), *ffn2, alpha=alpha, tm=tm)
        outs[0].append(_row_major_view(cmp_t, bp, tp))
        outs[2].append(_row_major_view(sel_t, bp, tp))
        outs[4].append(_row_major_view(win_t[:, :, tp - wbp:], bp, wbp))
        outs[6].append(b3(xr)[:, tp - (CONV_W - 1):])
        outs[8].append(h_last.reshape(bp, d_rnn))

        h1s = _ffn_half_step(xs, *ffn1, alpha=alpha, tm=bs)
        q_raw, q_rot, cmp, sel, win, gates, xr, gr, ga, gb = _in_proj(
            h1s, cos_s, sin_s, w_a, w_b, tm=bs, seq_len=bs, feature_major=False)
        b1 = lambda a: a.reshape(bs, 1, a.shape[-1])
        q_pos = past_len
        o_c, score = _cmp_sample(_feature_major(cache_cmp_kv[l]), page_table, w_cmp, cmpw, b1(q_raw),
                                 nc=nc_s, nsel=nsel_s, q_pos=q_pos)
        lanes = _top_blocks(score.reshape(bs * SUBLANES, score.shape[-1]))
        idx = lanes.reshape(bs, SUBLANES, LANES)[:, :N_KV, :N_SEL].reshape(bs, N_KV * N_SEL) // TOKENS_PER_BLOCK
        o_attn = _nsa_sample_attend(idx, page_table, b1(q_rot), b1(gates), o_c, b1(sel), b1(win),
                                    _feature_major(cache_win_kv[l]), _feature_major(cache_sel_kv[l]),
                                    q_pos=q_pos, n_past_blocks=n_past_blocks)
        conv_rows = [state_conv[l][:, k] for k in range(CONV_W - 1)]
        h2s, h_new = _mix_sample(xr, conv_rows, state_h[l], gr, ga, gb, o_attn.reshape(bs, Q_W), h1s, mixw, alpha=alpha)
        xs = _ffn_half_step(h2s, *ffn2, alpha=alpha, tm=bs)
        outs[1].append(cmp.reshape(bs, ts, 2, N_KV, HEAD_DIM))
        outs[3].append(sel.reshape(bs, ts, 2, N_KV, HEAD_DIM))
        win_all = jnp.concatenate([cache_win_kv[l], win.reshape(bs, ts, 2, N_KV, HEAD_DIM)], axis=1)
        outs[5].append(win_all[:, ts:])
        outs[7].append(jnp.concatenate([state_conv[l], xr[:, None, :]], axis=1)[:, ts:])
        outs[9].append(h_new)

    stacked = [jnp.stack(o) for o in outs]
    cmp_p, cmp_s, sel_p, sel_s, win_p, win_s, conv_p, conv_s, h_p, h_s = stacked
    return (xp.reshape(bp, tp, d), xs.reshape(bs, ts, d), cmp_p, cmp_s, sel_p, sel_s, win_p, win_s,
            conv_p, conv_s, h_p, h_s)
```

```python
import functools
import math

import numpy as np
import jax
import jax.numpy as jnp
from jax import lax
from jax.experimental import pallas as pl
from jax.experimental.pallas import tpu as pltpu

F32 = jnp.float32
BF16 = jnp.bfloat16

N_HEADS = 8
HEAD_DIM = 64
N_KV = 2
HPG = N_HEADS // N_KV
Q_W = N_HEADS * HEAD_DIM
KV_W = N_KV * HEAD_DIM
CMP_BLOCK = 32
CMP_STRIDE = 16
CMP_RATIO = CMP_BLOCK // CMP_STRIDE
CMP_HID = 64
SEL_BLOCK = 64
N_SEL = 16
SEL_BONUS = 1.0e4
WINDOW = 512
ROPE_THETA = 10000.0
RG_C = 8.0
CONV_W = 4
PAGE_SIZE = 128
LN_EPS = 1e-5
SCALE = HEAD_DIM ** -0.5
QK_SCALE_LOG2 = SCALE * math.log2(math.e)
NEG_INF = float("-inf")

LANES = 128
SUBLANES = 8
BF16_SUBLANES = 16
MXU_DIM = 256
VMEM_LIMIT_BYTES = 56 * 1024 * 1024
ROW_TILE = MXU_DIM
FFN_TILE = 2 * MXU_DIM
ATTN_TILE = 2 * MXU_DIM

ROW_W = 2 * KV_W
P_W = 2 * N_KV * CMP_RATIO * CMP_HID
GATE_PAD = LANES
PAGES_PER_STEP = 32
BLOCKS_PER_PAGE = PAGE_SIZE // SEL_BLOCK
TOKENS_PER_BLOCK = SEL_BLOCK // CMP_STRIDE


def _dot(a, b):
    return jnp.dot(a.astype(BF16), b.astype(BF16), preferred_element_type=F32)


def _dot_nt(a, b):
    return lax.dot_general(a.astype(BF16), b.astype(BF16), (((1,), (1,)), ((), ())),
                           preferred_element_type=F32)


def _dot_f32(a, b):
    return jnp.dot(a, b, preferred_element_type=F32, precision=lax.Precision.HIGHEST)


def _sigmoid(x):
    return 0.5 * jnp.tanh(0.5 * x) + 0.5


def _gelu(x):
    return 0.5 * x * (1.0 + jnp.tanh(math.sqrt(2.0 / math.pi) * (x + 0.044715 * (x * x * x))))


def _layer_norm(y, g, b):
    mu = jnp.mean(y, axis=-1, keepdims=True)
    yc = y - mu
    var = jnp.mean(yc * yc, axis=-1, keepdims=True)
    return yc * lax.rsqrt(var + LN_EPS) * g + b


def _softmax_parts(s, mask):
    s = jnp.where(mask, s, NEG_INF)
    m = jnp.max(s, axis=-1, keepdims=True)
    m = jnp.where(m > NEG_INF, m, 0.0)
    e = jnp.exp(s - m)
    d = jnp.sum(e, axis=-1, keepdims=True)
    return e, jnp.where(d > 0, d, 1.0)


def _const_spec(shape):
    nd = len(shape)
    return pl.BlockSpec(shape, lambda *_: (0,) * nd)


def _resident_spec(shape):
    nd = len(shape)
    return pl.BlockSpec(shape, lambda *_: (0,) * nd, pipeline_mode=pl.Buffered(1))


def _params(*sem):
    return pltpu.CompilerParams(dimension_semantics=sem, vmem_limit_bytes=VMEM_LIMIT_BYTES)


def _ffn_body(x_ref, wg_ref, wu_ref, wd_ref, g_ref, b_ref, o_ref, *, alpha):
    x = x_ref[...]
    xb = x.astype(BF16)
    gate = jnp.dot(xb, wg_ref[...], preferred_element_type=F32)
    up = jnp.dot(xb, wu_ref[...], preferred_element_type=F32)
    hmid = gate * _sigmoid(gate) * up
    f = _dot(hmid, wd_ref[...])
    o_ref[...] = _layer_norm(alpha * x + 0.5 * f, g_ref[...], b_ref[...])


def _ffn_half_step(x, wg, wu, wd, g, b, *, alpha, tm):
    n, d = x.shape
    ff = wg.shape[1]
    return pl.pallas_call(
        functools.partial(_ffn_body, alpha=alpha),
        grid=(n // tm,),
        in_specs=[pl.BlockSpec((tm, d), lambda i: (i, 0)),
                  _resident_spec((d, ff)), _resident_spec((d, ff)), _resident_spec((ff, d)),
                  _const_spec((1, d)), _const_spec((1, d))],
        out_specs=pl.BlockSpec((tm, d), lambda i: (i, 0)),
        out_shape=jax.ShapeDtypeStruct((n, d), F32),
        compiler_params=_params("parallel"),
        name="ffn_half_step",
    )(x, wg, wu, wd, g, b)


def _rope(x, cos, sin):
    w = x.shape[1]
    reps = w // LANES
    c = jnp.tile(cos, (1, reps)) if reps > 1 else cos
    s = jnp.tile(sin, (1, reps)) if reps > 1 else sin
    lane = lax.broadcasted_iota(jnp.int32, x.shape, 1)
    first_half = (lane % HEAD_DIM) < (HEAD_DIM // 2)
    swapped = jnp.where(first_half, pltpu.roll(x, w - HEAD_DIM // 2, 1), pltpu.roll(x, HEAD_DIM // 2, 1))
    return x * c + swapped * s


def _inproj_body(h_ref, cos_ref, sin_ref, wa_ref, wb_ref, *out_refs, feature_major):
    hb = h_ref[...].astype(BF16)
    cos = cos_ref[...]
    sin = sin_ref[...]
    za = jnp.dot(hb, wa_ref[...], preferred_element_type=F32)
    q = za[:, :Q_W]
    kv = za[:, Q_W:Q_W + 6 * KV_W]
    gates = _sigmoid(za[:, Q_W + 6 * KV_W:])
    q_rot = _rope(q, cos, sin)
    cmp = kv[:, :2 * KV_W]
    sel_k = _rope(kv[:, 2 * KV_W:3 * KV_W], cos, sin)
    win_k = _rope(kv[:, 4 * KV_W:5 * KV_W], cos, sin)
    sel = jnp.concatenate([sel_k, kv[:, 3 * KV_W:4 * KV_W]], axis=1)
    win = jnp.concatenate([win_k, kv[:, 5 * KV_W:6 * KV_W]], axis=1)
    if feature_major:
        (cmp_ref, selk_ref, wink_ref, xr_ref, gr_ref, ga_ref, gb_ref,
         qrawt_ref, qrott_ref, gatest_ref, cmpt_ref, selt_ref, wint_ref) = out_refs
        cmp_ref[...] = cmp
        selk_ref[...] = sel_k.astype(BF16)
        wink_ref[...] = win_k.astype(BF16)
        qrawt_ref[...] = q.T
        qrott_ref[...] = q_rot.T
        gatest_ref[...] = gates.T
        cmpt_ref[...] = cmp.T
        selt_ref[...] = sel.T
        wint_ref[...] = win.T
    else:
        qraw_ref, qrot_ref, cmp_ref, sel_ref, win_ref, gates_ref, xr_ref, gr_ref, ga_ref, gb_ref = out_refs
        qraw_ref[...] = q
        qrot_ref[...] = q_rot
        cmp_ref[...] = cmp
        sel_ref[...] = sel
        win_ref[...] = win
        gates_ref[...] = gates
    zb = jnp.dot(hb, wb_ref[...], preferred_element_type=F32)
    d = xr_ref.shape[1]
    xr_ref[...] = zb[:, :d]
    gr_ref[...] = zb[:, d:2 * d]
    ga_ref[...] = zb[:, 2 * d:3 * d]
    gb_ref[...] = zb[:, 3 * d:]


def _in_proj(h, cos, sin, wa, wb, *, tm, seq_len, feature_major):
    n, d = h.shape
    wa_w, wb_w = wa.shape[1], wb.shape[1]
    tiles = seq_len // tm
    row = lambda i: (i, 0)
    tab = lambda i: (i % tiles, 0)
    if feature_major:
        rows = [(ROW_W, F32), (KV_W, BF16), (KV_W, BF16), (d, F32), (d, F32), (d, F32), (d, F32)]
        cols = [Q_W, Q_W, GATE_PAD, ROW_W, ROW_W, ROW_W]
    else:
        rows = [(w, F32) for w in (Q_W, Q_W, ROW_W, ROW_W, ROW_W, GATE_PAD, d, d, d, d)]
        cols = []
    out_specs = [pl.BlockSpec((tm, w), row) for w, _ in rows]
    out_shape = [jax.ShapeDtypeStruct((n, w), dt) for w, dt in rows]
    out_specs += [pl.BlockSpec((None, w, tm), lambda i: (i // tiles, 0, i % tiles)) for w in cols]
    out_shape += [jax.ShapeDtypeStruct((n // seq_len, w, seq_len), F32) for w in cols]
    return pl.pallas_call(
        functools.partial(_inproj_body, feature_major=feature_major),
        grid=(n // tm,),
        in_specs=[pl.BlockSpec((tm, d), row), pl.BlockSpec((tm, LANES), tab), pl.BlockSpec((tm, LANES), tab),
                  _const_spec((d, wa_w)), _const_spec((d, wb_w))],
        out_specs=out_specs,
        out_shape=out_shape,
        compiler_params=_params("parallel"),
        name="in_proj",
    )(h, cos, sin, wa, wb)


def _cmp_chunks(rows_of, w_ref, kv, nch):
    acc = None
    for s in range(CMP_STRIDE):
        part = _dot(rows_of(s, nch), w_ref[kv, s])
        acc = part if acc is None else acc + part
    return acc


def _cmp_stage1_body(x_ref, w_ref, o_ref, rows_ref):
    nch = x_ref.shape[0] // CMP_STRIDE
    half = P_W // 2
    for kv in range(2):
        rows_ref[kv] = x_ref[:, kv * KV_W:(kv + 1) * KV_W]
        rows_of = lambda s, n, kv=kv: rows_ref[kv, pl.ds(s, n, stride=CMP_STRIDE), :]
        o_ref[:, kv * half:(kv + 1) * half] = _cmp_chunks(rows_of, w_ref, kv, nch)


def _cmp_stage1(x, w):
    b, t, _ = x.shape
    nch = t // CMP_STRIDE
    return pl.pallas_call(
        _cmp_stage1_body,
        grid=(b,),
        in_specs=[pl.BlockSpec((None, t, ROW_W), lambda i: (i, 0, 0)), _const_spec(w.shape)],
        out_specs=pl.BlockSpec((None, nch, P_W), lambda i: (i, 0, 0)),
        out_shape=jax.ShapeDtypeStruct((b, nch, P_W), F32),
        scratch_shapes=[pltpu.VMEM((2, t, KV_W), F32)],
        compiler_params=_params("parallel"),
        name="cmp_stage1",
    )(x, w)


def _cmp_sample_body(pt_ref, *refs, nc, nsel, q_pos):
    del pt_ref
    x_refs = refs[:PAGES_PER_STEP]
    (perm_ref, w_ref, pek_ref, w1k_ref, w2k_ref, pev_ref, w1v_ref, w2v_ref, qraw_ref,
     oc_ref, score_ref, planes_ref, stage_ref, p_ref) = refs[PAGES_PER_STEP:]
    step = pl.program_id(1)
    cpp = PAGE_SIZE // CMP_STRIDE
    nch_step = PAGES_PER_STEP * cpp
    perm = perm_ref[...]
    for k in range(0, PAGES_PER_STEP, 2):
        for kv in range(2):
            pair = jnp.concatenate([x_refs[k][kv], x_refs[k + 1][kv]], axis=1).astype(BF16)
            stage_ref[k // 2, kv] = jnp.dot(pair, perm, preferred_element_type=F32)
            for half in range(2):
                rows = stage_ref[k // 2, kv, :, half * PAGE_SIZE:(half + 1) * PAGE_SIZE].T
                first = (k + half) * cpp
                for s in range(CMP_STRIDE):
                    planes_ref[kv, s, first:first + cpp, :] = rows[s * cpp:(s + 1) * cpp]
    half = P_W // 2
    row0 = pl.multiple_of(step * nch_step, nch_step)
    for kv in range(2):
        acc = None
        for j in range(CMP_STRIDE // 2):
            pair = jnp.concatenate([planes_ref[kv, 2 * j], planes_ref[kv, 2 * j + 1]], axis=1)
            part = _dot(pair, w_ref[kv, j])
            acc = part if acc is None else acc + part
        p_ref[pl.ds(row0, nch_step), kv * half:(kv + 1) * half] = acc

    @pl.when(step == pl.num_programs(1) - 1)
    def _():
        ckv = _cmp_second_layer(p_ref[...], pek_ref, w1k_ref, w2k_ref, pev_ref, w1v_ref, w2v_ref, nc)
        oc_ref[...], score_ref[...] = _sample_scores(qraw_ref[...], ckv, score_ref.shape[1], nc, nsel, q_pos)


def _cmp_sample(cache_t, page_table, w, cmpw, q_raw, *, nc, nsel, q_pos):
    b, n_pages = page_table.shape
    steps = n_pages // PAGES_PER_STEP
    cpp = PAGE_SIZE // CMP_STRIDE
    nch = n_pages * cpp
    width = max(nch + LANES, -(-(TOKENS_PER_BLOCK * nsel) // LANES) * LANES)
    w = w.reshape(2, CMP_STRIDE // 2, 2 * KV_W, w.shape[-1])
    row = np.arange(PAGE_SIZE)
    perm = np.zeros((PAGE_SIZE, PAGE_SIZE), np.float32)
    perm[row, (row % CMP_STRIDE) * cpp + row // CMP_STRIDE] = 1.0
    perm = jnp.asarray(np.kron(np.eye(2, dtype=np.float32), perm)).astype(BF16)

    def page_spec(k):
        return pl.BlockSpec((None, 2, KV_W, PAGE_SIZE), lambda i, s, pt: (pt[i, s * PAGES_PER_STEP + k], 0, 0, 0))

    const = lambda a: pl.BlockSpec(a.shape, lambda i, s, pt: (0,) * a.ndim)
    per_request = lambda r, c: pl.BlockSpec((None, r, c), lambda i, s, pt: (i, 0, 0))
    return pl.pallas_call(
        functools.partial(_cmp_sample_body, nc=nc, nsel=nsel, q_pos=q_pos),
        grid_spec=pltpu.PrefetchScalarGridSpec(
            num_scalar_prefetch=1,
            grid=(b, steps),
            in_specs=[page_spec(k) for k in range(PAGES_PER_STEP)]
            + [const(perm), const(w)] + [const(a) for a in cmpw] + [per_request(1, Q_W)],
            out_specs=[per_request(1, Q_W), per_request(SUBLANES, width)],
            scratch_shapes=[pltpu.VMEM((2, CMP_STRIDE, PAGES_PER_STEP * cpp, KV_W), F32),
                            pltpu.VMEM((PAGES_PER_STEP // 2, 2, KV_W, 2 * PAGE_SIZE), F32),
                            pltpu.VMEM((nch, P_W), F32)],
        ),
        out_shape=[jax.ShapeDtypeStruct((b, 1, Q_W), F32), jax.ShapeDtypeStruct((b, SUBLANES, width), F32)],
        compiler_params=_params("parallel", "arbitrary"),
        name="cmp_sample",
    )(page_table, *([cache_t] * PAGES_PER_STEP), perm, w, *cmpw, q_raw)


def _cmp_second_layer(p, pek_ref, w1k_ref, w2k_ref, pev_ref, w1v_ref, w2v_ref, nc):
    nch = p.shape[0]
    row = lax.broadcasted_iota(jnp.int32, (nch, CMP_HID), 0)
    outs = []
    for kv, (pe_ref, w1_ref, w2_ref) in enumerate(((pek_ref, w1k_ref, w2k_ref), (pev_ref, w1v_ref, w2v_ref))):
        pe = jnp.broadcast_to(pe_ref[...], (SUBLANES, pe_ref.shape[1]))
        bias = _dot(pe, w1_ref[...])[0:1]
        for g in range(N_KV):
            c0 = (kv * N_KV + g) * CMP_RATIO * CMP_HID
            u = bias + p[:, c0:c0 + CMP_HID]
            for r in range(1, CMP_RATIO):
                u = u + pltpu.roll(p[:, c0 + r * CMP_HID:c0 + (r + 1) * CMP_HID], nch - r, 0)
            out = _dot(_gelu(u), w2_ref[...])
            outs.append(jnp.where(row < nc, out, 0.0))
    return jnp.concatenate(outs, axis=1)


def _cmp_finish_body(p_ref, pek_ref, w1k_ref, w2k_ref, pev_ref, w1v_ref, w2v_ref, o_ref, ot_ref, *, nc):
    ckv = _cmp_second_layer(p_ref[...], pek_ref, w1k_ref, w2k_ref, pev_ref, w1v_ref, w2v_ref, nc)
    o_ref[...] = ckv
    ot_ref[...] = ckv.T


def _cmp_finish(p, pe_k, w1_k, w2_k, pe_v, w1_v, w2_v, *, nc):
    b, nch, _ = p.shape
    flat = CMP_BLOCK * HEAD_DIM
    return pl.pallas_call(
        functools.partial(_cmp_finish_body, nc=nc),
        grid=(b,),
        in_specs=[pl.BlockSpec((None, nch, P_W), lambda i: (i, 0, 0)),
                  _const_spec((1, flat)), _const_spec((flat, CMP_HID)), _const_spec((CMP_HID, HEAD_DIM)),
                  _const_spec((1, flat)), _const_spec((flat, CMP_HID)), _const_spec((CMP_HID, HEAD_DIM))],
        out_specs=[pl.BlockSpec((None, nch, ROW_W), lambda i: (i, 0, 0)),
                   pl.BlockSpec((None, ROW_W, nch), lambda i: (i, 0, 0))],
        out_shape=[jax.ShapeDtypeStruct((b, nch, ROW_W), F32), jax.ShapeDtypeStruct((b, ROW_W, nch), F32)],
        compiler_params=_params("parallel"),
        name="cmp_finish",
    )(p, pe_k, w1_k, w2_k, pe_v, w1_v, w2_v)


def _block_scores(imp, q_pos, nsel_valid, block_axis=1):
    jblk = lax.broadcasted_iota(jnp.int32, imp.shape, block_axis)
    cur = q_pos // SEL_BLOCK
    valid = (jblk * SEL_BLOCK <= q_pos) & (jblk < nsel_valid)
    forced = (jblk == 0) | (jblk == cur) | (jblk == cur - 1)
    return jnp.where(valid, imp + jnp.where(forced, SEL_BONUS, 0.0), NEG_INF)


def _nsa_prompt_body(qrawt_ref, qrott_ref, gatest_ref, ckv_ref, ckvt_ref, selk_ref, selt_ref, wink_ref, wint_ref,
                     covert_ref, expandt_ref, o_ref, *, tq, nc):
    nchp = ckv_ref.shape[0]
    nsel = covert_ref.shape[0]
    tile = pl.program_id(1)
    s0 = tile * tq
    lane_t = lax.broadcasted_iota(jnp.int32, (1, tq), 1)
    key_sub = lax.broadcasted_iota(jnp.int32, (tq, 1), 0)
    qpos = s0 + lane_t
    causal_diag = key_sub <= lane_t
    window_edge = lane_t <= key_sub
    cidx = lax.broadcasted_iota(jnp.int32, (nchp, 1), 0)
    jblk = lax.broadcasted_iota(jnp.int32, (nsel, 1), 0)
    n_win_chunks = WINDOW // tq
    vrows = [slice(KV_W + g * HEAD_DIM, KV_W + (g + 1) * HEAD_DIM) for g in range(N_KV)]
    ones_rows = jnp.ones((BF16_SUBLANES, tq), BF16)
    group_of = lambda hd: hd // HPG

    def group_rows(qt, g):
        z = jnp.zeros_like(qt)
        return jnp.concatenate([qt, z] if g == 0 else [z, qt], axis=0)

    def head_rows(ref, hd):
        return group_rows((ref[hd * HEAD_DIM:(hd + 1) * HEAD_DIM, :] * QK_SCALE_LOG2).astype(BF16), group_of(hd))

    q_rot = [head_rows(qrott_ref, hd) for hd in range(N_HEADS)]
    q_raw = jnp.concatenate([head_rows(qrawt_ref, hd) for hd in range(N_HEADS)], axis=1)
    s_c = jnp.dot(ckv_ref[:, :KV_W].astype(BF16), q_raw, preferred_element_type=F32)
    m_c = (cidx * CMP_STRIDE + (CMP_BLOCK - 1) <= jnp.concatenate([qpos] * N_HEADS, axis=1)) & (cidx < nc)
    s_c = jnp.where(m_c, s_c, NEG_INF)
    mx = jnp.max(s_c, axis=0, keepdims=True)
    e_c = jnp.exp2(s_c - jnp.where(mx > NEG_INF, mx, 0.0))
    d_c = jnp.sum(e_c, axis=0, keepdims=True)
    p_c = e_c * (1.0 / jnp.where(d_c > 0, d_c, 1.0))
    head_cols = lambda hd: slice(hd * tq, (hd + 1) * tq)
    o_c = [_dot(ckvt_ref[vrows[group_of(hd)], :], p_c[:, head_cols(hd)]) for hd in range(N_HEADS)]
    p_sum = []
    for g in range(N_KV):
        acc = p_c[:, head_cols(g * HPG)]
        for hd in range(g * HPG + 1, (g + 1) * HPG):
            acc = acc + p_c[:, head_cols(hd)]
        p_sum.append(acc)
    imp = _dot_f32(covert_ref[...], jnp.concatenate(p_sum, axis=1))
    score = _block_scores(imp, jnp.concatenate([qpos] * N_KV, axis=1), nsel, block_axis=0)
    rank = jnp.zeros(score.shape, jnp.int32)
    for i in range(nsel):
        s_i = score[i:i + 1, :]
        beats = (s_i > score) | ((s_i == score) & (i < jblk))
        rank = rank + beats.astype(jnp.int32)
    chosen = (rank < N_SEL).astype(BF16)

    def picked(k0):
        hit = jnp.dot(expandt_ref[pl.ds(k0, tq), :], chosen, preferred_element_type=F32)
        return [hit[:, g * tq:(g + 1) * tq] > 0.5 for g in range(N_KV)]

    def attend(jobs):
        loaded = []
        for k_ref, vt_ref, kc, masks_of, _ in jobs:
            k0 = pl.multiple_of(kc * tq, tq)
            keys = k_ref[pl.ds(k0, tq), :]
            values_t = [jnp.concatenate([vt_ref[rows, pl.ds(k0, tq)].astype(BF16), ones_rows], axis=0)
                        for rows in vrows]
            loaded.append((keys, values_t, masks_of(k0)))
        scores = [[jnp.dot(keys, q_rot[hd], preferred_element_type=F32) for hd in range(N_HEADS)]
                  for keys, _, _ in loaded]
        updates = []
        for job, (_, _, masks), sc in zip(jobs, loaded, scores):
            carry = job[4]
            stats, probs = [], []
            for hd in range(N_HEADS):
                m = carry[hd][0]
                s = sc[hd] if masks is None else jnp.where(masks[group_of(hd)], sc[hd], NEG_INF)
                m_new = jnp.maximum(m, jnp.max(s, axis=0, keepdims=True))
                m_safe = m_new if masks is None else jnp.where(m_new > NEG_INF, m_new, 0.0)
                alpha = jnp.exp2(m - m_safe)
                stats.append((m_new, alpha))
                probs.append(jnp.exp2(s - m_safe).astype(BF16))
            updates.append((stats, probs))
        out = []
        for job, (_, values_t, _), (stats, probs) in zip(jobs, loaded, updates):
            carry = job[4]
            pv = [jnp.dot(values_t[group_of(hd)], probs[hd], preferred_element_type=F32) for hd in range(N_HEADS)]
            out.append(tuple((stats[hd][0],
                              carry[hd][1] * stats[hd][1] + pv[hd][HEAD_DIM:HEAD_DIM + 1],
                              carry[hd][2] * stats[hd][1] + pv[hd][:HEAD_DIM]) for hd in range(N_HEADS)))
        return out

    def finish(carry):
        return [acc * (1.0 / jnp.where(l > 0, l, 1.0)) for _, l, acc in carry]

    init = tuple((jnp.full((1, tq), NEG_INF, F32), jnp.zeros((1, tq), F32), jnp.zeros((HEAD_DIM, tq), F32))
                 for _ in range(N_HEADS))
    no_mask = lambda k0: None
    sel = lax.fori_loop(0, tile, lambda kc, c: attend([(selk_ref, selt_ref, kc, picked, c)])[0], init)
    edge = tile - n_win_chunks
    win = lax.fori_loop(jnp.maximum(edge, 0), jnp.maximum(edge + 1, 0),
                        lambda kc, c: attend([(wink_ref, wint_ref, kc, lambda k0: [window_edge] * N_KV, c)])[0], init)
    win = lax.fori_loop(jnp.maximum(edge + 1, 0), tile,
                        lambda kc, c: attend([(wink_ref, wint_ref, kc, no_mask, c)])[0], win)
    sel, win = attend([(selk_ref, selt_ref, tile, lambda k0: [hit & causal_diag for hit in picked(k0)], sel),
                       (wink_ref, wint_ref, tile, lambda k0: [causal_diag] * N_KV, win)])
    o_s, o_w = finish(sel), finish(win)
    out_heads = []
    for hd in range(N_HEADS):
        gate = lambda branch: gatest_ref[3 * hd + branch:3 * hd + branch + 1, :]
        out_heads.append(gate(0) * o_c[hd] + gate(1) * o_s[hd] + gate(2) * o_w[hd])
    o_ref[...] = jnp.concatenate(out_heads, axis=0).T


def _nsa_prompt(q_raw_t, q_rot_t, gates_t, ckv, ckv_t, sel_k, sel_t, win_k, win_t, cover_t, expand_t, *, tq, nc):
    b, _, t = q_raw_t.shape
    nchp = ckv.shape[1]
    nsel = cover_t.shape[0]
    assert WINDOW % tq == 0
    tile_t = lambda w: pl.BlockSpec((None, w, tq), lambda i, j: (i, 0, j))
    whole = lambda r, w: pl.BlockSpec((None, r, w), lambda i, j: (i, 0, 0))
    return pl.pallas_call(
        functools.partial(_nsa_prompt_body, tq=tq, nc=nc),
        grid=(b, t // tq),
        in_specs=[tile_t(Q_W), tile_t(Q_W), tile_t(GATE_PAD), whole(nchp, ROW_W), whole(ROW_W, nchp),
                  whole(t, KV_W), whole(ROW_W, t), whole(t, KV_W), whole(ROW_W, t),
                  _const_spec((nsel, nchp)), _const_spec((t, nsel))],
        out_specs=pl.BlockSpec((None, tq, Q_W), lambda i, j: (i, j, 0)),
        out_shape=jax.ShapeDtypeStruct((b, t, Q_W), F32),
        compiler_params=_params("parallel", "arbitrary"),
        name="nsa_prompt",
    )(q_raw_t, q_rot_t, gates_t, ckv, ckv_t, sel_k, sel_t, win_k, win_t, cover_t, expand_t)


def _softplus(x):
    return jnp.maximum(x, 0.0) + jnp.log1p(jnp.exp(-jnp.abs(x)))


def _rg_gates(xc, wgate_half_ref, ba_half, bx_half, lam):
    xcb = xc.astype(BF16)
    n_grp = wgate_half_ref.shape[0]
    gw = wgate_half_ref.shape[1]
    za, zx = [], []
    for k in range(n_grp):
        z = jnp.dot(xcb[:, k * gw:(k + 1) * gw], wgate_half_ref[k], preferred_element_type=F32)
        za.append(z[:, :gw])
        zx.append(z[:, gw:])
    t_r = jnp.tanh(jnp.concatenate(za, axis=1) + ba_half)
    t_i = jnp.tanh(jnp.concatenate(zx, axis=1) + bx_half)
    log_a = (1.0 + t_r) * (-0.5 * RG_C * _softplus(-lam))
    a = jnp.exp(log_a)
    u = jnp.sqrt(-0.25 * jnp.tanh(log_a) * (1.0 + a * a)) * (1.0 + t_i) * xc
    return a, u


def _gelu_of_half(xh):
    c = math.sqrt(2.0 / math.pi)
    return xh * (1.0 + jnp.tanh(xh * (2.0 * c + (8.0 * c * 0.044715) * (xh * xh))))


def _merge(h_res, hs, gr_half, ga_half, gb_half, o_attn, wbra_half_ref, wbrr_half_ref, wout_ref, g2, b2, alpha):
    y_rnn = hs * _gelu_of_half(gr_half)
    m = ((1.0 + jnp.tanh(ga_half)) * _dot(o_attn, wbra_half_ref[...])
         + (1.0 + jnp.tanh(gb_half)) * _dot(y_rnn, wbrr_half_ref[...]))
    return _layer_norm(alpha * h_res + _dot(m, wout_ref[...]), g2, b2)


def _mix_prompt_body(xr_ref, gr_ref, ga_ref, gb_ref, oat_ref, h1_ref, convw_ref, convb_ref, wgate_ref,
                     ba_ref, bx_ref, lam_ref, wbra_ref, wbrr_ref, wout_ref, g2_ref, b2_ref,
                     o_ref, hlast_ref, hc_ref, tail_ref, *, tt, alpha):
    @pl.when(pl.program_id(1) == 0)
    def _():
        hc_ref[...] = jnp.zeros_like(hc_ref)
        tail_ref[...] = jnp.zeros_like(tail_ref)

    x = xr_ref[...]
    convw = convw_ref[...]
    sub = lax.broadcasted_iota(jnp.int32, (SUBLANES, x.shape[1]), 0)
    shifts = range(1, CONV_W)
    prev_rot = [pltpu.roll(tail_ref[...], d, 0) for d in shifts]
    xc_groups = []
    for k in range(tt // SUBLANES):
        x_grp = x[k * SUBLANES:(k + 1) * SUBLANES]
        rot = [pltpu.roll(x_grp, d, 0) for d in shifts]
        xc_grp = convb_ref[...] + convw[CONV_W - 1:CONV_W] * x_grp
        for d, r_prev, r_cur in zip(shifts, prev_rot, rot):
            xc_grp = xc_grp + convw[CONV_W - 1 - d:CONV_W - d] * jnp.where(sub < d, r_prev, r_cur)
        xc_groups.append(xc_grp)
        prev_rot = rot
    xc = jnp.concatenate(xc_groups, axis=0)
    tail_ref[...] = x[tt - SUBLANES:tt]

    a, u = _rg_gates(xc, wgate_ref, ba_ref[...], bx_ref[...], lam_ref[...])
    sub = lax.broadcasted_iota(jnp.int32, (SUBLANES, a.shape[1]), 0)
    h_prev = jnp.broadcast_to(hc_ref[...], sub.shape)
    groups = []
    for k in range(tt // SUBLANES):
        rows = slice(k * SUBLANES, (k + 1) * SUBLANES)
        a_grp, u_grp = a[rows], u[rows]
        d = 1
        while d < SUBLANES:
            inside = sub >= d
            u_grp = jnp.where(inside, a_grp * pltpu.roll(u_grp, d, 0) + u_grp, u_grp)
            a_grp = jnp.where(inside, a_grp * pltpu.roll(a_grp, d, 0), a_grp)
            d *= 2
        h_grp = a_grp * h_prev + u_grp
        groups.append(h_grp)
        h_prev = jnp.broadcast_to(h_grp[SUBLANES - 1:SUBLANES], h_grp.shape)
    hs = jnp.concatenate(groups, axis=0)
    hc_ref[...] = hs[tt - 1:tt]
    hlast_ref[...] = hs[tt - 1:tt]
    o_ref[...] = _merge(h1_ref[...], hs, gr_ref[...], ga_ref[...], gb_ref[...], oat_ref[...],
                        wbra_ref, wbrr_ref, wout_ref, g2_ref[...], b2_ref[...], alpha)


def _mix_weight_specs(d, n_grp, gw):
    return [_const_spec((SUBLANES, d)), _const_spec((1, d)), _const_spec((n_grp, gw, 2 * gw)),
            _const_spec((1, d)), _const_spec((1, d)), _const_spec((1, d)),
            _const_spec((Q_W, d)), _const_spec((d, d)), _const_spec((d, d)),
            _const_spec((1, d)), _const_spec((1, d))]


def _mix_prompt(xr, gr, ga, gb, o_attn, h1, mixw, *, tt, alpha):
    b, t, d = xr.shape
    n_grp, gw = mixw[2].shape[:2]
    tile = lambda w: pl.BlockSpec((None, tt, w), lambda i, j: (i, j, 0))
    return pl.pallas_call(
        functools.partial(_mix_prompt_body, tt=tt, alpha=alpha),
        grid=(b, t // tt),
        in_specs=[tile(d), tile(d), tile(d), tile(d), tile(Q_W), tile(d)] + _mix_weight_specs(d, n_grp, gw),
        out_specs=[tile(d), pl.BlockSpec((None, 1, d), lambda i, j: (i, 0, 0))],
        out_shape=[jax.ShapeDtypeStruct((b, t, d), F32), jax.ShapeDtypeStruct((b, 1, d), F32)],
        scratch_shapes=[pltpu.VMEM((1, d), F32), pltpu.VMEM((SUBLANES, d), F32)],
        compiler_params=_params("parallel", "arbitrary"),
        name="mix_prompt",
    )(xr, gr, ga, gb, o_attn, h1, *mixw)


def _mix_sample_body(xr_ref, c0_ref, c1_ref, c2_ref, h0_ref, gr_ref, ga_ref, gb_ref, oat_ref, h1_ref,
                     convw_ref, convb_ref, wgate_ref, ba_ref, bx_ref, lam_ref, wbra_ref, wbrr_ref, wout_ref,
                     g2_ref, b2_ref, o_ref, hnew_ref, *, alpha):
    convw = convw_ref[...]
    xc = (convb_ref[...] + convw[0:1] * c0_ref[...] + convw[1:2] * c1_ref[...] + convw[2:3] * c2_ref[...]
          + convw[3:4] * xr_ref[...])
    a, u = _rg_gates(xc, wgate_ref, ba_ref[...], bx_ref[...], lam_ref[...])
    hs = a * h0_ref[...] + u
    hnew_ref[...] = hs
    o_ref[...] = _merge(h1_ref[...], hs, gr_ref[...], ga_ref[...], gb_ref[...], oat_ref[...],
                        wbra_ref, wbrr_ref, wout_ref, g2_ref[...], b2_ref[...], alpha)


def _mix_sample(xr, conv_rows, h0, gr, ga, gb, o_attn, h1, mixw, *, alpha):
    b, d = xr.shape
    n_grp, gw = mixw[2].shape[:2]
    full = lambda w: _const_spec((b, w))
    return pl.pallas_call(
        functools.partial(_mix_sample_body, alpha=alpha),
        grid=(1,),
        in_specs=[full(d)] * 8 + [full(Q_W), full(d)] + _mix_weight_specs(d, n_grp, gw),
        out_specs=[full(d), full(d)],
        out_shape=[jax.ShapeDtypeStruct((b, d), F32), jax.ShapeDtypeStruct((b, d), F32)],
        compiler_params=_params("arbitrary"),
        name="mix_sample",
    )(xr, *conv_rows, h0, gr, ga, gb, o_attn, h1, *mixw)


def _head_rows(row, g):
    parts = [row[:, (g * HPG + h) * HEAD_DIM:(g * HPG + h + 1) * HEAD_DIM] for h in range(HPG)]
    return jnp.concatenate(parts + [jnp.zeros((SUBLANES - HPG, HEAD_DIM), F32)], axis=0)


def _sample_scores(qrow, ckv, width, nc, nsel, q_pos):
    nchp = ckv.shape[0]
    cidx = lax.broadcasted_iota(jnp.int32, (1, nchp), 1)
    qrow = qrow * SCALE
    m_c = (cidx * CMP_STRIDE + (CMP_BLOCK - 1) <= q_pos) & (cidx < nc)
    lane = lax.broadcasted_iota(jnp.int32, (1, width), 1)
    jblk = lane // TOKENS_PER_BLOCK
    cur = q_pos // SEL_BLOCK
    on_block = (lane % TOKENS_PER_BLOCK == 0) & (jblk < nsel) & (jblk * SEL_BLOCK <= q_pos)
    bonus = jnp.where((jblk == 0) | (jblk == cur) | (jblk == cur - 1), SEL_BONUS, 0.0)
    oc_parts, score_rows = [], []
    for g in range(N_KV):
        q = _head_rows(qrow, g)
        s_c = _dot_nt(q, ckv[:, g * HEAD_DIM:(g + 1) * HEAD_DIM])
        e_c, d_c = _softmax_parts(s_c, m_c)
        p_c = e_c / d_c
        o_c = _dot(p_c, ckv[:, KV_W + g * HEAD_DIM:KV_W + (g + 1) * HEAD_DIM])
        oc_parts += [o_c[h:h + 1] for h in range(HPG)]
        p_sum = jnp.broadcast_to(jnp.sum(p_c[0:HPG], axis=0, keepdims=True), (SUBLANES, nchp))
        p_sum = jnp.concatenate([p_sum, jnp.zeros((SUBLANES, width - nchp), F32)], axis=1)
        imp = p_sum
        for back in range(1, CMP_RATIO):
            imp = imp + pltpu.roll(p_sum, back, 1)
        for ahead in range(1, TOKENS_PER_BLOCK):
            imp = imp + pltpu.roll(p_sum, width - ahead, 1)
        score_rows.append(jnp.where(on_block, imp[0:1] + bonus, NEG_INF))
    score = jnp.concatenate(score_rows + [jnp.full((SUBLANES - N_KV, width), NEG_INF, F32)], axis=0)
    return jnp.concatenate(oc_parts, axis=1), score


def _top_blocks_body(score_ref, idx_ref):
    score = score_ref[...]
    n, nselp = score.shape
    lane = lax.broadcasted_iota(jnp.int32, (n, nselp), 1)
    lane_out = lax.broadcasted_iota(jnp.int32, (n, LANES), 1)
    picked = jnp.zeros((n, LANES), jnp.int32)
    for r in range(N_SEL):
        m = jnp.max(score, axis=1, keepdims=True)
        j = jnp.min(jnp.where(score == m, lane, nselp), axis=1, keepdims=True)
        picked = jnp.where(lane_out == r, j, picked)
        score = jnp.where(lane == j, NEG_INF, score)
    idx_ref[...] = picked


def _top_blocks(score):
    n, nselp = score.shape
    return pl.pallas_call(
        _top_blocks_body,
        grid=(1,),
        in_specs=[_const_spec((n, nselp))],
        out_specs=_const_spec((n, LANES)),
        out_shape=jax.ShapeDtypeStruct((n, LANES), jnp.int32),
        compiler_params=_params("arbitrary"),
        name="top_blocks",
    )(score)


def _nsa_sample_attend_body(idx_ref, pt_ref, qrot_ref, gates_ref, oc_ref, selnew_ref, winnew_ref, wincache_ref,
                            *refs, q_pos, n_past_blocks):
    del pt_ref
    page_refs, o_ref = refs[:-1], refs[-1]
    b = pl.program_id(0)
    wb = wincache_ref.shape[2]
    qrow = qrot_ref[...] * SCALE
    gates = gates_ref[...]
    oc = oc_ref[...]
    sel_new = selnew_ref[...]
    win_new = winnew_ref[...]
    page_lane = lax.broadcasted_iota(jnp.int32, (1, PAGE_SIZE), 1)
    wpos = (q_pos - wb) + lax.broadcasted_iota(jnp.int32, (1, wb), 1)
    m_w = (q_pos - wpos <= WINDOW) & (wpos >= 0) & (wpos <= q_pos)
    cur = q_pos // SEL_BLOCK
    out_parts = []
    for g in range(N_KV):
        q = _head_rows(qrow, g)
        feat = slice(g * HEAD_DIM, (g + 1) * HEAD_DIM)
        kcol = slice(g * HEAD_DIM, (g + 1) * HEAD_DIM)
        vcol = slice(KV_W + g * HEAD_DIM, KV_W + (g + 1) * HEAD_DIM)
        kts, vts, kps = [], [], []
        new_chosen = None
        for n in range(N_SEL):
            j = idx_ref[b, g * N_SEL + n]
            page = page_refs[g * N_SEL + n]
            kts.append(page[0, feat, :])
            vts.append(page[1, feat, :])
            in_block = (page_lane // SEL_BLOCK == j % BLOCKS_PER_PAGE) & (j < n_past_blocks)
            kps.append(jnp.where(in_block, (j // BLOCKS_PER_PAGE) * PAGE_SIZE + page_lane, q_pos + 1))
            hit = j == cur
            new_chosen = hit if new_chosen is None else (new_chosen | hit)
        s_s = _dot(q, jnp.concatenate(kts, axis=1))
        s_new = jnp.where(new_chosen, jnp.sum(q * sel_new[:, kcol], axis=1, keepdims=True), NEG_INF)
        s_s = jnp.where(jnp.concatenate(kps, axis=1) <= q_pos, s_s, NEG_INF)
        m = jnp.maximum(jnp.max(s_s, axis=1, keepdims=True), s_new)
        m = jnp.where(m > NEG_INF, m, 0.0)
        e_s = jnp.exp(s_s - m)
        e_new = jnp.exp(s_new - m)
        d_s = jnp.sum(e_s, axis=1, keepdims=True) + e_new
        d_s = jnp.where(d_s > 0, d_s, 1.0)
        o_s = (_dot_nt(e_s, jnp.concatenate(vts, axis=1)) + e_new * sel_new[:, vcol]) / d_s
        s_w = jnp.where(m_w, _dot(q, wincache_ref[0, feat, :]), NEG_INF)
        s_wn = jnp.sum(q * win_new[:, kcol], axis=1, keepdims=True)
        m = jnp.maximum(jnp.max(s_w, axis=1, keepdims=True), s_wn)
        e_w = jnp.exp(s_w - m)
        e_wn = jnp.exp(s_wn - m)
        d_w = jnp.sum(e_w, axis=1, keepdims=True) + e_wn
        o_w = (_dot_nt(e_w, wincache_ref[1, feat, :]) + e_wn * win_new[:, vcol]) / d_w
        for h in range(HPG):
            hd = g * HPG + h
            out_parts.append(gates[:, 3 * hd:3 * hd + 1] * oc[:, hd * HEAD_DIM:(hd + 1) * HEAD_DIM]
                             + gates[:, 3 * hd + 1:3 * hd + 2] * o_s[h:h + 1]
                             + gates[:, 3 * hd + 2:3 * hd + 3] * o_w[h:h + 1])
    o_ref[...] = jnp.concatenate(out_parts, axis=1)


def _nsa_sample_attend(idx, page_table, q_rot, gates, o_c, sel_new, win_new, win_cache_t, sel_cache_t,
                       *, q_pos, n_past_blocks):
    b = q_rot.shape[0]
    wb = win_cache_t.shape[3]

    def page_spec(k):
        def index(i, idx_ref, pt_ref):
            j = jnp.minimum(idx_ref[i, k], n_past_blocks - 1)
            return (pt_ref[i, j // BLOCKS_PER_PAGE], 0, 0, 0)
        return pl.BlockSpec((None, 2, KV_W, PAGE_SIZE), index)

    one = lambda w: pl.BlockSpec((None, 1, w), lambda i, idx_ref, pt_ref: (i, 0, 0))
    return pl.pallas_call(
        functools.partial(_nsa_sample_attend_body, q_pos=q_pos, n_past_blocks=n_past_blocks),
        grid_spec=pltpu.PrefetchScalarGridSpec(
            num_scalar_prefetch=2,
            grid=(b,),
            in_specs=[one(Q_W), one(GATE_PAD), one(Q_W), one(ROW_W), one(ROW_W),
                      pl.BlockSpec((None, 2, KV_W, wb), lambda i, idx_ref, pt_ref: (i, 0, 0, 0))]
            + [page_spec(k) for k in range(N_KV * N_SEL)],
            out_specs=one(Q_W),
        ),
        out_shape=jax.ShapeDtypeStruct((b, 1, Q_W), F32),
        compiler_params=_params("arbitrary"),
        name="nsa_sample_attend",
    )(idx, page_table, q_rot, gates, o_c, sel_new, win_new, win_cache_t, *([sel_cache_t] * (N_KV * N_SEL)))


def _rope_tables(pos):
    half = HEAD_DIM // 2
    freqs = ROPE_THETA ** (-jnp.arange(half, dtype=F32) / half)
    ang = pos.astype(F32)[:, None] * freqs[None, :]
    cos, sin = jnp.cos(ang), jnp.sin(ang)
    reps = LANES // HEAD_DIM
    return jnp.tile(jnp.concatenate([cos, cos], axis=1), (1, reps)), jnp.tile(jnp.concatenate([-sin, sin], axis=1), (1, reps))


def _cover(nc, nsel, rows, cols):
    start = np.arange(rows)[:, None] * CMP_STRIDE
    j = np.arange(cols)[None, :]
    hit = (start < (j + 1) * SEL_BLOCK) & (start + CMP_BLOCK > j * SEL_BLOCK) & (np.arange(rows)[:, None] < nc) & (j < nsel)
    return jnp.asarray(hit.astype(np.float32))


def _cmp_chunk_weight(w1_k, w1_v):
    per = jnp.stack([w.reshape(CMP_RATIO, CMP_STRIDE, HEAD_DIM, CMP_HID) for w in (w1_k, w1_v)])
    eye = jnp.eye(N_KV, dtype=F32)
    big = jnp.einsum("ab,krsdh->ksadbrh", eye, per)
    return big.reshape(2, CMP_STRIDE, KV_W, N_KV * CMP_RATIO * CMP_HID).astype(BF16)


def _rg_gate_weight(w_a, w_x):
    nb, bw, _ = w_a.shape
    per = MXU_DIM // bw
    n_grp = nb // per
    eye = jnp.eye(per, dtype=F32)

    def group(w):
        w = w.reshape(n_grp, per, bw, bw)
        return jnp.einsum("pq,gpde->gpdqe", eye, w).reshape(n_grp, per * bw, per * bw)

    return jnp.concatenate([group(w_a), group(w_x)], axis=2).astype(BF16)


def _feature_major(kv_rows):
    lead = kv_rows.shape[:-4]
    rows = kv_rows.shape[-4]
    nd = len(lead)
    perm = tuple(range(nd)) + (nd + 1, nd + 2, nd + 3, nd)
    return jnp.transpose(kv_rows, perm).reshape(lead + (2, KV_W, rows))


def _row_major_view(kv_t, b, rows):
    return jnp.transpose(kv_t.reshape(b, 2, N_KV, HEAD_DIM, rows), (0, 4, 1, 2, 3))


def kernel(x_prompt, x_sample, cache_cmp_kv, cache_sel_kv, cache_win_kv, state_conv, state_h, page_table, ffn1_w_gate, ffn1_w_up, ffn1_w_down, ln1_g, ln1_b, w_in, cmp_pe_k, cmp_w1_k, cmp_w2_k, cmp_pe_v, cmp_w1_v, cmp_w2_v, conv_w, conv_b, rg_w_a, rg_b_a, rg_w_x, rg_b_x, rg_lam, w_br_attn, w_br_rnn, w_out, ln2_g, ln2_b, ffn2_w_gate, ffn2_w_up, ffn2_w_down, ln3_g, ln3_b):
    bp, tp, d = x_prompt.shape
    bs, ts, _ = x_sample.shape
    depth = w_in.shape[0]
    d_rnn = conv_w.shape[2]
    n_pages = page_table.shape[1]
    past_len = n_pages * PAGE_SIZE
    assert ts == 1 and tp % CMP_STRIDE == 0 and tp % SEL_BLOCK == 0
    alpha = (2.0 * depth) ** 0.25
    wb = cache_win_kv.shape[2]
    wbp = min(WINDOW, tp)

    nch_p = tp // CMP_STRIDE
    nc_p = nch_p - CMP_RATIO + 1
    nsel_p = tp // SEL_BLOCK
    nch_s = past_len // CMP_STRIDE
    nc_s = nch_s - CMP_RATIO + 1
    nsel_s = -(-(past_len + ts) // SEL_BLOCK)
    n_past_blocks = past_len // SEL_BLOCK
    assert nsel_s >= N_SEL and n_pages % PAGES_PER_STEP == 0

    tm = tt = min(ROW_TILE, tp)
    tf = min(FFN_TILE, tp)
    tq = min(ATTN_TILE, tp)

    cos_p, sin_p = _rope_tables(jnp.arange(tp))
    cos_s, sin_s = _rope_tables(jnp.full((bs,), past_len))
    cover_p = _cover(nc_p, nsel_p, nch_p, nsel_p).T
    expand_p = jnp.asarray((np.arange(tp)[:, None] // SEL_BLOCK == np.arange(nsel_p)[None, :]).astype(np.float32)).astype(BF16)

    xp = x_prompt.reshape(bp * tp, d)
    xs = x_sample.reshape(bs * ts, d)
    outs = [[] for _ in range(10)]
    for l in range(depth):
        row = lambda v: v[l].reshape(1, -1)
        ffn1 = (ffn1_w_gate[l].astype(BF16), ffn1_w_up[l].astype(BF16), ffn1_w_down[l].astype(BF16), row(ln1_g), row(ln1_b))
        ffn2 = (ffn2_w_gate[l].astype(BF16), ffn2_w_up[l].astype(BF16), ffn2_w_down[l].astype(BF16), row(ln3_g), row(ln3_b))
        n_a = Q_W + 6 * KV_W + 3 * N_HEADS
        w_a = jnp.pad(w_in[l][:, :n_a], ((0, 0), (0, GATE_PAD - 3 * N_HEADS))).astype(BF16)
        w_b = w_in[l][:, n_a:]
        w_b = jnp.concatenate([w_b[:, :d_rnn], 0.5 * w_b[:, d_rnn:]], axis=1).astype(BF16)
        w_cmp = _cmp_chunk_weight(cmp_w1_k[l], cmp_w1_v[l])
        cmpw = (cmp_pe_k[l].reshape(1, -1), cmp_w1_k[l], cmp_w2_k[l], cmp_pe_v[l].reshape(1, -1), cmp_w1_v[l], cmp_w2_v[l])
        mixw = (jnp.pad(conv_w[l], ((0, SUBLANES - CONV_W), (0, 0))), row(conv_b),
                _rg_gate_weight(0.5 * rg_w_a[l], 0.5 * rg_w_x[l]), 0.5 * row(rg_b_a), 0.5 * row(rg_b_x), row(rg_lam),
                (0.5 * w_br_attn[l]).astype(BF16), (0.5 * w_br_rnn[l]).astype(BF16),
                w_out[l].astype(BF16), row(ln2_g), row(ln2_b))

        h1 = _ffn_half_step(xp, *ffn1, alpha=alpha, tm=tf)
        (cmp, sel_k, win_k, xr, gr, ga, gb, q_raw_t, q_rot_t, gates_t, cmp_t, sel_t, win_t) = _in_proj(
            h1, cos_p, sin_p, w_a, w_b, tm=tm, seq_len=tp, feature_major=True)
        b3 = lambda a: a.reshape(bp, tp, a.shape[-1])
        p1 = _cmp_stage1(b3(cmp), w_cmp)
        ckv, ckv_t = _cmp_finish(p1, *cmpw, nc=nc_p)
        o_attn = _nsa_prompt(q_raw_t, q_rot_t, gates_t, ckv, ckv_t, b3(sel_k), sel_t, b3(win_k), win_t,
                             cover_p, expand_p, tq=tq, nc=nc_p)
        h2, h_last = _mix_prompt(b3(xr), b3(gr), b3(ga), b3(gb), o_attn, b3(h1), mixw, tt=tt, alpha=alpha)
        xp = _ffn_half_step(h2.reshape(bp * tp, d), *ffn2, alpha=alpha, tm=tf)
        outs[0].append(_row_major_view(cmp_t, bp, tp))
        outs[2].append(_row_major_view(sel_t, bp, tp))
        outs[4].append(_row_major_view(win_t[:, :, tp - wbp:], bp, wbp))
        outs[6].append(b3(xr)[:, tp - (CONV_W - 1):])
        outs[8].append(h_last.reshape(bp, d_rnn))

        h1s = _ffn_half_step(xs, *ffn1, alpha=alpha, tm=bs)
        q_raw, q_rot, cmp, sel, win, gates, xr, gr, ga, gb = _in_proj(
            h1s, cos_s, sin_s, w_a, w_b, tm=bs, seq_len=bs, feature_major=False)
        b1 = lambda a: a.reshape(bs, 1, a.shape[-1])
        q_pos = past_len
        o_c, score = _cmp_sample(_feature_major(cache_cmp_kv[l]), page_table, w_cmp, cmpw, b1(q_raw),
                                 nc=nc_s, nsel=nsel_s, q_pos=q_pos)
        lanes = _top_blocks(score.reshape(bs * SUBLANES, score.shape[-1]))
        idx = lanes.reshape(bs, SUBLANES, LANES)[:, :N_KV, :N_SEL].reshape(bs, N_KV * N_SEL) // TOKENS_PER_BLOCK
        o_attn = _nsa_sample_attend(idx, page_table, b1(q_rot), b1(gates), o_c, b1(sel), b1(win),
                                    _feature_major(cache_win_kv[l]), _feature_major(cache_sel_kv[l]),
                                    q_pos=q_pos, n_past_blocks=n_past_blocks)
        conv_rows = [state_conv[l][:, k] for k in range(CONV_W - 1)]
        h2s, h_new = _mix_sample(xr, conv_rows, state_h[l], gr, ga, gb, o_attn.reshape(bs, Q_W), h1s, mixw, alpha=alpha)
        xs = _ffn_half_step(h2s, *ffn2, alpha=alpha, tm=bs)
        outs[1].append(cmp.reshape(bs, ts, 2, N_KV, HEAD_DIM))
        outs[3].append(sel.reshape(bs, ts, 2, N_KV, HEAD_DIM))
        win_all = jnp.concatenate([cache_win_kv[l], win.reshape(bs, ts, 2, N_KV, HEAD_DIM)], axis=1)
        outs[5].append(win_all[:, ts:])
        outs[7].append(jnp.concatenate([state_conv[l], xr[:, None, :]], axis=1)[:, ts:])
        outs[9].append(h_new)

    stacked = [jnp.stack(o) for o in outs]
    cmp_p, cmp_s, sel_p, sel_s, win_p, win_s, conv_p, conv_s, h_p, h_s = stacked
    return (xp.reshape(bp, tp, d), xs.reshape(bs, ts, d), cmp_p, cmp_s, sel_p, sel_s, win_p, win_s,
            conv_p, conv_s, h_p, h_s)
```

```python
import functools
import math

import numpy as np
import jax
import jax.numpy as jnp
from jax import lax
from jax.experimental import pallas as pl
from jax.experimental.pallas import tpu as pltpu

F32 = jnp.float32
BF16 = jnp.bfloat16

N_HEADS = 8
HEAD_DIM = 64
N_KV = 2
HPG = N_HEADS // N_KV
Q_W = N_HEADS * HEAD_DIM
KV_W = N_KV * HEAD_DIM
CMP_BLOCK = 32
CMP_STRIDE = 16
CMP_RATIO = CMP_BLOCK // CMP_STRIDE
CMP_HID = 64
SEL_BLOCK = 64
N_SEL = 16
SEL_BONUS = 1.0e4
WINDOW = 512
ROPE_THETA = 10000.0
RG_C = 8.0
CONV_W = 4
PAGE_SIZE = 128
LN_EPS = 1e-5
SCALE = HEAD_DIM ** -0.5
QK_SCALE_LOG2 = SCALE * math.log2(math.e)
NEG_INF = float("-inf")

LANES = 128
SUBLANES = 8
BF16_SUBLANES = 16
MXU_DIM = 256
VMEM_LIMIT_BYTES = 56 * 1024 * 1024
ROW_TILE = MXU_DIM
FFN_TILE = 2 * MXU_DIM
ATTN_TILE = 2 * MXU_DIM

ROW_W = 2 * KV_W
P_W = 2 * N_KV * CMP_RATIO * CMP_HID
GATE_PAD = LANES
PAGES_PER_STEP = 64
BLOCKS_PER_PAGE = PAGE_SIZE // SEL_BLOCK
TOKENS_PER_BLOCK = SEL_BLOCK // CMP_STRIDE


def _dot(a, b):
    return jnp.dot(a.astype(BF16), b.astype(BF16), preferred_element_type=F32)


def _dot_nt(a, b):
    return lax.dot_general(a.astype(BF16), b.astype(BF16), (((1,), (1,)), ((), ())),
                           preferred_element_type=F32)


def _dot_f32(a, b):
    return jnp.dot(a, b, preferred_element_type=F32, precision=lax.Precision.HIGHEST)


def _sigmoid(x):
    return 0.5 * jnp.tanh(0.5 * x) + 0.5


def _gelu(x):
    return 0.5 * x * (1.0 + jnp.tanh(math.sqrt(2.0 / math.pi) * (x + 0.044715 * (x * x * x))))


def _layer_norm(y, g, b):
    mu = jnp.mean(y, axis=-1, keepdims=True)
    yc = y - mu
    var = jnp.mean(yc * yc, axis=-1, keepdims=True)
    return yc * lax.rsqrt(var + LN_EPS) * g + b


def _softmax_parts(s, mask):
    s = jnp.where(mask, s, NEG_INF)
    m = jnp.max(s, axis=-1, keepdims=True)
    m = jnp.where(m > NEG_INF, m, 0.0)
    e = jnp.exp(s - m)
    d = jnp.sum(e, axis=-1, keepdims=True)
    return e, jnp.where(d > 0, d, 1.0)


def _const_spec(shape):
    nd = len(shape)
    return pl.BlockSpec(shape, lambda *_: (0,) * nd)


def _resident_spec(shape):
    nd = len(shape)
    return pl.BlockSpec(shape, lambda *_: (0,) * nd, pipeline_mode=pl.Buffered(1))


def _params(*sem):
    return pltpu.CompilerParams(dimension_semantics=sem, vmem_limit_bytes=VMEM_LIMIT_BYTES)


def _ffn_body(x_ref, wg_ref, wu_ref, wd_ref, g_ref, b_ref, o_ref, *, alpha):
    x = x_ref[...]
    xb = x.astype(BF16)
    gate = jnp.dot(xb, wg_ref[...], preferred_element_type=F32)
    up = jnp.dot(xb, wu_ref[...], preferred_element_type=F32)
    hmid = gate * _sigmoid(gate) * up
    f = _dot(hmid, wd_ref[...])
    o_ref[...] = _layer_norm(alpha * x + 0.5 * f, g_ref[...], b_ref[...])


def _ffn_half_step(x, wg, wu, wd, g, b, *, alpha, tm):
    n, d = x.shape
    ff = wg.shape[1]
    return pl.pallas_call(
        functools.partial(_ffn_body, alpha=alpha),
        grid=(n // tm,),
        in_specs=[pl.BlockSpec((tm, d), lambda i: (i, 0)),
                  _resident_spec((d, ff)), _resident_spec((d, ff)), _resident_spec((ff, d)),
                  _const_spec((1, d)), _const_spec((1, d))],
        out_specs=pl.BlockSpec((tm, d), lambda i: (i, 0)),
        out_shape=jax.ShapeDtypeStruct((n, d), F32),
        compiler_params=_params("parallel"),
        name="ffn_half_step",
    )(x, wg, wu, wd, g, b)


def _rope(x, cos, sin):
    w = x.shape[1]
    reps = w // LANES
    c = jnp.tile(cos, (1, reps)) if reps > 1 else cos
    s = jnp.tile(sin, (1, reps)) if reps > 1 else sin
    lane = lax.broadcasted_iota(jnp.int32, x.shape, 1)
    first_half = (lane % HEAD_DIM) < (HEAD_DIM // 2)
    swapped = jnp.where(first_half, pltpu.roll(x, w - HEAD_DIM // 2, 1), pltpu.roll(x, HEAD_DIM // 2, 1))
    return x * c + swapped * s


def _inproj_body(h_ref, cos_ref, sin_ref, wa_ref, wb_ref, *out_refs, feature_major):
    hb = h_ref[...].astype(BF16)
    cos = cos_ref[...]
    sin = sin_ref[...]
    za = jnp.dot(hb, wa_ref[...], preferred_element_type=F32)
    q = za[:, :Q_W]
    kv = za[:, Q_W:Q_W + 6 * KV_W]
    gates = _sigmoid(za[:, Q_W + 6 * KV_W:])
    q_rot = _rope(q, cos, sin)
    cmp = kv[:, :2 * KV_W]
    sel_k = _rope(kv[:, 2 * KV_W:3 * KV_W], cos, sin)
    win_k = _rope(kv[:, 4 * KV_W:5 * KV_W], cos, sin)
    sel = jnp.concatenate([sel_k, kv[:, 3 * KV_W:4 * KV_W]], axis=1)
    win = jnp.concatenate([win_k, kv[:, 5 * KV_W:6 * KV_W]], axis=1)
    if feature_major:
        (cmp_ref, selk_ref, wink_ref, xr_ref, gr_ref, ga_ref, gb_ref,
         qrawt_ref, qrott_ref, gatest_ref, cmpt_ref, selt_ref, wint_ref) = out_refs
        cmp_ref[...] = cmp
        selk_ref[...] = sel_k.astype(BF16)
        wink_ref[...] = win_k.astype(BF16)
        qrawt_ref[...] = q.T
        qrott_ref[...] = q_rot.T
        gatest_ref[...] = gates.T
        cmpt_ref[...] = cmp.T
        selt_ref[...] = sel.T
        wint_ref[...] = win.T
    else:
        qraw_ref, qrot_ref, cmp_ref, sel_ref, win_ref, gates_ref, xr_ref, gr_ref, ga_ref, gb_ref = out_refs
        qraw_ref[...] = q
        qrot_ref[...] = q_rot
        cmp_ref[...] = cmp
        sel_ref[...] = sel
        win_ref[...] = win
        gates_ref[...] = gates
    zb = jnp.dot(hb, wb_ref[...], preferred_element_type=F32)
    d = xr_ref.shape[1]
    xr_ref[...] = zb[:, :d]
    gr_ref[...] = zb[:, d:2 * d]
    ga_ref[...] = zb[:, 2 * d:3 * d]
    gb_ref[...] = zb[:, 3 * d:]


def _in_proj(h, cos, sin, wa, wb, *, tm, seq_len, feature_major):
    n, d = h.shape
    wa_w, wb_w = wa.shape[1], wb.shape[1]
    tiles = seq_len // tm
    row = lambda i: (i, 0)
    tab = lambda i: (i % tiles, 0)
    if feature_major:
        rows = [(ROW_W, F32), (KV_W, BF16), (KV_W, BF16), (d, F32), (d, F32), (d, F32), (d, F32)]
        cols = [Q_W, Q_W, GATE_PAD, ROW_W, ROW_W, ROW_W]
    else:
        rows = [(w, F32) for w in (Q_W, Q_W, ROW_W, ROW_W, ROW_W, GATE_PAD, d, d, d, d)]
        cols = []
    out_specs = [pl.BlockSpec((tm, w), row) for w, _ in rows]
    out_shape = [jax.ShapeDtypeStruct((n, w), dt) for w, dt in rows]
    out_specs += [pl.BlockSpec((None, w, tm), lambda i: (i // tiles, 0, i % tiles)) for w in cols]
    out_shape += [jax.ShapeDtypeStruct((n // seq_len, w, seq_len), F32) for w in cols]
    return pl.pallas_call(
        functools.partial(_inproj_body, feature_major=feature_major),
        grid=(n // tm,),
        in_specs=[pl.BlockSpec((tm, d), row), pl.BlockSpec((tm, LANES), tab), pl.BlockSpec((tm, LANES), tab),
                  _const_spec((d, wa_w)), _const_spec((d, wb_w))],
        out_specs=out_specs,
        out_shape=out_shape,
        compiler_params=_params("parallel"),
        name="in_proj",
    )(h, cos, sin, wa, wb)


def _cmp_chunks(rows_of, w_ref, kv, nch):
    acc = None
    for s in range(CMP_STRIDE):
        part = _dot(rows_of(s, nch), w_ref[kv, s])
        acc = part if acc is None else acc + part
    return acc


def _cmp_stage1_body(x_ref, w_ref, o_ref, rows_ref):
    nch = x_ref.shape[0] // CMP_STRIDE
    half = P_W // 2
    for kv in range(2):
        rows_ref[kv] = x_ref[:, kv * KV_W:(kv + 1) * KV_W]
        rows_of = lambda s, n, kv=kv: rows_ref[kv, pl.ds(s, n, stride=CMP_STRIDE), :]
        o_ref[:, kv * half:(kv + 1) * half] = _cmp_chunks(rows_of, w_ref, kv, nch)


def _cmp_stage1(x, w):
    b, t, _ = x.shape
    nch = t // CMP_STRIDE
    return pl.pallas_call(
        _cmp_stage1_body,
        grid=(b,),
        in_specs=[pl.BlockSpec((None, t, ROW_W), lambda i: (i, 0, 0)), _const_spec(w.shape)],
        out_specs=pl.BlockSpec((None, nch, P_W), lambda i: (i, 0, 0)),
        out_shape=jax.ShapeDtypeStruct((b, nch, P_W), F32),
        scratch_shapes=[pltpu.VMEM((2, t, KV_W), F32)],
        compiler_params=_params("parallel"),
        name="cmp_stage1",
    )(x, w)


def _cmp_sample_body(pt_ref, *refs, nc, nsel, q_pos):
    del pt_ref
    x_refs = refs[:PAGES_PER_STEP]
    (perm_ref, w_ref, pek_ref, w1k_ref, w2k_ref, pev_ref, w1v_ref, w2v_ref, qraw_ref,
     oc_ref, score_ref, planes_ref, stage_ref, p_ref) = refs[PAGES_PER_STEP:]
    step = pl.program_id(1)
    cpp = PAGE_SIZE // CMP_STRIDE
    nch_step = PAGES_PER_STEP * cpp
    perm = perm_ref[...]
    for k in range(0, PAGES_PER_STEP, 2):
        for kv in range(2):
            pair = jnp.concatenate([x_refs[k][kv], x_refs[k + 1][kv]], axis=1).astype(BF16)
            stage_ref[k // 2, kv] = jnp.dot(pair, perm, preferred_element_type=F32)
            for half in range(2):
                rows = stage_ref[k // 2, kv, :, half * PAGE_SIZE:(half + 1) * PAGE_SIZE].T
                first = (k + half) * cpp
                for s in range(CMP_STRIDE):
                    planes_ref[kv, s, first:first + cpp, :] = rows[s * cpp:(s + 1) * cpp]
    half = P_W // 2
    row0 = pl.multiple_of(step * nch_step, nch_step)
    for kv in range(2):
        acc = None
        for j in range(CMP_STRIDE // 2):
            pair = jnp.concatenate([planes_ref[kv, 2 * j], planes_ref[kv, 2 * j + 1]], axis=1)
            part = _dot(pair, w_ref[kv, j])
            acc = part if acc is None else acc + part
        p_ref[pl.ds(row0, nch_step), kv * half:(kv + 1) * half] = acc

    @pl.when(step == pl.num_programs(1) - 1)
    def _():
        ckv = _cmp_second_layer(p_ref[...], pek_ref, w1k_ref, w2k_ref, pev_ref, w1v_ref, w2v_ref, nc)
        oc_ref[...], score_ref[...] = _sample_scores(qraw_ref[...], ckv, score_ref.shape[1], nc, nsel, q_pos)


def _cmp_sample(cache_t, page_table, w, cmpw, q_raw, *, nc, nsel, q_pos):
    b, n_pages = page_table.shape
    steps = n_pages // PAGES_PER_STEP
    cpp = PAGE_SIZE // CMP_STRIDE
    nch = n_pages * cpp
    width = max(nch + LANES, -(-(TOKENS_PER_BLOCK * nsel) // LANES) * LANES)
    w = w.reshape(2, CMP_STRIDE // 2, 2 * KV_W, w.shape[-1])
    row = np.arange(PAGE_SIZE)
    perm = np.zeros((PAGE_SIZE, PAGE_SIZE), np.float32)
    perm[row, (row % CMP_STRIDE) * cpp + row // CMP_STRIDE] = 1.0
    perm = jnp.asarray(np.kron(np.eye(2, dtype=np.float32), perm)).astype(BF16)

    def page_spec(k):
        return pl.BlockSpec((None, 2, KV_W, PAGE_SIZE), lambda i, s, pt: (pt[i, s * PAGES_PER_STEP + k], 0, 0, 0))

    const = lambda a: pl.BlockSpec(a.shape, lambda i, s, pt: (0,) * a.ndim)
    per_request = lambda r, c: pl.BlockSpec((None, r, c), lambda i, s, pt: (i, 0, 0))
    return pl.pallas_call(
        functools.partial(_cmp_sample_body, nc=nc, nsel=nsel, q_pos=q_pos),
        grid_spec=pltpu.PrefetchScalarGridSpec(
            num_scalar_prefetch=1,
            grid=(b, steps),
            in_specs=[page_spec(k) for k in range(PAGES_PER_STEP)]
            + [const(perm), const(w)] + [const(a) for a in cmpw] + [per_request(1, Q_W)],
            out_specs=[per_request(1, Q_W), per_request(SUBLANES, width)],
            scratch_shapes=[pltpu.VMEM((2, CMP_STRIDE, PAGES_PER_STEP * cpp, KV_W), F32),
                            pltpu.VMEM((PAGES_PER_STEP // 2, 2, KV_W, 2 * PAGE_SIZE), F32),
                            pltpu.VMEM((nch, P_W), F32)],
        ),
        out_shape=[jax.ShapeDtypeStruct((b, 1, Q_W), F32), jax.ShapeDtypeStruct((b, SUBLANES, width), F32)],
        compiler_params=_params("parallel", "arbitrary"),
        name="cmp_sample",
    )(page_table, *([cache_t] * PAGES_PER_STEP), perm, w, *cmpw, q_raw)


def _cmp_second_layer(p, pek_ref, w1k_ref, w2k_ref, pev_ref, w1v_ref, w2v_ref, nc):
    nch = p.shape[0]
    row = lax.broadcasted_iota(jnp.int32, (nch, CMP_HID), 0)
    outs = []
    for kv, (pe_ref, w1_ref, w2_ref) in enumerate(((pek_ref, w1k_ref, w2k_ref), (pev_ref, w1v_ref, w2v_ref))):
        pe = jnp.broadcast_to(pe_ref[...], (SUBLANES, pe_ref.shape[1]))
        bias = _dot(pe, w1_ref[...])[0:1]
        for g in range(N_KV):
            c0 = (kv * N_KV + g) * CMP_RATIO * CMP_HID
            u = bias + p[:, c0:c0 + CMP_HID]
            for r in range(1, CMP_RATIO):
                u = u + pltpu.roll(p[:, c0 + r * CMP_HID:c0 + (r + 1) * CMP_HID], nch - r, 0)
            out = _dot(_gelu(u), w2_ref[...])
            outs.append(jnp.where(row < nc, out, 0.0))
    return jnp.concatenate(outs, axis=1)


def _cmp_finish_body(p_ref, pek_ref, w1k_ref, w2k_ref, pev_ref, w1v_ref, w2v_ref, o_ref, ot_ref, *, nc):
    ckv = _cmp_second_layer(p_ref[...], pek_ref, w1k_ref, w2k_ref, pev_ref, w1v_ref, w2v_ref, nc)
    o_ref[...] = ckv
    ot_ref[...] = ckv.T


def _cmp_finish(p, pe_k, w1_k, w2_k, pe_v, w1_v, w2_v, *, nc):
    b, nch, _ = p.shape
    flat = CMP_BLOCK * HEAD_DIM
    return pl.pallas_call(
        functools.partial(_cmp_finish_body, nc=nc),
        grid=(b,),
        in_specs=[pl.BlockSpec((None, nch, P_W), lambda i: (i, 0, 0)),
                  _const_spec((1, flat)), _const_spec((flat, CMP_HID)), _const_spec((CMP_HID, HEAD_DIM)),
                  _const_spec((1, flat)), _const_spec((flat, CMP_HID)), _const_spec((CMP_HID, HEAD_DIM))],
        out_specs=[pl.BlockSpec((None, nch, ROW_W), lambda i: (i, 0, 0)),
                   pl.BlockSpec((None, ROW_W, nch), lambda i: (i, 0, 0))],
        out_shape=[jax.ShapeDtypeStruct((b, nch, ROW_W), F32), jax.ShapeDtypeStruct((b, ROW_W, nch), F32)],
        compiler_params=_params("parallel"),
        name="cmp_finish",
    )(p, pe_k, w1_k, w2_k, pe_v, w1_v, w2_v)


def _block_scores(imp, q_pos, nsel_valid, block_axis=1):
    jblk = lax.broadcasted_iota(jnp.int32, imp.shape, block_axis)
    cur = q_pos // SEL_BLOCK
    valid = (jblk * SEL_BLOCK <= q_pos) & (jblk < nsel_valid)
    forced = (jblk == 0) | (jblk == cur) | (jblk == cur - 1)
    return jnp.where(valid, imp + jnp.where(forced, SEL_BONUS, 0.0), NEG_INF)


def _nsa_prompt_body(qrawt_ref, qrott_ref, gatest_ref, ckv_ref, ckvt_ref, selk_ref, selt_ref, wink_ref, wint_ref,
                     covert_ref, expandt_ref, o_ref, *, tq, nc):
    nchp = ckv_ref.shape[0]
    nsel = covert_ref.shape[0]
    tile = pl.program_id(1)
    s0 = tile * tq
    lane_t = lax.broadcasted_iota(jnp.int32, (1, tq), 1)
    key_sub = lax.broadcasted_iota(jnp.int32, (tq, 1), 0)
    qpos = s0 + lane_t
    causal_diag = key_sub <= lane_t
    window_edge = lane_t <= key_sub
    cidx = lax.broadcasted_iota(jnp.int32, (nchp, 1), 0)
    jblk = lax.broadcasted_iota(jnp.int32, (nsel, 1), 0)
    n_win_chunks = WINDOW // tq
    vrows = [slice(KV_W + g * HEAD_DIM, KV_W + (g + 1) * HEAD_DIM) for g in range(N_KV)]
    ones_rows = jnp.ones((BF16_SUBLANES, tq), BF16)
    group_of = lambda hd: hd // HPG

    def group_rows(qt, g):
        z = jnp.zeros_like(qt)
        return jnp.concatenate([qt, z] if g == 0 else [z, qt], axis=0)

    def head_rows(ref, hd):
        return group_rows((ref[hd * HEAD_DIM:(hd + 1) * HEAD_DIM, :] * QK_SCALE_LOG2).astype(BF16), group_of(hd))

    q_rot = [head_rows(qrott_ref, hd) for hd in range(N_HEADS)]
    q_raw = jnp.concatenate([head_rows(qrawt_ref, hd) for hd in range(N_HEADS)], axis=1)
    s_c = jnp.dot(ckv_ref[:, :KV_W].astype(BF16), q_raw, preferred_element_type=F32)
    m_c = (cidx * CMP_STRIDE + (CMP_BLOCK - 1) <= jnp.concatenate([qpos] * N_HEADS, axis=1)) & (cidx < nc)
    s_c = jnp.where(m_c, s_c, NEG_INF)
    mx = jnp.max(s_c, axis=0, keepdims=True)
    e_c = jnp.exp2(s_c - jnp.where(mx > NEG_INF, mx, 0.0))
    d_c = jnp.sum(e_c, axis=0, keepdims=True)
    p_c = e_c * (1.0 / jnp.where(d_c > 0, d_c, 1.0))
    head_cols = lambda hd: slice(hd * tq, (hd + 1) * tq)
    o_c = [_dot(ckvt_ref[vrows[group_of(hd)], :], p_c[:, head_cols(hd)]) for hd in range(N_HEADS)]
    p_sum = []
    for g in range(N_KV):
        acc = p_c[:, head_cols(g * HPG)]
        for hd in range(g * HPG + 1, (g + 1) * HPG):
            acc = acc + p_c[:, head_cols(hd)]
        p_sum.append(acc)
    imp = _dot_f32(covert_ref[...], jnp.concatenate(p_sum, axis=1))
    score = _block_scores(imp, jnp.concatenate([qpos] * N_KV, axis=1), nsel, block_axis=0)
    rank = jnp.zeros(score.shape, jnp.int32)
    for i in range(nsel):
        s_i = score[i:i + 1, :]
        beats = (s_i > score) | ((s_i == score) & (i < jblk))
        rank = rank + beats.astype(jnp.int32)
    chosen = (rank < N_SEL).astype(BF16)

    def picked(k0):
        hit = jnp.dot(expandt_ref[pl.ds(k0, tq), :], chosen, preferred_element_type=F32)
        return [hit[:, g * tq:(g + 1) * tq] > 0.5 for g in range(N_KV)]

    def attend(jobs):
        loaded = []
        for k_ref, vt_ref, kc, masks_of, _ in jobs:
            k0 = pl.multiple_of(kc * tq, tq)
            keys = k_ref[pl.ds(k0, tq), :]
            values_t = [jnp.concatenate([vt_ref[rows, pl.ds(k0, tq)].astype(BF16), ones_rows], axis=0)
                        for rows in vrows]
            loaded.append((keys, values_t, masks_of(k0)))
        scores = [[jnp.dot(keys, q_rot[hd], preferred_element_type=F32) for hd in range(N_HEADS)]
                  for keys, _, _ in loaded]
        updates = []
        for job, (_, _, masks), sc in zip(jobs, loaded, scores):
            carry = job[4]
            stats, probs = [], []
            for hd in range(N_HEADS):
                m = carry[hd][0]
                s = sc[hd] if masks is None else jnp.where(masks[group_of(hd)], sc[hd], NEG_INF)
                m_new = jnp.maximum(m, jnp.max(s, axis=0, keepdims=True))
                m_safe = m_new if masks is None else jnp.where(m_new > NEG_INF, m_new, 0.0)
                alpha = jnp.exp2(m - m_safe)
                stats.append((m_new, alpha))
                probs.append(jnp.exp2(s - m_safe).astype(BF16))
            updates.append((stats, probs))
        out = []
        for job, (_, values_t, _), (stats, probs) in zip(jobs, loaded, updates):
            carry = job[4]
            pv = [jnp.dot(values_t[group_of(hd)], probs[hd], preferred_element_type=F32) for hd in range(N_HEADS)]
            out.append(tuple((stats[hd][0],
                              carry[hd][1] * stats[hd][1] + pv[hd][HEAD_DIM:HEAD_DIM + 1],
                              carry[hd][2] * stats[hd][1] + pv[hd][:HEAD_DIM]) for hd in range(N_HEADS)))
        return out

    def finish(carry):
        return [acc * (1.0 / jnp.where(l > 0, l, 1.0)) for _, l, acc in carry]

    init = tuple((jnp.full((1, tq), NEG_INF, F32), jnp.zeros((1, tq), F32), jnp.zeros((HEAD_DIM, tq), F32))
                 for _ in range(N_HEADS))
    no_mask = lambda k0: None
    sel = lax.fori_loop(0, tile, lambda kc, c: attend([(selk_ref, selt_ref, kc, picked, c)])[0], init)
    edge = tile - n_win_chunks
    win = lax.fori_loop(jnp.maximum(edge, 0), jnp.maximum(edge + 1, 0),
                        lambda kc, c: attend([(wink_ref, wint_ref, kc, lambda k0: [window_edge] * N_KV, c)])[0], init)
    win = lax.fori_loop(jnp.maximum(edge + 1, 0), tile,
                        lambda kc, c: attend([(wink_ref, wint_ref, kc, no_mask, c)])[0], win)
    sel, win = attend([(selk_ref, selt_ref, tile, lambda k0: [hit & causal_diag for hit in picked(k0)], sel),
                       (wink_ref, wint_ref, tile, lambda k0: [causal_diag] * N_KV, win)])
    o_s, o_w = finish(sel), finish(win)
    out_heads = []
    for hd in range(N_HEADS):
        gate = lambda branch: gatest_ref[3 * hd + branch:3 * hd + branch + 1, :]
        out_heads.append(gate(0) * o_c[hd] + gate(1) * o_s[hd] + gate(2) * o_w[hd])
    o_ref[...] = jnp.concatenate(out_heads, axis=0).T


def _nsa_prompt(q_raw_t, q_rot_t, gates_t, ckv, ckv_t, sel_k, sel_t, win_k, win_t, cover_t, expand_t, *, tq, nc):
    b, _, t = q_raw_t.shape
    nchp = ckv.shape[1]
    nsel = cover_t.shape[0]
    assert WINDOW % tq == 0
    tile_t = lambda w: pl.BlockSpec((None, w, tq), lambda i, j: (i, 0, j))
    whole = lambda r, w: pl.BlockSpec((None, r, w), lambda i, j: (i, 0, 0))
    return pl.pallas_call(
        functools.partial(_nsa_prompt_body, tq=tq, nc=nc),
        grid=(b, t // tq),
        in_specs=[tile_t(Q_W), tile_t(Q_W), tile_t(GATE_PAD), whole(nchp, ROW_W), whole(ROW_W, nchp),
                  whole(t, KV_W), whole(ROW_W, t), whole(t, KV_W), whole(ROW_W, t),
                  _const_spec((nsel, nchp)), _const_spec((t, nsel))],
        out_specs=pl.BlockSpec((None, tq, Q_W), lambda i, j: (i, j, 0)),
        out_shape=jax.ShapeDtypeStruct((b, t, Q_W), F32),
        compiler_params=_params("parallel", "arbitrary"),
        name="nsa_prompt",
    )(q_raw_t, q_rot_t, gates_t, ckv, ckv_t, sel_k, sel_t, win_k, win_t, cover_t, expand_t)


def _softplus(x):
    return jnp.maximum(x, 0.0) + jnp.log1p(jnp.exp(-jnp.abs(x)))


def _rg_gates(xc, wgate_half_ref, ba_half, bx_half, lam):
    xcb = xc.astype(BF16)
    n_grp = wgate_half_ref.shape[0]
    gw = wgate_half_ref.shape[1]
    za, zx = [], []
    for k in range(n_grp):
        z = jnp.dot(xcb[:, k * gw:(k + 1) * gw], wgate_half_ref[k], preferred_element_type=F32)
        za.append(z[:, :gw])
        zx.append(z[:, gw:])
    t_r = jnp.tanh(jnp.concatenate(za, axis=1) + ba_half)
    t_i = jnp.tanh(jnp.concatenate(zx, axis=1) + bx_half)
    log_a = (1.0 + t_r) * (-0.5 * RG_C * _softplus(-lam))
    a = jnp.exp(log_a)
    u = jnp.sqrt(-0.25 * jnp.tanh(log_a) * (1.0 + a * a)) * (1.0 + t_i) * xc
    return a, u


def _gelu_of_half(xh):
    c = math.sqrt(2.0 / math.pi)
    return xh * (1.0 + jnp.tanh(xh * (2.0 * c + (8.0 * c * 0.044715) * (xh * xh))))


def _merge(h_res, hs, gr_half, ga_half, gb_half, o_attn, wbra_half_ref, wbrr_half_ref, wout_ref, g2, b2, alpha):
    y_rnn = hs * _gelu_of_half(gr_half)
    m = ((1.0 + jnp.tanh(ga_half)) * _dot(o_attn, wbra_half_ref[...])
         + (1.0 + jnp.tanh(gb_half)) * _dot(y_rnn, wbrr_half_ref[...]))
    return _layer_norm(alpha * h_res + _dot(m, wout_ref[...]), g2, b2)


def _mix_prompt_body(xr_ref, gr_ref, ga_ref, gb_ref, oat_ref, h1_ref, convw_ref, convb_ref, wgate_ref,
                     ba_ref, bx_ref, lam_ref, wbra_ref, wbrr_ref, wout_ref, g2_ref, b2_ref,
                     o_ref, hlast_ref, hc_ref, tail_ref, *, tt, alpha):
    @pl.when(pl.program_id(1) == 0)
    def _():
        hc_ref[...] = jnp.zeros_like(hc_ref)
        tail_ref[...] = jnp.zeros_like(tail_ref)

    x = xr_ref[...]
    convw = convw_ref[...]
    sub = lax.broadcasted_iota(jnp.int32, (SUBLANES, x.shape[1]), 0)
    shifts = range(1, CONV_W)
    prev_rot = [pltpu.roll(tail_ref[...], d, 0) for d in shifts]
    xc_groups = []
    for k in range(tt // SUBLANES):
        x_grp = x[k * SUBLANES:(k + 1) * SUBLANES]
        rot = [pltpu.roll(x_grp, d, 0) for d in shifts]
        xc_grp = convb_ref[...] + convw[CONV_W - 1:CONV_W] * x_grp
        for d, r_prev, r_cur in zip(shifts, prev_rot, rot):
            xc_grp = xc_grp + convw[CONV_W - 1 - d:CONV_W - d] * jnp.where(sub < d, r_prev, r_cur)
        xc_groups.append(xc_grp)
        prev_rot = rot
    xc = jnp.concatenate(xc_groups, axis=0)
    tail_ref[...] = x[tt - SUBLANES:tt]

    a, u = _rg_gates(xc, wgate_ref, ba_ref[...], bx_ref[...], lam_ref[...])
    sub = lax.broadcasted_iota(jnp.int32, (SUBLANES, a.shape[1]), 0)
    h_prev = jnp.broadcast_to(hc_ref[...], sub.shape)
    groups = []
    for k in range(tt // SUBLANES):
        rows = slice(k * SUBLANES, (k + 1) * SUBLANES)
        a_grp, u_grp = a[rows], u[rows]
        d = 1
        while d < SUBLANES:
            inside = sub >= d
            u_grp = jnp.where(inside, a_grp * pltpu.roll(u_grp, d, 0) + u_grp, u_grp)
            a_grp = jnp.where(inside, a_grp * pltpu.roll(a_grp, d, 0), a_grp)
            d *= 2
        h_grp = a_grp * h_prev + u_grp
        groups.append(h_grp)
        h_prev = jnp.broadcast_to(h_grp[SUBLANES - 1:SUBLANES], h_grp.shape)
    hs = jnp.concatenate(groups, axis=0)
    hc_ref[...] = hs[tt - 1:tt]
    hlast_ref[...] = hs[tt - 1:tt]
    o_ref[...] = _merge(h1_ref[...], hs, gr_ref[...], ga_ref[...], gb_ref[...], oat_ref[...],
                        wbra_ref, wbrr_ref, wout_ref, g2_ref[...], b2_ref[...], alpha)


def _mix_weight_specs(d, n_grp, gw):
    return [_const_spec((SUBLANES, d)), _const_spec((1, d)), _const_spec((n_grp, gw, 2 * gw)),
            _const_spec((1, d)), _const_spec((1, d)), _const_spec((1, d)),
            _const_spec((Q_W, d)), _const_spec((d, d)), _const_spec((d, d)),
            _const_spec((1, d)), _const_spec((1, d))]


def _mix_prompt(xr, gr, ga, gb, o_attn, h1, mixw, *, tt, alpha):
    b, t, d = xr.shape
    n_grp, gw = mixw[2].shape[:2]
    tile = lambda w: pl.BlockSpec((None, tt, w), lambda i, j: (i, j, 0))
    return pl.pallas_call(
        functools.partial(_mix_prompt_body, tt=tt, alpha=alpha),
        grid=(b, t // tt),
        in_specs=[tile(d), tile(d), tile(d), tile(d), tile(Q_W), tile(d)] + _mix_weight_specs(d, n_grp, gw),
        out_specs=[tile(d), pl.BlockSpec((None, 1, d), lambda i, j: (i, 0, 0))],
        out_shape=[jax.ShapeDtypeStruct((b, t, d), F32), jax.ShapeDtypeStruct((b, 1, d), F32)],
        scratch_shapes=[pltpu.VMEM((1, d), F32), pltpu.VMEM((SUBLANES, d), F32)],
        compiler_params=_params("parallel", "arbitrary"),
        name="mix_prompt",
    )(xr, gr, ga, gb, o_attn, h1, *mixw)


def _mix_sample_body(xr_ref, c0_ref, c1_ref, c2_ref, h0_ref, gr_ref, ga_ref, gb_ref, oat_ref, h1_ref,
                     convw_ref, convb_ref, wgate_ref, ba_ref, bx_ref, lam_ref, wbra_ref, wbrr_ref, wout_ref,
                     g2_ref, b2_ref, o_ref, hnew_ref, *, alpha):
    convw = convw_ref[...]
    xc = (convb_ref[...] + convw[0:1] * c0_ref[...] + convw[1:2] * c1_ref[...] + convw[2:3] * c2_ref[...]
          + convw[3:4] * xr_ref[...])
    a, u = _rg_gates(xc, wgate_ref, ba_ref[...], bx_ref[...], lam_ref[...])
    hs = a * h0_ref[...] + u
    hnew_ref[...] = hs
    o_ref[...] = _merge(h1_ref[...], hs, gr_ref[...], ga_ref[...], gb_ref[...], oat_ref[...],
                        wbra_ref, wbrr_ref, wout_ref, g2_ref[...], b2_ref[...], alpha)


def _mix_sample(xr, conv_rows, h0, gr, ga, gb, o_attn, h1, mixw, *, alpha):
    b, d = xr.shape
    n_grp, gw = mixw[2].shape[:2]
    full = lambda w: _const_spec((b, w))
    return pl.pallas_call(
        functools.partial(_mix_sample_body, alpha=alpha),
        grid=(1,),
        in_specs=[full(d)] * 8 + [full(Q_W), full(d)] + _mix_weight_specs(d, n_grp, gw),
        out_specs=[full(d), full(d)],
        out_shape=[jax.ShapeDtypeStruct((b, d), F32), jax.ShapeDtypeStruct((b, d), F32)],
        compiler_params=_params("arbitrary"),
        name="mix_sample",
    )(xr, *conv_rows, h0, gr, ga, gb, o_attn, h1, *mixw)


def _head_rows(row, g):
    parts = [row[:, (g * HPG + h) * HEAD_DIM:(g * HPG + h + 1) * HEAD_DIM] for h in range(HPG)]
    return jnp.concatenate(parts + [jnp.zeros((SUBLANES - HPG, HEAD_DIM), F32)], axis=0)


def _sample_scores(qrow, ckv, width, nc, nsel, q_pos):
    nchp = ckv.shape[0]
    cidx = lax.broadcasted_iota(jnp.int32, (1, nchp), 1)
    qrow = qrow * SCALE
    m_c = (cidx * CMP_STRIDE + (CMP_BLOCK - 1) <= q_pos) & (cidx < nc)
    lane = lax.broadcasted_iota(jnp.int32, (1, width), 1)
    jblk = lane // TOKENS_PER_BLOCK
    cur = q_pos // SEL_BLOCK
    on_block = (lane % TOKENS_PER_BLOCK == 0) & (jblk < nsel) & (jblk * SEL_BLOCK <= q_pos)
    bonus = jnp.where((jblk == 0) | (jblk == cur) | (jblk == cur - 1), SEL_BONUS, 0.0)
    oc_parts, score_rows = [], []
    for g in range(N_KV):
        q = _head_rows(qrow, g)
        s_c = _dot_nt(q, ckv[:, g * HEAD_DIM:(g + 1) * HEAD_DIM])
        e_c, d_c = _softmax_parts(s_c, m_c)
        p_c = e_c / d_c
        o_c = _dot(p_c, ckv[:, KV_W + g * HEAD_DIM:KV_W + (g + 1) * HEAD_DIM])
        oc_parts += [o_c[h:h + 1] for h in range(HPG)]
        p_sum = jnp.broadcast_to(jnp.sum(p_c[0:HPG], axis=0, keepdims=True), (SUBLANES, nchp))
        p_sum = jnp.concatenate([p_sum, jnp.zeros((SUBLANES, width - nchp), F32)], axis=1)
        imp = p_sum
        for back in range(1, CMP_RATIO):
            imp = imp + pltpu.roll(p_sum, back, 1)
        for ahead in range(1, TOKENS_PER_BLOCK):
            imp = imp + pltpu.roll(p_sum, width - ahead, 1)
        score_rows.append(jnp.where(on_block, imp[0:1] + bonus, NEG_INF))
    score = jnp.concatenate(score_rows + [jnp.full((SUBLANES - N_KV, width), NEG_INF, F32)], axis=0)
    return jnp.concatenate(oc_parts, axis=1), score


def _top_blocks_body(score_ref, idx_ref):
    score = score_ref[...]
    n, nselp = score.shape
    lane = lax.broadcasted_iota(jnp.int32, (n, nselp), 1)
    lane_out = lax.broadcasted_iota(jnp.int32, (n, LANES), 1)
    picked = jnp.zeros((n, LANES), jnp.int32)
    for r in range(N_SEL):
        m = jnp.max(score, axis=1, keepdims=True)
        j = jnp.min(jnp.where(score == m, lane, nselp), axis=1, keepdims=True)
        picked = jnp.where(lane_out == r, j, picked)
        score = jnp.where(lane == j, NEG_INF, score)
    idx_ref[...] = picked


def _top_blocks(score):
    n, nselp = score.shape
    return pl.pallas_call(
        _top_blocks_body,
        grid=(1,),
        in_specs=[_const_spec((n, nselp))],
        out_specs=_const_spec((n, LANES)),
        out_shape=jax.ShapeDtypeStruct((n, LANES), jnp.int32),
        compiler_params=_params("arbitrary"),
        name="top_blocks",
    )(score)


def _nsa_sample_attend_body(idx_ref, pt_ref, qrot_ref, gates_ref, oc_ref, selnew_ref, winnew_ref, wincache_ref,
                            *refs, q_pos, n_past_blocks):
    del pt_ref
    page_refs, o_ref = refs[:-1], refs[-1]
    b = pl.program_id(0)
    wb = wincache_ref.shape[2]
    qrow = qrot_ref[...] * SCALE
    gates = gates_ref[...]
    oc = oc_ref[...]
    sel_new = selnew_ref[...]
    win_new = winnew_ref[...]
    page_lane = lax.broadcasted_iota(jnp.int32, (1, PAGE_SIZE), 1)
    wpos = (q_pos - wb) + lax.broadcasted_iota(jnp.int32, (1, wb), 1)
    m_w = (q_pos - wpos <= WINDOW) & (wpos >= 0) & (wpos <= q_pos)
    cur = q_pos // SEL_BLOCK
    out_parts = []
    for g in range(N_KV):
        q = _head_rows(qrow, g)
        feat = slice(g * HEAD_DIM, (g + 1) * HEAD_DIM)
        kcol = slice(g * HEAD_DIM, (g + 1) * HEAD_DIM)
        vcol = slice(KV_W + g * HEAD_DIM, KV_W + (g + 1) * HEAD_DIM)
        kts, vts, kps = [], [], []
        new_chosen = None
        for n in range(N_SEL):
            j = idx_ref[b, g * N_SEL + n]
            page = page_refs[g * N_SEL + n]
            kts.append(page[0, feat, :])
            vts.append(page[1, feat, :])
            in_block = (page_lane // SEL_BLOCK == j % BLOCKS_PER_PAGE) & (j < n_past_blocks)
            kps.append(jnp.where(in_block, (j // BLOCKS_PER_PAGE) * PAGE_SIZE + page_lane, q_pos + 1))
            hit = j == cur
            new_chosen = hit if new_chosen is None else (new_chosen | hit)
        s_s = _dot(q, jnp.concatenate(kts, axis=1))
        s_new = jnp.where(new_chosen, jnp.sum(q * sel_new[:, kcol], axis=1, keepdims=True), NEG_INF)
        s_s = jnp.where(jnp.concatenate(kps, axis=1) <= q_pos, s_s, NEG_INF)
        m = jnp.maximum(jnp.max(s_s, axis=1, keepdims=True), s_new)
        m = jnp.where(m > NEG_INF, m, 0.0)
        e_s = jnp.exp(s_s - m)
        e_new = jnp.exp(s_new - m)
        d_s = jnp.sum(e_s, axis=1, keepdims=True) + e_new
        d_s = jnp.where(d_s > 0, d_s, 1.0)
        o_s = (_dot_nt(e_s, jnp.concatenate(vts, axis=1)) + e_new * sel_new[:, vcol]) / d_s
        s_w = jnp.where(m_w, _dot(q, wincache_ref[0, feat, :]), NEG_INF)
        s_wn = jnp.sum(q * win_new[:, kcol], axis=1, keepdims=True)
        m = jnp.maximum(jnp.max(s_w, axis=1, keepdims=True), s_wn)
        e_w = jnp.exp(s_w - m)
        e_wn = jnp.exp(s_wn - m)
        d_w = jnp.sum(e_w, axis=1, keepdims=True) + e_wn
        o_w = (_dot_nt(e_w, wincache_ref[1, feat, :]) + e_wn * win_new[:, vcol]) / d_w
        for h in range(HPG):
            hd = g * HPG + h
            out_parts.append(gates[:, 3 * hd:3 * hd + 1] * oc[:, hd * HEAD_DIM:(hd + 1) * HEAD_DIM]
                             + gates[:, 3 * hd + 1:3 * hd + 2] * o_s[h:h + 1]
                             + gates[:, 3 * hd + 2:3 * hd + 3] * o_w[h:h + 1])
    o_ref[...] = jnp.concatenate(out_parts, axis=1)


def _nsa_sample_attend(idx, page_table, q_rot, gates, o_c, sel_new, win_new, win_cache_t, sel_cache_t,
                       *, q_pos, n_past_blocks):
    b = q_rot.shape[0]
    wb = win_cache_t.shape[3]

    def page_spec(k):
        def index(i, idx_ref, pt_ref):
            j = jnp.minimum(idx_ref[i, k], n_past_blocks - 1)
            return (pt_ref[i, j // BLOCKS_PER_PAGE], 0, 0, 0)
        return pl.BlockSpec((None, 2, KV_W, PAGE_SIZE), index)

    one = lambda w: pl.BlockSpec((None, 1, w), lambda i, idx_ref, pt_ref: (i, 0, 0))
    return pl.pallas_call(
        functools.partial(_nsa_sample_attend_body, q_pos=q_pos, n_past_blocks=n_past_blocks),
        grid_spec=pltpu.PrefetchScalarGridSpec(
            num_scalar_prefetch=2,
            grid=(b,),
            in_specs=[one(Q_W), one(GATE_PAD), one(Q_W), one(ROW_W), one(ROW_W),
                      pl.BlockSpec((None, 2, KV_W, wb), lambda i, idx_ref, pt_ref: (i, 0, 0, 0))]
            + [page_spec(k) for k in range(N_KV * N_SEL)],
            out_specs=one(Q_W),
        ),
        out_shape=jax.ShapeDtypeStruct((b, 1, Q_W), F32),
        compiler_params=_params("arbitrary"),
        name="nsa_sample_attend",
    )(idx, page_table, q_rot, gates, o_c, sel_new, win_new, win_cache_t, *([sel_cache_t] * (N_KV * N_SEL)))


def _rope_tables(pos):
    half = HEAD_DIM // 2
    freqs = ROPE_THETA ** (-jnp.arange(half, dtype=F32) / half)
    ang = pos.astype(F32)[:, None] * freqs[None, :]
    cos, sin = jnp.cos(ang), jnp.sin(ang)
    reps = LANES // HEAD_DIM
    return jnp.tile(jnp.concatenate([cos, cos], axis=1), (1, reps)), jnp.tile(jnp.concatenate([-sin, sin], axis=1), (1, reps))


def _cover(nc, nsel, rows, cols):
    start = np.arange(rows)[:, None] * CMP_STRIDE
    j = np.arange(cols)[None, :]
    hit = (start < (j + 1) * SEL_BLOCK) & (start + CMP_BLOCK > j * SEL_BLOCK) & (np.arange(rows)[:, None] < nc) & (j < nsel)
    return jnp.asarray(hit.astype(np.float32))


def _cmp_chunk_weight(w1_k, w1_v):
    per = jnp.stack([w.reshape(CMP_RATIO, CMP_STRIDE, HEAD_DIM, CMP_HID) for w in (w1_k, w1_v)])
    eye = jnp.eye(N_KV, dtype=F32)
    big = jnp.einsum("ab,krsdh->ksadbrh", eye, per)
    return big.reshape(2, CMP_STRIDE, KV_W, N_KV * CMP_RATIO * CMP_HID).astype(BF16)


def _rg_gate_weight(w_a, w_x):
    nb, bw, _ = w_a.shape
    per = MXU_DIM // bw
    n_grp = nb // per
    eye = jnp.eye(per, dtype=F32)

    def group(w):
        w = w.reshape(n_grp, per, bw, bw)
        return jnp.einsum("pq,gpde->gpdqe", eye, w).reshape(n_grp, per * bw, per * bw)

    return jnp.concatenate([group(w_a), group(w_x)], axis=2).astype(BF16)


def _feature_major(kv_rows):
    lead = kv_rows.shape[:-4]
    rows = kv_rows.shape[-4]
    nd = len(lead)
    perm = tuple(range(nd)) + (nd + 1, nd + 2, nd + 3, nd)
    return jnp.transpose(kv_rows, perm).reshape(lead + (2, KV_W, rows))


def _row_major_view(kv_t, b, rows):
    return jnp.transpose(kv_t.reshape(b, 2, N_KV, HEAD_DIM, rows), (0, 4, 1, 2, 3))


def kernel(x_prompt, x_sample, cache_cmp_kv, cache_sel_kv, cache_win_kv, state_conv, state_h, page_table, ffn1_w_gate, ffn1_w_up, ffn1_w_down, ln1_g, ln1_b, w_in, cmp_pe_k, cmp_w1_k, cmp_w2_k, cmp_pe_v, cmp_w1_v, cmp_w2_v, conv_w, conv_b, rg_w_a, rg_b_a, rg_w_x, rg_b_x, rg_lam, w_br_attn, w_br_rnn, w_out, ln2_g, ln2_b, ffn2_w_gate, ffn2_w_up, ffn2_w_down, ln3_g, ln3_b):
    bp, tp, d = x_prompt.shape
    bs, ts, _ = x_sample.shape
    depth = w_in.shape[0]
    d_rnn = conv_w.shape[2]
    n_pages = page_table.shape[1]
    past_len = n_pages * PAGE_SIZE
    assert ts == 1 and tp % CMP_STRIDE == 0 and tp % SEL_BLOCK == 0
    alpha = (2.0 * depth) ** 0.25
    wb = cache_win_kv.shape[2]
    wbp = min(WINDOW, tp)

    nch_p = tp // CMP_STRIDE
    nc_p = nch_p - CMP_RATIO + 1
    nsel_p = tp // SEL_BLOCK
    nch_s = past_len // CMP_STRIDE
    nc_s = nch_s - CMP_RATIO + 1
    nsel_s = -(-(past_len + ts) // SEL_BLOCK)
    n_past_blocks = past_len // SEL_BLOCK
    assert nsel_s >= N_SEL and n_pages % PAGES_PER_STEP == 0

    tm = tt = min(ROW_TILE, tp)
    tf = min(FFN_TILE, tp)
    tq = min(ATTN_TILE, tp)

    cos_p, sin_p = _rope_tables(jnp.arange(tp))
    cos_s, sin_s = _rope_tables(jnp.full((bs,), past_len))
    cover_p = _cover(nc_p, nsel_p, nch_p, nsel_p).T
    expand_p = jnp.asarray((np.arange(tp)[:, None] // SEL_BLOCK == np.arange(nsel_p)[None, :]).astype(np.float32)).astype(BF16)

    xp = x_prompt.reshape(bp * tp, d)
    xs = x_sample.reshape(bs * ts, d)
    outs = [[] for _ in range(10)]
    for l in range(depth):
        row = lambda v: v[l].reshape(1, -1)
        ffn1 = (ffn1_w_gate[l].astype(BF16), ffn1_w_up[l].astype(BF16), ffn1_w_down[l].astype(BF16), row(ln1_g), row(ln1_b))
        ffn2 = (ffn2_w_gate[l].astype(BF16), ffn2_w_up[l].astype(BF16), ffn2_w_down[l].astype(BF16), row(ln3_g), row(ln3_b))
        n_a = Q_W + 6 * KV_W + 3 * N_HEADS
        w_a = jnp.pad(w_in[l][:, :n_a], ((0, 0), (0, GATE_PAD - 3 * N_HEADS))).astype(BF16)
        w_b = w_in[l][:, n_a:]
        w_b = jnp.concatenate([w_b[:, :d_rnn], 0.5 * w_b[:, d_rnn:]], axis=1).astype(BF16)
        w_cmp = _cmp_chunk_weight(cmp_w1_k[l], cmp_w1_v[l])
        cmpw = (cmp_pe_k[l].reshape(1, -1), cmp_w1_k[l], cmp_w2_k[l], cmp_pe_v[l].reshape(1, -1), cmp_w1_v[l], cmp_w2_v[l])
        mixw = (jnp.pad(conv_w[l], ((0, SUBLANES - CONV_W), (0, 0))), row(conv_b),
                _rg_gate_weight(0.5 * rg_w_a[l], 0.5 * rg_w_x[l]), 0.5 * row(rg_b_a), 0.5 * row(rg_b_x), row(rg_lam),
                (0.5 * w_br_attn[l]).astype(BF16), (0.5 * w_br_rnn[l]).astype(BF16),
                w_out[l].astype(BF16), row(ln2_g), row(ln2_b))

        h1 = _ffn_half_step(xp, *ffn1, alpha=alpha, tm=tf)
        (cmp, sel_k, win_k, xr, gr, ga, gb, q_raw_t, q_rot_t, gates_t, cmp_t, sel_t, win_t) = _in_proj(
            h1, cos_p, sin_p, w_a, w_b, tm=tm, seq_len=tp, feature_major=True)
        b3 = lambda a: a.reshape(bp, tp, a.shape[-1])
        p1 = _cmp_stage1(b3(cmp), w_cmp)
        ckv, ckv_t = _cmp_finish(p1, *cmpw, nc=nc_p)
        o_attn = _nsa_prompt(q_raw_t, q_rot_t, gates_t, ckv, ckv_t, b3(sel_k), sel_t, b3(win_k), win_t,
                             cover_p, expand_p, tq=tq, nc=nc_p)
        h2, h_last = _mix_prompt(b3(xr), b3(gr), b3(ga), b3(gb), o_attn, b3(h1), mixw, tt=tt, alpha=alpha)
        xp = _ffn_half_step(h2.reshape(bp * tp, d), *ffn2, alpha=alpha, tm=tf)
        outs[0].append(_row_major_view(cmp_t, bp, tp))
        outs[2].append(_row_major_view(sel_t, bp, tp))
        outs[4].append(_row_major_view(win_t[:, :, tp - wbp:], bp, wbp))
        outs[6].append(b3(xr)[:, tp - (CONV_W - 1):])
        outs[8].append(h_last.reshape(bp, d_rnn))

        h1s = _ffn_half_step(xs, *ffn1, alpha=alpha, tm=bs)
        q_raw, q_rot, cmp, sel, win, gates, xr, gr, ga, gb = _in_proj(
            h1s, cos_s, sin_s, w_a, w_b, tm=bs, seq_len=bs, feature_major=False)
        b1 = lambda a: a.reshape(bs, 1, a.shape[-1])
        q_pos = past_len
        o_c, score = _cmp_sample(_feature_major(cache_cmp_kv[l]), page_table, w_cmp, cmpw, b1(q_raw),
                                 nc=nc_s, nsel=nsel_s, q_pos=q_pos)
        lanes = _top_blocks(score.reshape(bs * SUBLANES, score.shape[-1]))
        idx = lanes.reshape(bs, SUBLANES, LANES)[:, :N_KV, :N_SEL].reshape(bs, N_KV * N_SEL) // TOKENS_PER_BLOCK
        o_attn = _nsa_sample_attend(idx, page_table, b1(q_rot), b1(gates), o_c, b1(sel), b1(win),
                                    _feature_major(cache_win_kv[l]), _feature_major(cache_sel_kv[l]),
                                    q_pos=q_pos, n_past_blocks=n_past_blocks)
        conv_rows = [state_conv[l][:, k] for k in range(CONV_W - 1)]
        h2s, h_new = _mix_sample(xr, conv_rows, state_h[l], gr, ga, gb, o_attn.reshape(bs, Q_W), h1s, mixw, alpha=alpha)
        xs = _ffn_half_step(h2s, *ffn2, alpha=alpha, tm=bs)
        outs[1].append(cmp.reshape(bs, ts, 2, N_KV, HEAD_DIM))
        outs[3].append(sel.reshape(bs, ts, 2, N_KV, HEAD_DIM))
        win_all = jnp.concatenate([cache_win_kv[l], win.reshape(bs, ts, 2, N_KV, HEAD_DIM)], axis=1)
        outs[5].append(win_all[:, ts:])
        outs[7].append(jnp.concatenate([state_conv[l], xr[:, None, :]], axis=1)[:, ts:])
        outs[9].append(h_new)

    stacked = [jnp.stack(o) for o in outs]
    cmp_p, cmp_s, sel_p, sel_s, win_p, win_s, conv_p, conv_s, h_p, h_s = stacked
    return (xp.reshape(bp, tp, d), xs.reshape(bs, ts, d), cmp_p, cmp_s, sel_p, sel_s, win_p, win_s,
            conv_p, conv_s, h_p, h_s)
```
